```python
import math
import jax, jax.numpy as jnp
from jax import lax
import numpy as np

D_MODEL = 1024
BATCH = 8
SEQ = 2048
DEPTH = 1
DEC_BATCH = 128
DEC_SEQ = 4
PAST_LEN = 16384
PAGE_SIZE = 128

W_MIX = D_MODEL
W_SSM = W_MIX // 2
W_CONV = W_MIX - W_SSM
GROUP_P = 16
N_GROUPS = W_SSM // GROUP_P
N_STATE = 64
CONV_K = 31
IN_COLS = 2 * W_SSM + 3 * W_CONV
ALPHA = (2.0 * DEPTH) ** 0.25
BETA = (8.0 * DEPTH) ** -0.25
LN_EPS = 1e-5
DT_MIN = 0.001
DT_MAX = 0.1

kernel_name = "hymba_s5_conformer_conv_deepnorm_step"


def _layernorm(x, g, b):
    xf = x.astype(jnp.float32)
    mu = jnp.mean(xf, axis=-1, keepdims=True)
    var = jnp.mean(jnp.square(xf - mu), axis=-1, keepdims=True)
    y = (xf - mu) * lax.rsqrt(var + LN_EPS) * g.astype(jnp.float32) + b.astype(jnp.float32)
    return y.astype(x.dtype)


def _scan_combine(e1, e2):
    a1, b1 = e1
    a2, b2 = e2
    return a1 * a2, a2 * b1 + b2


def _s5(u, h0, lam_re, lam_im, log_dt, b_re, b_im, c_re, c_im, d_skip):
    bsz, length, _ = u.shape
    f32 = jnp.float32
    uf = u.astype(f32).reshape(bsz, length, N_GROUPS, GROUP_P)
    lam = lax.complex(lam_re.astype(f32), lam_im.astype(f32))
    dt = jnp.exp(log_dt.astype(f32))[:, None]
    lam_bar = jnp.exp(lam * dt)
    bmat = lax.complex(b_re.astype(f32), b_im.astype(f32))
    b_bar = ((lam_bar - 1.0) / lam)[..., None] * bmat
    bu = jnp.einsum('blgp,gnp->blgn', uf.astype(jnp.complex64), b_bar)
    bu = bu.at[:, 0].add(lam_bar[None] * h0)
    a = jnp.broadcast_to(lam_bar, bu.shape)
    _, h = lax.associative_scan(_scan_combine, (a, bu), axis=1)
    cmat = lax.complex(c_re.astype(f32), c_im.astype(f32))
    y = jnp.real(jnp.einsum('blgn,gpn->blgp', h, cmat)) + uf * d_skip.astype(f32).reshape(N_GROUPS, GROUP_P)
    return y.reshape(bsz, length, W_SSM), h[:, -1]


def _conformer_conv(a, b, buf, w_dw, b_dw, g_ln, b_ln, w_pw2, b_pw2):
    v = a * jax.nn.sigmoid(b)
    full = jnp.concatenate([buf.astype(v.dtype), v], axis=1)
    new_buf = full[:, -(CONV_K - 1):]
    out = lax.conv_general_dilated(
        full, w_dw.astype(full.dtype)[:, None, :], window_strides=(1,), padding='VALID',
        dimension_numbers=('NWC', 'WIO', 'NWC'), feature_group_count=W_CONV) + b_dw
    out = _layernorm(out, g_ln, b_ln)
    out = jax.nn.silu(out)
    out = out @ w_pw2 + b_pw2
    return out, new_buf


def _layer(x, h0, conv_buf, w_in, b_in, lam_re, lam_im, log_dt, b_re, b_im, c_re, c_im,
           d_skip, w_glu, b_glu, w_dw, b_dw, g_conv_ln, b_conv_ln, w_pw2, b_pw2,
           w_out, g_post, b_post):
    z = x @ w_in + b_in
    u_s, g_s, a_c, b_c, g_c = jnp.split(
        z, [W_SSM, 2 * W_SSM, 2 * W_SSM + W_CONV, 2 * W_SSM + 2 * W_CONV], axis=-1)
    y_s, h_last = _s5(u_s, h0, lam_re, lam_im, log_dt, b_re, b_im, c_re, c_im, d_skip)
    s = jax.nn.gelu(y_s).astype(x.dtype)
    s = s * jax.nn.sigmoid(s @ w_glu + b_glu)
    s = s * jax.nn.silu(g_s)
    c, new_buf = _conformer_conv(a_c, b_c, conv_buf, w_dw, b_dw, g_conv_ln, b_conv_ln, w_pw2, b_pw2)
    c = c * jax.nn.silu(g_c)
    mix = jnp.concatenate([s, c.astype(x.dtype)], axis=-1) @ w_out
    y = _layernorm(ALPHA * x + mix, g_post, b_post)
    return y, h_last, new_buf


def setup_inputs(seed: int = 0) -> dict:
    key = jax.random.key(seed)
    ks = jax.random.split(key, 32)
    f32 = jnp.float32
    nrm = lambda k, shape, s: (jax.random.normal(k, shape, f32) * s)
    n_idx = jnp.arange(N_STATE, dtype=f32)
    lam_re = -0.5 * jnp.ones((DEPTH, N_GROUPS, N_STATE), f32) + nrm(ks[5], (DEPTH, N_GROUPS, N_STATE), 0.01)
    lam_im = jnp.pi * n_idx[None, None, :] + nrm(ks[6], (DEPTH, N_GROUPS, N_STATE), 0.01)
    log_dt = jax.random.uniform(ks[7], (DEPTH, N_GROUPS), f32, math.log(DT_MIN), math.log(DT_MAX))
    bscale = (2.0 * GROUP_P) ** -0.5
    cscale = (2.0 * N_STATE) ** -0.5
    return {
        "x_prompt": nrm(ks[0], (BATCH, SEQ, D_MODEL), 1.0),
        "x_sample": nrm(ks[1], (DEC_BATCH, DEC_SEQ, D_MODEL), 1.0),
        "state_ssm_re": nrm(ks[2], (DEPTH, DEC_BATCH, N_GROUPS, N_STATE), 0.5),
        "state_ssm_im": nrm(ks[3], (DEPTH, DEC_BATCH, N_GROUPS, N_STATE), 0.5),
        "state_conv": nrm(ks[4], (DEPTH, DEC_BATCH, CONV_K - 1, W_CONV), 1.0),
        "w_in": nrm(ks[8], (DEPTH, D_MODEL, IN_COLS), D_MODEL ** -0.5),
        "b_in": nrm(ks[9], (DEPTH, IN_COLS), 0.01),
        "lam_re": lam_re,
        "lam_im": lam_im,
        "log_dt": log_dt,
        "b_re": nrm(ks[10], (DEPTH, N_GROUPS, N_STATE, GROUP_P), bscale),
        "b_im": nrm(ks[11], (DEPTH, N_GROUPS, N_STATE, GROUP_P), bscale),
        "c_re": nrm(ks[12], (DEPTH, N_GROUPS, GROUP_P, N_STATE), cscale),
        "c_im": nrm(ks[13], (DEPTH, N_GROUPS, GROUP_P, N_STATE), cscale),
        "d_skip": nrm(ks[14], (DEPTH, W_SSM), 1.0),
        "w_glu": nrm(ks[15], (DEPTH, W_SSM, W_SSM), W_SSM ** -0.5),
        "b_glu": nrm(ks[16], (DEPTH, W_SSM), 0.01),
        "w_dw": nrm(ks[17], (DEPTH, CONV_K, W_CONV), CONV_K ** -0.5),
        "b_dw": nrm(ks[18], (DEPTH, W_CONV), 0.01),
        "g_conv_ln": 1.0 + nrm(ks[19], (DEPTH, W_CONV), 0.02),
        "b_conv_ln": nrm(ks[20], (DEPTH, W_CONV), 0.01),
        "w_pw2": nrm(ks[21], (DEPTH, W_CONV, W_CONV), W_CONV ** -0.5),
        "b_pw2": nrm(ks[22], (DEPTH, W_CONV), 0.01),
        "w_out": nrm(ks[23], (DEPTH, W_MIX, D_MODEL), BETA * W_MIX ** -0.5),
        "g_post": 1.0 + nrm(ks[24], (DEPTH, D_MODEL), 0.02),
        "b_post": nrm(ks[25], (DEPTH, D_MODEL), 0.01),
    }


def reference(x_prompt, x_sample, state_ssm_re, state_ssm_im, state_conv, w_in, b_in,
              lam_re, lam_im, log_dt, b_re, b_im, c_re, c_im, d_skip, w_glu, b_glu,
              w_dw, b_dw, g_conv_ln, b_conv_ln, w_pw2, b_pw2, w_out, g_post, b_post):
    f32 = jnp.float32
    hp = x_prompt
    hs = x_sample
    re_p, im_p, cv_p, re_s, im_s, cv_s = [], [], [], [], [], []
    for layer in range(DEPTH):
        params = (w_in[layer], b_in[layer], lam_re[layer], lam_im[layer], log_dt[layer],
                  b_re[layer], b_im[layer], c_re[layer], c_im[layer], d_skip[layer],
                  w_glu[layer], b_glu[layer], w_dw[layer], b_dw[layer], g_conv_ln[layer],
                  b_conv_ln[layer], w_pw2[layer], b_pw2[layer], w_out[layer],
                  g_post[layer], b_post[layer])
        h0_p = jnp.zeros((hp.shape[0], N_GROUPS, N_STATE), jnp.complex64)
        buf_p = jnp.zeros((hp.shape[0], CONV_K - 1, W_CONV), hp.dtype)
        hp, hl_p, nb_p = _layer(hp, h0_p, buf_p, *params)
        h0_s = lax.complex(state_ssm_re[layer].astype(f32), state_ssm_im[layer].astype(f32))
        hs, hl_s, nb_s = _layer(hs, h0_s, state_conv[layer], *params)
        re_p.append(jnp.real(hl_p)); im_p.append(jnp.imag(hl_p)); cv_p.append(nb_p)
        re_s.append(jnp.real(hl_s)); im_s.append(jnp.imag(hl_s)); cv_s.append(nb_s)
    return (hp, hs,
            jnp.stack(re_p), jnp.stack(im_p), jnp.stack(cv_p),
            jnp.stack(re_s), jnp.stack(im_s), jnp.stack(cv_s))
```

```python
import functools

import jax
import jax.numpy as jnp
from jax import lax
from jax.experimental import pallas as pl
from jax.experimental.pallas import tpu as pltpu

D_MODEL = 1024
W_SSM = 512
W_CONV = 512
GROUP_P = 16
N_GROUPS = 32
N_STATE = 64
CONV_K = 31
IN_COLS = 2 * W_SSM + 3 * W_CONV
LN_EPS = 1e-5

LANES = 128
SUBLANES = 8
GROUPS_PER_BLOCK = LANES // GROUP_P
N_BLOCKS = N_GROUPS // GROUPS_PER_BLOCK
BLOCK_STATE = GROUPS_PER_BLOCK * N_STATE
N_STATE_ALL = N_GROUPS * N_STATE
CONV_ROWS = 32
VMEM_LIMIT_BYTES = 58 * 1024 * 1024

F32 = jnp.float32
BF16 = jnp.bfloat16


def _layernorm(x, g, b):
    mu = jnp.mean(x, axis=-1, keepdims=True)
    xc = x - mu
    var = jnp.mean(xc * xc, axis=-1, keepdims=True)
    return xc * lax.rsqrt(var + LN_EPS) * g + b


def _prep_kernel(lre_e, lim_e, ldt_e, bre_e, bim_e, lre_r, lim_r, ldt_r, cre_e, cim_e,
                 bblk_ref, lbre_ref, lbim_ref, cre_ref, cim_ref):
    def zoh(lr, li, ldt):
        dt = jnp.exp(ldt)
        mag = jnp.exp(lr * dt)
        ang = li * dt
        return mag * jnp.cos(ang), mag * jnp.sin(ang)

    lr, li = lre_e[...], lim_e[...]
    ar, ai = zoh(lr, li, ldt_e[...])
    nr, ni = ar - 1.0, ai
    den = lr * lr + li * li
    cr = (nr * lr + ni * li) / den
    ci = (ni * lr - nr * li) / den
    br, bi = bre_e[...], bim_e[...]
    bbr = cr * br - ci * bi
    bbi = cr * bi + ci * br
    shape = bbr.shape
    row_g = lax.broadcasted_iota(jnp.int32, shape, 0) // GROUP_P % GROUPS_PER_BLOCK
    col_g = lax.broadcasted_iota(jnp.int32, shape, 1) // N_STATE
    keep = row_g == col_g
    bblk_ref[:, 0:BLOCK_STATE] = jnp.where(keep, bbr, 0.0).astype(BF16)
    bblk_ref[:, BLOCK_STATE:2 * BLOCK_STATE] = jnp.where(keep, bbi, 0.0).astype(BF16)

    a_r, a_i = zoh(lre_r[...], lim_r[...], ldt_r[...])
    lbre_ref[...] = a_r
    lbim_ref[...] = a_i

    cshape = cre_e.shape
    crow_g = lax.broadcasted_iota(jnp.int32, cshape, 0) // N_STATE % GROUPS_PER_BLOCK
    ccol_g = lax.broadcasted_iota(jnp.int32, cshape, 1) // GROUP_P
    ckeep = crow_g == ccol_g
    cre_ref[...] = jnp.where(ckeep, cre_e[...], 0.0).astype(BF16)
    cim_ref[...] = jnp.where(ckeep, -cim_e[...], 0.0).astype(BF16)


def _layer_kernel(nb, tl, n_steps, alpha,
                  x_ref, h0re_ref, h0im_ref, cbuf0_ref, win_ref, bin_ref, lre_ref, lim_ref,
                  bblk_ref, cre_ref, cim_ref, dskip_ref, wglu_ref, bglu_ref, wdw_ref, bdw_ref,
                  gcl_ref, bcl_ref, wpw_ref, bpw_ref, wout_ref, gpost_ref, bpost_ref,
                  y_ref, hre_ref, him_ref, cbuf_ref,
                  z_ref, bu_ref, ys_ref, full_ref, mix_ref, act_ref):
    rows = nb * tl
    hist = (CONV_K - 1) * nb

    def init_state():
        hre_ref[...] = h0re_ref[...]
        him_ref[...] = h0im_ref[...]
        full_ref[0:hist, :] = cbuf0_ref[...]

    if n_steps > 1:
        pl.when(pl.program_id(0) == 0)(init_state)
    else:
        init_state()

    z_ref[...] = jnp.dot(x_ref[...].astype(BF16), win_ref[...],
                         preferred_element_type=F32) + bin_ref[...]

    for k in range(N_BLOCKS):
        cs = slice(LANES * k, LANES * (k + 1))
        ss = slice(BLOCK_STATE * k, BLOCK_STATE * (k + 1))
        u_k = z_ref[:, cs]
        bu_ref[...] = jnp.dot(u_k.astype(BF16), bblk_ref[cs, :], preferred_element_type=F32)
        lr = lre_ref[:, ss]
        li = lim_ref[:, ss]
        for rb in range(nb // SUBLANES):
            rs = slice(SUBLANES * rb, SUBLANES * (rb + 1))

            def step(t, carry, rb=rb):
                hr, hi = carry
                row = pl.multiple_of(t * nb + SUBLANES * rb, SUBLANES)
                br = bu_ref[pl.ds(row, SUBLANES), 0:BLOCK_STATE]
                bi = bu_ref[pl.ds(row, SUBLANES), BLOCK_STATE:2 * BLOCK_STATE]
                nr = lr * hr - li * hi + br
                ni = lr * hi + li * hr + bi
                bu_ref[pl.ds(row, SUBLANES), 0:BLOCK_STATE] = nr
                bu_ref[pl.ds(row, SUBLANES), BLOCK_STATE:2 * BLOCK_STATE] = ni
                return nr, ni

            hr, hi = lax.fori_loop(0, tl, step, (hre_ref[rs, ss], him_ref[rs, ss]),
                                   unroll=min(tl, 4))
            hre_ref[rs, ss] = hr
            him_ref[rs, ss] = hi
        y_k = (jnp.dot(bu_ref[:, 0:BLOCK_STATE].astype(BF16), cre_ref[ss, :],
                       preferred_element_type=F32)
               + jnp.dot(bu_ref[:, BLOCK_STATE:2 * BLOCK_STATE].astype(BF16), cim_ref[ss, :],
                         preferred_element_type=F32)
               + u_k * dskip_ref[:, cs])
        ys_ref[:, cs] = y_k

    sg = jax.nn.gelu(ys_ref[...])
    glu = jnp.dot(sg.astype(BF16), wglu_ref[...], preferred_element_type=F32) + bglu_ref[...]
    s = sg * jax.nn.sigmoid(glu) * jax.nn.silu(z_ref[:, W_SSM:2 * W_SSM])
    mix_ref[:, 0:W_SSM] = s.astype(BF16)

    c0 = 2 * W_SSM
    v = z_ref[:, c0:c0 + W_CONV] * jax.nn.sigmoid(z_ref[:, c0 + W_CONV:c0 + 2 * W_CONV])
    full_ref[hist:hist + rows, :] = v
    bdw = bdw_ref[...]
    gcl = gcl_ref[...]
    bcl = bcl_ref[...]
    sub = CONV_ROWS // SUBLANES

    def conv_block(i, carry):
        base = pl.multiple_of(i * CONV_ROWS, CONV_ROWS)
        accs = [None] * sub
        for tap in range(CONV_K):
            wk = wdw_ref[tap]
            for r in range(sub):
                term = wk * full_ref[pl.ds(base + SUBLANES * r + tap * nb, SUBLANES), :]
                accs[r] = term if accs[r] is None else accs[r] + term
        c = jnp.concatenate(accs, axis=0) + bdw
        c = jax.nn.silu(_layernorm(c, gcl, bcl))
        act_ref[pl.ds(base, CONV_ROWS), :] = c.astype(BF16)
        return carry

    lax.fori_loop(0, rows // CONV_ROWS, conv_block, 0)
    cbuf_ref[...] = full_ref[tl * nb:tl * nb + hist, :]
    if n_steps > 1:
        full_ref[0:hist, :] = full_ref[tl * nb:tl * nb + hist, :]

    cpw = jnp.dot(act_ref[...], wpw_ref[...], preferred_element_type=F32) + bpw_ref[...]
    cg = cpw * jax.nn.silu(z_ref[:, c0 + 2 * W_CONV:c0 + 3 * W_CONV])
    mix_ref[:, W_SSM:W_SSM + W_CONV] = cg.astype(BF16)

    mix = jnp.dot(mix_ref[...], wout_ref[...], preferred_element_type=F32)
    y_ref[...] = _layernorm(alpha * x_ref[...] + mix, gpost_ref[...], bpost_ref[...])


def _layer_call(x_rows, h0re, h0im, cbuf0, consts, *, nb, tl, n_steps, alpha, name):
    rows = nb * tl
    hist = (CONV_K - 1) * nb
    assert x_rows.shape == (rows * n_steps, D_MODEL)
    assert nb % SUBLANES == 0 and rows % CONV_ROWS == 0
    assert n_steps == 1 or tl >= CONV_K - 1
    out_shape = (
        jax.ShapeDtypeStruct((rows * n_steps, D_MODEL), F32),
        jax.ShapeDtypeStruct((nb, N_STATE_ALL), F32),
        jax.ShapeDtypeStruct((nb, N_STATE_ALL), F32),
        jax.ShapeDtypeStruct((hist, W_CONV), F32),
    )
    scratch = [
        pltpu.VMEM((rows, IN_COLS), F32),
        pltpu.VMEM((rows, 2 * BLOCK_STATE), F32),
        pltpu.VMEM((rows, W_SSM), F32),
        pltpu.VMEM((hist + rows, W_CONV), F32),
        pltpu.VMEM((rows, W_SSM + W_CONV), BF16),
        pltpu.VMEM((rows, W_CONV), BF16),
    ]
    operands = (x_rows, h0re, h0im, cbuf0) + tuple(consts)
    kernel = functools.partial(_layer_kernel, nb, tl, n_steps, alpha)
    if n_steps == 1:
        return pl.pallas_call(
            kernel, out_shape=out_shape, scratch_shapes=scratch, name=name,
            compiler_params=pltpu.CompilerParams(vmem_limit_bytes=VMEM_LIMIT_BYTES),
        )(*operands)

    def whole(a):
        zeros = (0,) * a.ndim
        return pl.BlockSpec(a.shape, lambda i: zeros)

    in_specs = [pl.BlockSpec((rows, D_MODEL), lambda i: (i, 0))] + [whole(a) for a in operands[1:]]
    out_specs = (
        pl.BlockSpec((rows, D_MODEL), lambda i: (i, 0)),
        pl.BlockSpec((nb, N_STATE_ALL), lambda i: (0, 0)),
        pl.BlockSpec((nb, N_STATE_ALL), lambda i: (0, 0)),
        pl.BlockSpec((hist, W_CONV), lambda i: (0, 0)),
    )
    return pl.pallas_call(
        kernel, out_shape=out_shape, grid=(n_steps,), in_specs=in_specs, out_specs=out_specs,
        scratch_shapes=scratch, name=name,
        compiler_params=pltpu.CompilerParams(dimension_semantics=("arbitrary",),
                                             vmem_limit_bytes=VMEM_LIMIT_BYTES),
    )(*operands)


def _prep_call(lam_re, lam_im, log_dt, b_re, b_im, c_re, c_im):
    gp = N_GROUPS * GROUP_P
    tile_b = lambda a: jnp.tile(a, (1, GROUPS_PER_BLOCK))
    rep_rows = lambda a: jnp.repeat(a, GROUP_P, axis=0)
    lre_e = tile_b(rep_rows(lam_re))
    lim_e = tile_b(rep_rows(lam_im))
    ldt_e = jnp.broadcast_to(rep_rows(log_dt[:, None]), (gp, BLOCK_STATE))
    bre_e = tile_b(b_re.transpose(0, 2, 1).reshape(gp, N_STATE))
    bim_e = tile_b(b_im.transpose(0, 2, 1).reshape(gp, N_STATE))
    bc8 = lambda a: jnp.broadcast_to(a.reshape(1, N_STATE_ALL), (SUBLANES, N_STATE_ALL))
    lre_r = bc8(lam_re)
    lim_r = bc8(lam_im)
    ldt_r = bc8(jnp.repeat(log_dt, N_STATE))
    cre_e = tile_b(c_re.transpose(0, 2, 1).reshape(N_STATE_ALL, GROUP_P))
    cim_e = tile_b(c_im.transpose(0, 2, 1).reshape(N_STATE_ALL, GROUP_P))
    out_shape = (
        jax.ShapeDtypeStruct((gp, 2 * BLOCK_STATE), BF16),
        jax.ShapeDtypeStruct((SUBLANES, N_STATE_ALL), F32),
        jax.ShapeDtypeStruct((SUBLANES, N_STATE_ALL), F32),
        jax.ShapeDtypeStruct((N_STATE_ALL, LANES), BF16),
        jax.ShapeDtypeStruct((N_STATE_ALL, LANES), BF16),
    )
    return pl.pallas_call(_prep_kernel, out_shape=out_shape, name="s5_prep")(
        lre_e, lim_e, ldt_e, bre_e, bim_e, lre_r, lim_r, ldt_r, cre_e, cim_e)


def kernel(x_prompt, x_sample, state_ssm_re, state_ssm_im, state_conv, w_in, b_in, lam_re, lam_im, log_dt, b_re, b_im, c_re, c_im, d_skip, w_glu, b_glu, w_dw, b_dw, g_conv_ln, b_conv_ln, w_pw2, b_pw2, w_out, g_post, b_post):
    depth = w_in.shape[0]
    alpha = (2.0 * depth) ** 0.25
    bsz, seq, _ = x_prompt.shape
    dbsz, dseq, _ = x_sample.shape
    tl_prompt = 64

    hp = x_prompt.transpose(1, 0, 2).reshape(seq * bsz, D_MODEL)
    hs = x_sample.transpose(1, 0, 2).reshape(dseq * dbsz, D_MODEL)
    outs = {k: [] for k in ("re_p", "im_p", "cv_p", "re_s", "im_s", "cv_s")}
    row = lambda a: a.reshape(1, -1)
    for layer in range(depth):
        bblk, lbre, lbim, cre, cim = _prep_call(
            lam_re[layer], lam_im[layer], log_dt[layer], b_re[layer], b_im[layer],
            c_re[layer], c_im[layer])
        wdw8 = jnp.broadcast_to(w_dw[layer][:, None, :], (CONV_K, SUBLANES, W_CONV))
        consts = (w_in[layer].astype(BF16), row(b_in[layer]), lbre, lbim, bblk, cre, cim,
                  row(d_skip[layer]), w_glu[layer].astype(BF16), row(b_glu[layer]), wdw8,
                  row(b_dw[layer]), row(g_conv_ln[layer]), row(b_conv_ln[layer]),
                  w_pw2[layer].astype(BF16), row(b_pw2[layer]), w_out[layer].astype(BF16),
                  row(g_post[layer]), row(b_post[layer]))

        zeros_h = jnp.zeros((bsz, N_STATE_ALL), F32)
        zeros_c = jnp.zeros(((CONV_K - 1) * bsz, W_CONV), F32)
        hp, re_p, im_p, cv_p = _layer_call(
            hp, zeros_h, zeros_h, zeros_c, consts, nb=bsz, tl=tl_prompt,
            n_steps=seq // tl_prompt, alpha=alpha, name="layer_prompt")

        h0re = state_ssm_re[layer].reshape(dbsz, N_STATE_ALL)
        h0im = state_ssm_im[layer].reshape(dbsz, N_STATE_ALL)
        cbuf0 = state_conv[layer].transpose(1, 0, 2).reshape((CONV_K - 1) * dbsz, W_CONV)
        hs, re_s, im_s, cv_s = _layer_call(
            hs, h0re, h0im, cbuf0, consts, nb=dbsz, tl=dseq, n_steps=1, alpha=alpha,
            name="layer_sample")

        unrow = lambda a, n: a.reshape(CONV_K - 1, n, W_CONV).transpose(1, 0, 2)
        outs["re_p"].append(re_p.reshape(bsz, N_GROUPS, N_STATE))
        outs["im_p"].append(im_p.reshape(bsz, N_GROUPS, N_STATE))
        outs["cv_p"].append(unrow(cv_p, bsz))
        outs["re_s"].append(re_s.reshape(dbsz, N_GROUPS, N_STATE))
        outs["im_s"].append(im_s.reshape(dbsz, N_GROUPS, N_STATE))
        outs["cv_s"].append(unrow(cv_s, dbsz))

    y_prompt = hp.reshape(seq, bsz, D_MODEL).transpose(1, 0, 2)
    y_sample = hs.reshape(dseq, dbsz, D_MODEL).transpose(1, 0, 2)
    return (y_prompt, y_sample,
            jnp.stack(outs["re_p"]), jnp.stack(outs["im_p"]), jnp.stack(outs["cv_p"]),
            jnp.stack(outs["re_s"]), jnp.stack(outs["im_s"]), jnp.stack(outs["cv_s"]))
```

```python
import functools

import jax
import jax.numpy as jnp
from jax import lax
from jax.experimental import pallas as pl
from jax.experimental.pallas import tpu as pltpu

D_MODEL = 1024
W_SSM = 512
W_CONV = 512
GROUP_P = 16
N_GROUPS = 32
N_STATE = 64
CONV_K = 31
IN_COLS = 2 * W_SSM + 3 * W_CONV
LN_EPS = 1e-5

LANES = 128
SUBLANES = 8
GROUPS_PER_BLOCK = LANES // GROUP_P
N_BLOCKS = N_GROUPS // GROUPS_PER_BLOCK
BLOCK_STATE = GROUPS_PER_BLOCK * N_STATE
N_STATE_ALL = N_GROUPS * N_STATE
SSM_TILES = W_SSM // LANES
CONV_TILES = W_CONV // LANES
VMEM_LIMIT_BYTES = 58 * 1024 * 1024

F32 = jnp.float32
BF16 = jnp.bfloat16


def _layernorm(x, g, b):
    mu = jnp.mean(x, axis=-1, keepdims=True)
    xc = x - mu
    var = jnp.mean(xc * xc, axis=-1, keepdims=True)
    return xc * lax.rsqrt(var + LN_EPS) * g + b


def _slab_pitch(tl):
    if tl % SUBLANES:
        return tl
    pitch = tl
    while (pitch // SUBLANES) % 2 == 0:
        pitch += SUBLANES
    return pitch


def _tb_rows(m, t, pitch):
    return pl.ds(SUBLANES * m * pitch + t, SUBLANES, stride=pitch)


def _store_bt(slab_ref, j, val, nb, tl, pitch):
    if pitch == tl:
        slab_ref[j] = val
    else:
        for b in range(nb):
            slab_ref[j, b * pitch:b * pitch + tl, :] = val[b * tl:(b + 1) * tl]


def _load_bt(slab_ref, j, nb, tl, pitch):
    if pitch == tl:
        return slab_ref[j]
    return jnp.concatenate([slab_ref[j, b * pitch:b * pitch + tl, :] for b in range(nb)], axis=0)


def _gather_tb(slab_ref, j, nb, tl, pitch):
    pieces = [slab_ref[j, _tb_rows(m, t, pitch), :]
              for t in range(tl) for m in range(nb // SUBLANES)]
    return jnp.concatenate(pieces, axis=0)


def _scatter_tb(slab_ref, j, val, nb, tl, pitch):
    for t in range(tl):
        for m in range(nb // SUBLANES):
            r0 = t * nb + SUBLANES * m
            slab_ref[j, _tb_rows(m, t, pitch), :] = val[r0:r0 + SUBLANES]


def _prep_kernel(lre_e, lim_e, ldt_e, bre_e, bim_e, lre_r, lim_r, ldt_r, cre_e, cim_e,
                 bblk_ref, lbre_ref, lbim_ref, cre_ref, cim_ref):
    def zoh(lr, li, ldt):
        dt = jnp.exp(ldt)
        mag = jnp.exp(lr * dt)
        ang = li * dt
        return mag * jnp.cos(ang), mag * jnp.sin(ang)

    lr, li = lre_e[...], lim_e[...]
    ar, ai = zoh(lr, li, ldt_e[...])
    nr, ni = ar - 1.0, ai
    den = lr * lr + li * li
    cr = (nr * lr + ni * li) / den
    ci = (ni * lr - nr * li) / den
    br, bi = bre_e[...], bim_e[...]
    bbr = cr * br - ci * bi
    bbi = cr * bi + ci * br
    shape = bbr.shape
    row_g = lax.broadcasted_iota(jnp.int32, shape, 0) // GROUP_P % GROUPS_PER_BLOCK
    col_g = lax.broadcasted_iota(jnp.int32, shape, 1) // N_STATE
    keep = row_g == col_g
    bblk_ref[:, 0:BLOCK_STATE] = jnp.where(keep, bbr, 0.0).astype(BF16)
    bblk_ref[:, BLOCK_STATE:2 * BLOCK_STATE] = jnp.where(keep, bbi, 0.0).astype(BF16)

    a_r, a_i = zoh(lre_r[...], lim_r[...], ldt_r[...])
    lbre_ref[...] = a_r
    lbim_ref[...] = a_i

    cshape = cre_e.shape
    crow_g = lax.broadcasted_iota(jnp.int32, cshape, 0) // N_STATE % GROUPS_PER_BLOCK
    ccol_g = lax.broadcasted_iota(jnp.int32, cshape, 1) // GROUP_P
    ckeep = crow_g == ccol_g
    cre_ref[...] = jnp.where(ckeep, cre_e[...], 0.0).astype(BF16)
    cim_ref[...] = jnp.where(ckeep, -cim_e[...], 0.0).astype(BF16)


def _layer_kernel(nb, tl, n_steps, alpha, conv_steps,
                  x_ref, h0re_ref, h0im_ref, cbuf0_ref, win_ref, bin_ref, lre_ref, lim_ref,
                  bblk_ref, cre_ref, cim_ref, dskip_ref, wglu_ref, bglu_ref, wdw_ref, bdw_ref,
                  gcl_ref, bcl_ref, wpw_ref, bpw_ref, wout_ref, gpost_ref, bpost_ref,
                  y_ref, hre_ref, him_ref, cbuf_ref,
                  z_ref, bu_ref, slab_ref, full_ref, mix_ref):
    rows = nb * tl
    hist = (CONV_K - 1) * nb
    pitch = _slab_pitch(tl)
    nsub = nb // SUBLANES

    def init_state():
        hre_ref[...] = h0re_ref[...]
        him_ref[...] = h0im_ref[...]
        full_ref[0:hist, :] = cbuf0_ref[...]

    if n_steps > 1:
        pl.when(pl.program_id(0) == 0)(init_state)
    else:
        init_state()

    x2 = x_ref[...].reshape(rows, D_MODEL)
    z_ref[...] = jnp.dot(x2.astype(BF16), win_ref[...], preferred_element_type=F32) + bin_ref[...]

    c0 = 2 * W_SSM
    v = z_ref[:, c0:c0 + W_CONV] * jax.nn.sigmoid(z_ref[:, c0 + W_CONV:c0 + 2 * W_CONV])
    for j in range(SSM_TILES):
        _store_bt(slab_ref, j, z_ref[:, LANES * j:LANES * (j + 1)], nb, tl, pitch)
    for j in range(CONV_TILES):
        _store_bt(slab_ref, SSM_TILES + j, v[:, LANES * j:LANES * (j + 1)], nb, tl, pitch)

    for k in range(N_BLOCKS):
        cs = slice(LANES * k, LANES * (k + 1))
        ss = slice(BLOCK_STATE * k, BLOCK_STATE * (k + 1))
        u_k = _gather_tb(slab_ref, k, nb, tl, pitch)
        bu_ref[...] = jnp.dot(u_k.astype(BF16), bblk_ref[cs, :], preferred_element_type=F32)
        lr = lre_ref[:, ss]
        li = lim_ref[:, ss]
        for m in range(nsub):
            rs = slice(SUBLANES * m, SUBLANES * (m + 1))

            def step(t, carry, m=m):
                hr, hi = carry
                row = pl.multiple_of(t * nb + SUBLANES * m, SUBLANES)
                br = bu_ref[pl.ds(row, SUBLANES), 0:BLOCK_STATE]
                bi = bu_ref[pl.ds(row, SUBLANES), BLOCK_STATE:2 * BLOCK_STATE]
                nr = lr * hr - li * hi + br
                ni = lr * hi + li * hr + bi
                bu_ref[pl.ds(row, SUBLANES), 0:BLOCK_STATE] = nr
                bu_ref[pl.ds(row, SUBLANES), BLOCK_STATE:2 * BLOCK_STATE] = ni
                return nr, ni

            hr, hi = lax.fori_loop(0, tl, step, (hre_ref[rs, ss], him_ref[rs, ss]),
                                   unroll=min(tl, 4))
            hre_ref[rs, ss] = hr
            him_ref[rs, ss] = hi
        y_k = (jnp.dot(bu_ref[:, 0:BLOCK_STATE].astype(BF16), cre_ref[ss, :],
                       preferred_element_type=F32)
               + jnp.dot(bu_ref[:, BLOCK_STATE:2 * BLOCK_STATE].astype(BF16), cim_ref[ss, :],
                         preferred_element_type=F32)
               + u_k * dskip_ref[:, cs])
        _scatter_tb(slab_ref, k, y_k, nb, tl, pitch)

    ys = jnp.concatenate([_load_bt(slab_ref, j, nb, tl, pitch) for j in range(SSM_TILES)], axis=1)
    sg = jax.nn.gelu(ys)
    glu = jnp.dot(sg.astype(BF16), wglu_ref[...], preferred_element_type=F32) + bglu_ref[...]
    s = sg * jax.nn.sigmoid(glu) * jax.nn.silu(z_ref[:, W_SSM:2 * W_SSM])
    mix_ref[:, 0:W_SSM] = s.astype(BF16)

    for c in range(CONV_TILES):
        lanes = slice(LANES * c, LANES * (c + 1))
        full_ref[hist:hist + rows, lanes] = _gather_tb(slab_ref, SSM_TILES + c, nb, tl, pitch)
    for c in range(CONV_TILES):
        lanes = slice(LANES * c, LANES * (c + 1))
        w = [wdw_ref[tap, :, lanes] for tap in range(CONV_K)]
        for m in range(nsub):

            def conv_block(i, carry, m=m, c=c, w=w, lanes=lanes):
                t0 = i * conv_steps
                base = pl.multiple_of(t0 * nb + SUBLANES * m, SUBLANES)
                accs = [None] * conv_steps
                for j in range(conv_steps + CONV_K - 1):
                    xj = full_ref[pl.ds(base + j * nb, SUBLANES), lanes]
                    for r in range(conv_steps):
                        tap = j - r
                        if 0 <= tap < CONV_K:
                            term = w[tap] * xj
                            accs[r] = term if accs[r] is None else accs[r] + term
                for r in range(conv_steps):
                    slab_ref[SSM_TILES + c, _tb_rows(m, t0 + r, pitch), :] = accs[r]
                return carry

            if tl == conv_steps:
                conv_block(0, 0)
            else:
                lax.fori_loop(0, tl // conv_steps, conv_block, 0)
    cbuf_ref[...] = full_ref[rows:rows + hist, :]
    if n_steps > 1:
        full_ref[0:hist, :] = full_ref[rows:rows + hist, :]

    cv = jnp.concatenate([_load_bt(slab_ref, SSM_TILES + j, nb, tl, pitch)
                          for j in range(CONV_TILES)], axis=1) + bdw_ref[...]
    act = jax.nn.silu(_layernorm(cv, gcl_ref[...], bcl_ref[...]))
    cpw = jnp.dot(act.astype(BF16), wpw_ref[...], preferred_element_type=F32) + bpw_ref[...]
    cg = cpw * jax.nn.silu(z_ref[:, c0 + 2 * W_CONV:c0 + 3 * W_CONV])
    mix_ref[:, W_SSM:W_SSM + W_CONV] = cg.astype(BF16)

    mix = jnp.dot(mix_ref[...], wout_ref[...], preferred_element_type=F32)
    x2 = x_ref[...].reshape(rows, D_MODEL)
    out = _layernorm(alpha * x2 + mix, gpost_ref[...], bpost_ref[...])
    y_ref[...] = out.reshape(y_ref.shape)


def _layer_call(x, h0re, h0im, cbuf0, consts, *, nb, tl, n_steps, alpha, conv_steps, name):
    rows = nb * tl
    hist = (CONV_K - 1) * nb
    pitch = _slab_pitch(tl)
    assert nb % SUBLANES == 0 and tl % conv_steps == 0 and (nb * pitch) % SUBLANES == 0
    assert n_steps == 1 or (tl >= CONV_K - 1 and tl % SUBLANES == 0)
    out_shape = (
        jax.ShapeDtypeStruct(x.shape, F32),
        jax.ShapeDtypeStruct((nb, N_STATE_ALL), F32),
        jax.ShapeDtypeStruct((nb, N_STATE_ALL), F32),
        jax.ShapeDtypeStruct((hist, W_CONV), F32),
    )
    scratch = [
        pltpu.VMEM((rows, IN_COLS), F32),
        pltpu.VMEM((rows, 2 * BLOCK_STATE), F32),
        pltpu.VMEM((SSM_TILES + CONV_TILES, nb * pitch, LANES), F32),
        pltpu.VMEM((hist + rows, W_CONV), F32),
        pltpu.VMEM((rows, W_SSM + W_CONV), BF16),
    ]
    operands = (x, h0re, h0im, cbuf0) + tuple(consts)
    kernel = functools.partial(_layer_kernel, nb, tl, n_steps, alpha, conv_steps)
    if n_steps == 1:
        assert x.shape == (rows, D_MODEL)
        return pl.pallas_call(
            kernel, out_shape=out_shape, scratch_shapes=scratch, name=name,
            compiler_params=pltpu.CompilerParams(vmem_limit_bytes=VMEM_LIMIT_BYTES),
        )(*operands)

    assert x.shape == (nb, tl * n_steps, D_MODEL)

    def whole(a):
        zeros = (0,) * a.ndim
        return pl.BlockSpec(a.shape, lambda i: zeros)

    x_spec = pl.BlockSpec((nb, tl, D_MODEL), lambda i: (0, i, 0))
    in_specs = [x_spec] + [whole(a) for a in operands[1:]]
    out_specs = (
        x_spec,
        pl.BlockSpec((nb, N_STATE_ALL), lambda i: (0, 0)),
        pl.BlockSpec((nb, N_STATE_ALL), lambda i: (0, 0)),
        pl.BlockSpec((hist, W_CONV), lambda i: (0, 0)),
    )
    return pl.pallas_call(
        kernel, out_shape=out_shape, grid=(n_steps,), in_specs=in_specs, out_specs=out_specs,
        scratch_shapes=scratch, name=name,
        compiler_params=pltpu.CompilerParams(dimension_semantics=("arbitrary",),
                                             vmem_limit_bytes=VMEM_LIMIT_BYTES),
    )(*operands)


def _prep_call(lam_re, lam_im, log_dt, b_re, b_im, c_re, c_im):
    gp = N_GROUPS * GROUP_P
    tile_b = lambda a: jnp.tile(a, (1, GROUPS_PER_BLOCK))
    rep_rows = lambda a: jnp.repeat(a, GROUP_P, axis=0)
    lre_e = tile_b(rep_rows(lam_re))
    lim_e = tile_b(rep_rows(lam_im))
    ldt_e = jnp.broadcast_to(rep_rows(log_dt[:, None]), (gp, BLOCK_STATE))
    bre_e = tile_b(b_re.transpose(0, 2, 1).reshape(gp, N_STATE))
    bim_e = tile_b(b_im.transpose(0, 2, 1).reshape(gp, N_STATE))
    bc8 = lambda a: jnp.broadcast_to(a.reshape(1, N_STATE_ALL), (SUBLANES, N_STATE_ALL))
    lre_r = bc8(lam_re)
    lim_r = bc8(lam_im)
    ldt_r = bc8(jnp.repeat(log_dt, N_STATE))
    cre_e = tile_b(c_re.transpose(0, 2, 1).reshape(N_STATE_ALL, GROUP_P))
    cim_e = tile_b(c_im.transpose(0, 2, 1).reshape(N_STATE_ALL, GROUP_P))
    out_shape = (
        jax.ShapeDtypeStruct((gp, 2 * BLOCK_STATE), BF16),
        jax.ShapeDtypeStruct((SUBLANES, N_STATE_ALL), F32),
        jax.ShapeDtypeStruct((SUBLANES, N_STATE_ALL), F32),
        jax.ShapeDtypeStruct((N_STATE_ALL, LANES), BF16),
        jax.ShapeDtypeStruct((N_STATE_ALL, LANES), BF16),
    )
    return pl.pallas_call(_prep_kernel, out_shape=out_shape, name="s5_prep")(
        lre_e, lim_e, ldt_e, bre_e, bim_e, lre_r, lim_r, ldt_r, cre_e, cim_e)


def kernel(x_prompt, x_sample, state_ssm_re, state_ssm_im, state_conv, w_in, b_in, lam_re, lam_im, log_dt, b_re, b_im, c_re, c_im, d_skip, w_glu, b_glu, w_dw, b_dw, g_conv_ln, b_conv_ln, w_pw2, b_pw2, w_out, g_post, b_post):
    depth = w_in.shape[0]
    alpha = (2.0 * depth) ** 0.25
    bsz, seq, _ = x_prompt.shape
    dbsz, dseq, _ = x_sample.shape
    tl_prompt = 64

    hp = x_prompt
    hs = x_sample.reshape(dbsz * dseq, D_MODEL)
    outs = {k: [] for k in ("re_p", "im_p", "cv_p", "re_s", "im_s", "cv_s")}
    row = lambda a: a.reshape(1, -1)
    for layer in range(depth):
        bblk, lbre, lbim, cre, cim = _prep_call(
            lam_re[layer], lam_im[layer], log_dt[layer], b_re[layer], b_im[layer],
            c_re[layer], c_im[layer])
        wdw8 = jnp.broadcast_to(w_dw[layer][:, None, :], (CONV_K, SUBLANES, W_CONV))
        consts = (w_in[layer].astype(BF16), row(b_in[layer]), lbre, lbim, bblk, cre, cim,
                  row(d_skip[layer]), w_glu[layer].astype(BF16), row(b_glu[layer]), wdw8,
                  row(b_dw[layer]), row(g_conv_ln[layer]), row(b_conv_ln[layer]),
                  w_pw2[layer].astype(BF16), row(b_pw2[layer]), w_out[layer].astype(BF16),
                  row(g_post[layer]), row(b_post[layer]))

        zeros_h = jnp.zeros((bsz, N_STATE_ALL), F32)
        zeros_c = jnp.zeros(((CONV_K - 1) * bsz, W_CONV), F32)
        hp, re_p, im_p, cv_p = _layer_call(
            hp, zeros_h, zeros_h, zeros_c, consts, nb=bsz, tl=tl_prompt,
            n_steps=seq // tl_prompt, alpha=alpha, conv_steps=8, name="layer_prompt")

        h0re = state_ssm_re[layer].reshape(dbsz, N_STATE_ALL)
        h0im = state_ssm_im[layer].reshape(dbsz, N_STATE_ALL)
        cbuf0 = state_conv[layer].transpose(1, 0, 2).reshape((CONV_K - 1) * dbsz, W_CONV)
        hs, re_s, im_s, cv_s = _layer_call(
            hs, h0re, h0im, cbuf0, consts, nb=dbsz, tl=dseq, n_steps=1, alpha=alpha,
            conv_steps=dseq, name="layer_sample")

        unrow = lambda a, n: a.reshape(CONV_K - 1, n, W_CONV).transpose(1, 0, 2)
        outs["re_p"].append(re_p.reshape(bsz, N_GROUPS, N_STATE))
        outs["im_p"].append(im_p.reshape(bsz, N_GROUPS, N_STATE))
        outs["cv_p"].append(unrow(cv_p, bsz))
        outs["re_s"].append(re_s.reshape(dbsz, N_GROUPS, N_STATE))
        outs["im_s"].append(im_s.reshape(dbsz, N_GROUPS, N_STATE))
        outs["cv_s"].append(unrow(cv_s, dbsz))

    y_sample = hs.reshape(dbsz, dseq, D_MODEL)
    return (hp, y_sample,
            jnp.stack(outs["re_p"]), jnp.stack(outs["im_p"]), jnp.stack(outs["cv_p"]),
            jnp.stack(outs["re_s"]), jnp.stack(outs["im_s"]), jnp.stack(outs["cv_s"]))
```

```python
import functools

import jax
import jax.numpy as jnp
from jax import lax
from jax.experimental import pallas as pl
from jax.experimental.pallas import tpu as pltpu

D_MODEL = 1024
W_SSM = 512
W_CONV = 512
GROUP_P = 16
N_GROUPS = 32
N_STATE = 64
CONV_K = 31
IN_COLS = 2 * W_SSM + 3 * W_CONV
LN_EPS = 1e-5

LANES = 128
SUBLANES = 8
GROUPS_PER_BLOCK = LANES // GROUP_P
N_BLOCKS = N_GROUPS // GROUPS_PER_BLOCK
BLOCK_STATE = GROUPS_PER_BLOCK * N_STATE
N_STATE_ALL = N_GROUPS * N_STATE
SSM_TILES = W_SSM // LANES
CONV_TILES = W_CONV // LANES
VMEM_LIMIT_BYTES = 58 * 1024 * 1024

F32 = jnp.float32
BF16 = jnp.bfloat16


def _layernorm(x, g, b):
    mu = jnp.mean(x, axis=-1, keepdims=True)
    xc = x - mu
    var = jnp.mean(xc * xc, axis=-1, keepdims=True)
    return xc * lax.rsqrt(var + LN_EPS) * g + b


def _slab_pitch(tl):
    if tl % SUBLANES:
        return tl
    pitch = tl
    while (pitch // SUBLANES) % 2 == 0:
        pitch += SUBLANES
    return pitch


def _tb_rows(m, t, pitch):
    return pl.ds(SUBLANES * m * pitch + t, SUBLANES, stride=pitch)


def _store_bt(slab_ref, j, val, nb, tl, pitch):
    if pitch == tl:
        slab_ref[j] = val
    else:
        for b in range(nb):
            slab_ref[j, b * pitch:b * pitch + tl, :] = val[b * tl:(b + 1) * tl]


def _load_bt(slab_ref, j, nb, tl, pitch):
    if pitch == tl:
        return slab_ref[j]
    return jnp.concatenate([slab_ref[j, b * pitch:b * pitch + tl, :] for b in range(nb)], axis=0)


def _gather_tb(slab_ref, j, nb, tl, pitch):
    pieces = [slab_ref[j, _tb_rows(m, t, pitch), :]
              for t in range(tl) for m in range(nb // SUBLANES)]
    return jnp.concatenate(pieces, axis=0)


def _scatter_tb(slab_ref, j, val, nb, tl, pitch):
    for t in range(tl):
        for m in range(nb // SUBLANES):
            r0 = t * nb + SUBLANES * m
            slab_ref[j, _tb_rows(m, t, pitch), :] = val[r0:r0 + SUBLANES]


def _prep_kernel(lre_e, lim_e, ldt_e, bre_e, bim_e, lre_r, lim_r, ldt_r, cre_e, cim_e,
                 bblk_ref, lbre_ref, lbim_ref, cre_ref, cim_ref):
    def zoh(lr, li, ldt):
        dt = jnp.exp(ldt)
        mag = jnp.exp(lr * dt)
        ang = li * dt
        return mag * jnp.cos(ang), mag * jnp.sin(ang)

    lr, li = lre_e[...], lim_e[...]
    ar, ai = zoh(lr, li, ldt_e[...])
    nr, ni = ar - 1.0, ai
    den = lr * lr + li * li
    cr = (nr * lr + ni * li) / den
    ci = (ni * lr - nr * li) / den
    br, bi = bre_e[...], bim_e[...]
    bbr = cr * br - ci * bi
    bbi = cr * bi + ci * br
    shape = bbr.shape
    row_g = lax.broadcasted_iota(jnp.int32, shape, 0) // GROUP_P % GROUPS_PER_BLOCK
    col_g = lax.broadcasted_iota(jnp.int32, shape, 1) // N_STATE
    keep = row_g == col_g
    bblk_ref[:, 0:BLOCK_STATE] = jnp.where(keep, bbr, 0.0).astype(BF16)
    bblk_ref[:, BLOCK_STATE:2 * BLOCK_STATE] = jnp.where(keep, bbi, 0.0).astype(BF16)

    a_r, a_i = zoh(lre_r[...], lim_r[...], ldt_r[...])
    lbre_ref[...] = a_r
    lbim_ref[...] = a_i

    cshape = cre_e.shape
    crow_g = lax.broadcasted_iota(jnp.int32, cshape, 0) // N_STATE % GROUPS_PER_BLOCK
    ccol_g = lax.broadcasted_iota(jnp.int32, cshape, 1) // GROUP_P
    ckeep = crow_g == ccol_g
    cre_ref[...] = jnp.where(ckeep, cre_e[...], 0.0).astype(BF16)
    cim_ref[...] = jnp.where(ckeep, -cim_e[...], 0.0).astype(BF16)


def _layer_kernel(nb, tl, n_steps, alpha, conv_steps,
                  x_ref, h0re_ref, h0im_ref, cbuf0_ref, win_ref, bin_ref, lre_ref, lim_ref,
                  bblk_ref, cre_ref, cim_ref, dskip_ref, wglu_ref, bglu_ref, wdw_ref, bdw_ref,
                  gcl_ref, bcl_ref, wpw_ref, bpw_ref, wout_ref, gpost_ref, bpost_ref,
                  y_ref, hre_ref, him_ref, cbuf_ref,
                  z_ref, bu_ref, slab_ref, full_ref, mix_ref):
    rows = nb * tl
    hist = (CONV_K - 1) * nb
    pitch = _slab_pitch(tl)
    nsub = nb // SUBLANES

    def init_state():
        hre_ref[...] = h0re_ref[...]
        him_ref[...] = h0im_ref[...]
        full_ref[0:hist, :] = cbuf0_ref[...]

    if n_steps > 1:
        pl.when(pl.program_id(0) == 0)(init_state)
    else:
        init_state()

    c0 = 2 * W_SSM
    xb = x_ref[...].reshape(rows, D_MODEL).astype(BF16)
    for lo, hi in ((c0, c0 + 2 * W_CONV), (0, W_SSM), (W_SSM, c0), (c0 + 2 * W_CONV, IN_COLS)):
        z_ref[:, lo:hi] = (jnp.dot(xb, win_ref[:, lo:hi], preferred_element_type=F32)
                           + bin_ref[:, lo:hi])

    v = z_ref[:, c0:c0 + W_CONV] * jax.nn.sigmoid(z_ref[:, c0 + W_CONV:c0 + 2 * W_CONV])
    for j in range(SSM_TILES):
        _store_bt(slab_ref, j, z_ref[:, LANES * j:LANES * (j + 1)], nb, tl, pitch)
    for j in range(CONV_TILES):
        _store_bt(slab_ref, SSM_TILES + j, v[:, LANES * j:LANES * (j + 1)], nb, tl, pitch)

    for k in range(N_BLOCKS):
        cs = slice(LANES * k, LANES * (k + 1))
        ss = slice(BLOCK_STATE * k, BLOCK_STATE * (k + 1))
        u_k = _gather_tb(slab_ref, k, nb, tl, pitch)
        bu_ref[...] = jnp.dot(u_k.astype(BF16), bblk_ref[cs, :], preferred_element_type=F32)
        lr = lre_ref[:, ss]
        li = lim_ref[:, ss]
        for m in range(nsub):
            rs = slice(SUBLANES * m, SUBLANES * (m + 1))

            hr, hi = hre_ref[rs, ss], him_ref[rs, ss]
            for t in range(tl):
                row = slice(t * nb + SUBLANES * m, t * nb + SUBLANES * (m + 1))
                br = bu_ref[row, 0:BLOCK_STATE]
                bi = bu_ref[row, BLOCK_STATE:2 * BLOCK_STATE]
                hr, hi = lr * hr - li * hi + br, lr * hi + li * hr + bi
                bu_ref[row, 0:BLOCK_STATE] = hr
                bu_ref[row, BLOCK_STATE:2 * BLOCK_STATE] = hi
            hre_ref[rs, ss] = hr
            him_ref[rs, ss] = hi
        y_k = (jnp.dot(bu_ref[:, 0:BLOCK_STATE].astype(BF16), cre_ref[ss, :],
                       preferred_element_type=F32)
               + jnp.dot(bu_ref[:, BLOCK_STATE:2 * BLOCK_STATE].astype(BF16), cim_ref[ss, :],
                         preferred_element_type=F32)
               + u_k * dskip_ref[:, cs])
        _scatter_tb(slab_ref, k, y_k, nb, tl, pitch)

    ys = jnp.concatenate([_load_bt(slab_ref, j, nb, tl, pitch) for j in range(SSM_TILES)], axis=1)
    sg = jax.nn.gelu(ys)
    glu = jnp.dot(sg.astype(BF16), wglu_ref[...], preferred_element_type=F32) + bglu_ref[...]
    s = sg * jax.nn.sigmoid(glu) * jax.nn.silu(z_ref[:, W_SSM:2 * W_SSM])
    mix_ref[:, 0:W_SSM] = s.astype(BF16)

    for c in range(CONV_TILES):
        lanes = slice(LANES * c, LANES * (c + 1))
        full_ref[hist:hist + rows, lanes] = _gather_tb(slab_ref, SSM_TILES + c, nb, tl, pitch)
    for c in range(CONV_TILES):
        lanes = slice(LANES * c, LANES * (c + 1))
        w = [wdw_ref[tap, :, lanes] for tap in range(CONV_K)]
        for m in range(nsub):

            for t0 in range(0, tl, conv_steps):
                base = t0 * nb + SUBLANES * m
                accs = [None] * conv_steps
                for j in range(conv_steps + CONV_K - 1):
                    xj = full_ref[base + j * nb:base + j * nb + SUBLANES, lanes]
                    for r in range(conv_steps):
                        tap = j - r
                        if 0 <= tap < CONV_K:
                            term = w[tap] * xj
                            accs[r] = term if accs[r] is None else accs[r] + term
                for r in range(conv_steps):
                    slab_ref[SSM_TILES + c, _tb_rows(m, t0 + r, pitch), :] = accs[r]
    cbuf_ref[...] = full_ref[rows:rows + hist, :]
    if n_steps > 1:
        full_ref[0:hist, :] = full_ref[rows:rows + hist, :]

    cv = jnp.concatenate([_load_bt(slab_ref, SSM_TILES + j, nb, tl, pitch)
                          for j in range(CONV_TILES)], axis=1) + bdw_ref[...]
    act = jax.nn.silu(_layernorm(cv, gcl_ref[...], bcl_ref[...]))
    cpw = jnp.dot(act.astype(BF16), wpw_ref[...], preferred_element_type=F32) + bpw_ref[...]
    cg = cpw * jax.nn.silu(z_ref[:, c0 + 2 * W_CONV:c0 + 3 * W_CONV])
    mix_ref[:, W_SSM:W_SSM + W_CONV] = cg.astype(BF16)

    mix = jnp.dot(mix_ref[...], wout_ref[...], preferred_element_type=F32)
    x2 = x_ref[...].reshape(rows, D_MODEL)
    out = _layernorm(alpha * x2 + mix, gpost_ref[...], bpost_ref[...])
    y_ref[...] = out.reshape(y_ref.shape)


def _layer_call(x, h0re, h0im, cbuf0, consts, *, nb, tl, n_steps, alpha, conv_steps, name):
    rows = nb * tl
    hist = (CONV_K - 1) * nb
    pitch = _slab_pitch(tl)
    assert nb % SUBLANES == 0 and tl % conv_steps == 0 and (nb * pitch) % SUBLANES == 0
    assert n_steps == 1 or (tl >= CONV_K - 1 and tl % SUBLANES == 0)
    out_shape = (
        jax.ShapeDtypeStruct(x.shape, F32),
        jax.ShapeDtypeStruct((nb, N_STATE_ALL), F32),
        jax.ShapeDtypeStruct((nb, N_STATE_ALL), F32),
        jax.ShapeDtypeStruct((hist, W_CONV), F32),
    )
    scratch = [
        pltpu.VMEM((rows, IN_COLS), F32),
        pltpu.VMEM((rows, 2 * BLOCK_STATE), F32),
        pltpu.VMEM((SSM_TILES + CONV_TILES, nb * pitch, LANES), F32),
        pltpu.VMEM((hist + rows, W_CONV), F32),
        pltpu.VMEM((rows, W_SSM + W_CONV), BF16),
    ]
    operands = (x, h0re, h0im, cbuf0) + tuple(consts)
    kernel = functools.partial(_layer_kernel, nb, tl, n_steps, alpha, conv_steps)
    if n_steps == 1:
        assert x.shape == (rows, D_MODEL)
        return pl.pallas_call(
            kernel, out_shape=out_shape, scratch_shapes=scratch, name=name,
            compiler_params=pltpu.CompilerParams(vmem_limit_bytes=VMEM_LIMIT_BYTES),
        )(*operands)

    assert x.shape == (nb, tl * n_steps, D_MODEL)

    def whole(a):
        zeros = (0,) * a.ndim
        return pl.BlockSpec(a.shape, lambda i: zeros)

    x_spec = pl.BlockSpec((nb, tl, D_MODEL), lambda i: (0, i, 0))
    in_specs = [x_spec] + [whole(a) for a in operands[1:]]
    out_specs = (
        x_spec,
        pl.BlockSpec((nb, N_STATE_ALL), lambda i: (0, 0)),
        pl.BlockSpec((nb, N_STATE_ALL), lambda i: (0, 0)),
        pl.BlockSpec((hist, W_CONV), lambda i: (0, 0)),
    )
    return pl.pallas_call(
        kernel, out_shape=out_shape, grid=(n_steps,), in_specs=in_specs, out_specs=out_specs,
        scratch_shapes=scratch, name=name,
        compiler_params=pltpu.CompilerParams(dimension_semantics=("arbitrary",),
                                             vmem_limit_bytes=VMEM_LIMIT_BYTES),
    )(*operands)


def _prep_call(lam_re, lam_im, log_dt, b_re, b_im, c_re, c_im):
    gp = N_GROUPS * GROUP_P
    tile_b = lambda a: jnp.tile(a, (1, GROUPS_PER_BLOCK))
    rep_rows = lambda a: jnp.repeat(a, GROUP_P, axis=0)
    lre_e = tile_b(rep_rows(lam_re))
    lim_e = tile_b(rep_rows(lam_im))
    ldt_e = jnp.broadcast_to(rep_rows(log_dt[:, None]), (gp, BLOCK_STATE))
    bre_e = tile_b(b_re.transpose(0, 2, 1).reshape(gp, N_STATE))
    bim_e = tile_b(b_im.transpose(0, 2, 1).reshape(gp, N_STATE))
    bc8 = lambda a: jnp.broadcast_to(a.reshape(1, N_STATE_ALL), (SUBLANES, N_STATE_ALL))
    lre_r = bc8(lam_re)
    lim_r = bc8(lam_im)
    ldt_r = bc8(jnp.repeat(log_dt, N_STATE))
    cre_e = tile_b(c_re.transpose(0, 2, 1).reshape(N_STATE_ALL, GROUP_P))
    cim_e = tile_b(c_im.transpose(0, 2, 1).reshape(N_STATE_ALL, GROUP_P))
    out_shape = (
        jax.ShapeDtypeStruct((gp, 2 * BLOCK_STATE), BF16),
        jax.ShapeDtypeStruct((SUBLANES, N_STATE_ALL), F32),
        jax.ShapeDtypeStruct((SUBLANES, N_STATE_ALL), F32),
        jax.ShapeDtypeStruct((N_STATE_ALL, LANES), BF16),
        jax.ShapeDtypeStruct((N_STATE_ALL, LANES), BF16),
    )
    return pl.pallas_call(_prep_kernel, out_shape=out_shape, name="s5_prep")(
        lre_e, lim_e, ldt_e, bre_e, bim_e, lre_r, lim_r, ldt_r, cre_e, cim_e)


def kernel(x_prompt, x_sample, state_ssm_re, state_ssm_im, state_conv, w_in, b_in, lam_re, lam_im, log_dt, b_re, b_im, c_re, c_im, d_skip, w_glu, b_glu, w_dw, b_dw, g_conv_ln, b_conv_ln, w_pw2, b_pw2, w_out, g_post, b_post):
    depth = w_in.shape[0]
    alpha = (2.0 * depth) ** 0.25
    bsz, seq, _ = x_prompt.shape
    dbsz, dseq, _ = x_sample.shape
    tl_prompt = 64

    hp = x_prompt
    hs = x_sample.reshape(dbsz * dseq, D_MODEL)
    outs = {k: [] for k in ("re_p", "im_p", "cv_p", "re_s", "im_s", "cv_s")}
    row = lambda a: a.reshape(1, -1)
    for layer in range(depth):
        bblk, lbre, lbim, cre, cim = _prep_call(
            lam_re[layer], lam_im[layer], log_dt[layer], b_re[layer], b_im[layer],
            c_re[layer], c_im[layer])
        wdw8 = jnp.broadcast_to(w_dw[layer][:, None, :], (CONV_K, SUBLANES, W_CONV))
        consts = (w_in[layer].astype(BF16), row(b_in[layer]), lbre, lbim, bblk, cre, cim,
                  row(d_skip[layer]), w_glu[layer].astype(BF16), row(b_glu[layer]), wdw8,
                  row(b_dw[layer]), row(g_conv_ln[layer]), row(b_conv_ln[layer]),
                  w_pw2[layer].astype(BF16), row(b_pw2[layer]), w_out[layer].astype(BF16),
                  row(g_post[layer]), row(b_post[layer]))

        zeros_h = jnp.zeros((bsz, N_STATE_ALL), F32)
        zeros_c = jnp.zeros(((CONV_K - 1) * bsz, W_CONV), F32)
        hp, re_p, im_p, cv_p = _layer_call(
            hp, zeros_h, zeros_h, zeros_c, consts, nb=bsz, tl=tl_prompt,
            n_steps=seq // tl_prompt, alpha=alpha, conv_steps=8, name="layer_prompt")

        h0re = state_ssm_re[layer].reshape(dbsz, N_STATE_ALL)
        h0im = state_ssm_im[layer].reshape(dbsz, N_STATE_ALL)
        cbuf0 = state_conv[layer].transpose(1, 0, 2).reshape((CONV_K - 1) * dbsz, W_CONV)
        hs, re_s, im_s, cv_s = _layer_call(
            hs, h0re, h0im, cbuf0, consts, nb=dbsz, tl=dseq, n_steps=1, alpha=alpha,
            conv_steps=dseq, name="layer_sample")

        unrow = lambda a, n: a.reshape(CONV_K - 1, n, W_CONV).transpose(1, 0, 2)
        outs["re_p"].append(re_p.reshape(bsz, N_GROUPS, N_STATE))
        outs["im_p"].append(im_p.reshape(bsz, N_GROUPS, N_STATE))
        outs["cv_p"].append(unrow(cv_p, bsz))
        outs["re_s"].append(re_s.reshape(dbsz, N_GROUPS, N_STATE))
        outs["im_s"].append(im_s.reshape(dbsz, N_GROUPS, N_STATE))
        outs["cv_s"].append(unrow(cv_s, dbsz))

    y_sample = hs.reshape(dbsz, dseq, D_MODEL)
    return (hp, y_sample,
            jnp.stack(outs["re_p"]), jnp.stack(outs["im_p"]), jnp.stack(outs["cv_p"]),
            jnp.stack(outs["re_s"]), jnp.stack(outs["im_s"]), jnp.stack(outs["cv_s"]))
```

```python
import functools

import jax
import jax.numpy as jnp
from jax import lax
from jax.experimental import pallas as pl
from jax.experimental.pallas import tpu as pltpu

D_MODEL = 1024
W_SSM = 512
W_CONV = 512
GROUP_P = 16
N_GROUPS = 32
N_STATE = 64
CONV_K = 31
IN_COLS = 2 * W_SSM + 3 * W_CONV
LN_EPS = 1e-5

LANES = 128
SUBLANES = 8
GROUPS_PER_BLOCK = LANES // GROUP_P
N_BLOCKS = N_GROUPS // GROUPS_PER_BLOCK
BLOCK_STATE = GROUPS_PER_BLOCK * N_STATE
N_STATE_ALL = N_GROUPS * N_STATE
SSM_TILES = W_SSM // LANES
CONV_TILES = W_CONV // LANES
VMEM_LIMIT_BYTES = 58 * 1024 * 1024

F32 = jnp.float32
BF16 = jnp.bfloat16


def _layernorm(x, g, b):
    mu = jnp.mean(x, axis=-1, keepdims=True)
    xc = x - mu
    var = jnp.mean(xc * xc, axis=-1, keepdims=True)
    return xc * lax.rsqrt(var + LN_EPS) * g + b


def _slab_pitch(tl):
    if tl % SUBLANES:
        return tl
    pitch = tl
    while (pitch // SUBLANES) % 2 == 0:
        pitch += SUBLANES
    return pitch


def _tb_rows(m, t, pitch):
    return pl.ds(SUBLANES * m * pitch + t, SUBLANES, stride=pitch)


def _store_bt(slab_ref, j, val, nb, tl, pitch):
    if pitch == tl:
        slab_ref[j] = val
    else:
        for b in range(nb):
            slab_ref[j, b * pitch:b * pitch + tl, :] = val[b * tl:(b + 1) * tl]


def _load_bt(slab_ref, j, nb, tl, pitch):
    if pitch == tl:
        return slab_ref[j]
    return jnp.concatenate([slab_ref[j, b * pitch:b * pitch + tl, :] for b in range(nb)], axis=0)


def _gather_tb(slab_ref, j, nb, tl, pitch):
    pieces = [slab_ref[j, _tb_rows(m, t, pitch), :]
              for t in range(tl) for m in range(nb // SUBLANES)]
    return jnp.concatenate(pieces, axis=0)


def _scatter_tb(slab_ref, j, val, nb, tl, pitch):
    for t in range(tl):
        for m in range(nb // SUBLANES):
            r0 = t * nb + SUBLANES * m
            slab_ref[j, _tb_rows(m, t, pitch), :] = val[r0:r0 + SUBLANES]


def _prep_kernel(lam_ref, lamrow_ref, bt_ref, ct_ref,
                 bblk_ref, lbre_ref, lbim_ref, cre_ref, cim_ref):
    gp = N_GROUPS * GROUP_P

    def zoh(lr, li, ldt):
        dt = jnp.exp(ldt)
        mag = jnp.exp(lr * dt)
        ang = li * dt
        return mag * jnp.cos(ang), mag * jnp.sin(ang)

    def per_channel(a):
        wide = jnp.broadcast_to(a[:, None, :], (N_GROUPS, GROUP_P, BLOCK_STATE))
        return wide.reshape(gp, BLOCK_STATE)

    lr, li = lam_ref[0], lam_ref[1]
    ar, ai = zoh(lr, li, lam_ref[2])
    nr, ni = ar - 1.0, ai
    den = lr * lr + li * li
    cr = per_channel((nr * lr + ni * li) / den)
    ci = per_channel((ni * lr - nr * li) / den)
    br, bi = bt_ref[0], bt_ref[1]
    bbr = cr * br - ci * bi
    bbi = cr * bi + ci * br
    shape = bbr.shape
    row_g = lax.broadcasted_iota(jnp.int32, shape, 0) // GROUP_P % GROUPS_PER_BLOCK
    col_g = lax.broadcasted_iota(jnp.int32, shape, 1) // N_STATE
    keep = row_g == col_g
    bblk_ref[:, 0:BLOCK_STATE] = jnp.where(keep, bbr, 0.0).astype(BF16)
    bblk_ref[:, BLOCK_STATE:2 * BLOCK_STATE] = jnp.where(keep, bbi, 0.0).astype(BF16)

    a_r, a_i = zoh(lamrow_ref[0], lamrow_ref[1], lamrow_ref[2])
    lbre_ref[...] = jnp.broadcast_to(a_r, lbre_ref.shape)
    lbim_ref[...] = jnp.broadcast_to(a_i, lbim_ref.shape)

    cshape = cre_ref.shape
    crow_g = lax.broadcasted_iota(jnp.int32, cshape, 0) // N_STATE % GROUPS_PER_BLOCK
    ccol_g = lax.broadcasted_iota(jnp.int32, cshape, 1) // GROUP_P
    ckeep = crow_g == ccol_g
    cre_ref[...] = jnp.where(ckeep, ct_ref[0], 0.0).astype(BF16)
    cim_ref[...] = jnp.where(ckeep, -ct_ref[1], 0.0).astype(BF16)


def _layer_kernel(nb, tl, n_steps, alpha, conv_steps, zero_state, x_ref, *refs):
    if zero_state:
        refs = (None, None, None) + refs
    _layer_body(nb, tl, n_steps, alpha, conv_steps, x_ref, *refs)


def _layer_body(nb, tl, n_steps, alpha, conv_steps,
                x_ref, h0re_ref, h0im_ref, cbuf0_ref, win_ref, bin_ref, lre_ref, lim_ref,
                bblk_ref, cre_ref, cim_ref, dskip_ref, wglu_ref, bglu_ref, wdw_ref, bdw_ref,
                gcl_ref, bcl_ref, wpw_ref, bpw_ref, wout_ref, gpost_ref, bpost_ref,
                y_ref, hre_ref, him_ref, cbuf_ref,
                z_ref, bu_ref, slab_ref, full_ref, mix_ref):
    rows = nb * tl
    hist = (CONV_K - 1) * nb
    pitch = _slab_pitch(tl)
    nsub = nb // SUBLANES
    c0 = 2 * W_SSM

    def init_state():
        if h0re_ref is None:
            hre_ref[...] = jnp.zeros_like(hre_ref)
            him_ref[...] = jnp.zeros_like(him_ref)
            full_ref[0:hist, :] = jnp.zeros((hist, W_CONV), F32)
        else:
            hre_ref[...] = h0re_ref[...]
            him_ref[...] = h0im_ref[...]
            full_ref[0:hist, :] = cbuf0_ref[...]

    if n_steps > 1:
        pl.when(pl.program_id(0) == 0)(init_state)
    else:
        init_state()

    xb = x_ref[...].reshape(rows, D_MODEL).astype(BF16)
    for lo, hi in ((c0, c0 + 2 * W_CONV), (0, W_SSM), (W_SSM, c0), (c0 + 2 * W_CONV, IN_COLS)):
        z_ref[:, lo:hi] = (jnp.dot(xb, win_ref[:, lo:hi], preferred_element_type=F32)
                           + bin_ref[:, lo:hi])
    v = z_ref[:, c0:c0 + W_CONV] * jax.nn.sigmoid(z_ref[:, c0 + W_CONV:c0 + 2 * W_CONV])
    for j in range(CONV_TILES):
        _store_bt(slab_ref, SSM_TILES + j, v[:, LANES * j:LANES * (j + 1)], nb, tl, pitch)

    for c in range(CONV_TILES):
        lanes = slice(LANES * c, LANES * (c + 1))
        full_ref[hist:hist + rows, lanes] = _gather_tb(slab_ref, SSM_TILES + c, nb, tl, pitch)
    for c in range(CONV_TILES):
        lanes = slice(LANES * c, LANES * (c + 1))
        w = [wdw_ref[tap, :, lanes] for tap in range(CONV_K)]
        for m in range(nsub):
            for t0 in range(0, tl, conv_steps):
                base = t0 * nb + SUBLANES * m
                accs = [None] * conv_steps
                for j in range(conv_steps + CONV_K - 1):
                    xj = full_ref[base + j * nb:base + j * nb + SUBLANES, lanes]
                    for r in range(conv_steps):
                        tap = j - r
                        if 0 <= tap < CONV_K:
                            term = w[tap] * xj
                            accs[r] = term if accs[r] is None else accs[r] + term
                for r in range(conv_steps):
                    slab_ref[SSM_TILES + c, _tb_rows(m, t0 + r, pitch), :] = accs[r]
    cbuf_ref[...] = full_ref[rows:rows + hist, :]
    if n_steps > 1:
        full_ref[0:hist, :] = full_ref[rows:rows + hist, :]
    for j in range(SSM_TILES):
        _store_bt(slab_ref, j, z_ref[:, LANES * j:LANES * (j + 1)], nb, tl, pitch)

    for k in range(N_BLOCKS):
        cs = slice(LANES * k, LANES * (k + 1))
        ss = slice(BLOCK_STATE * k, BLOCK_STATE * (k + 1))
        u_k = _gather_tb(slab_ref, k, nb, tl, pitch)
        bu_ref[...] = jnp.dot(u_k.astype(BF16), bblk_ref[cs, :], preferred_element_type=F32)
        lr = lre_ref[:, ss]
        li = lim_ref[:, ss]
        for m in range(nsub):
            rs = slice(SUBLANES * m, SUBLANES * (m + 1))
            hr, hi = hre_ref[rs, ss], him_ref[rs, ss]
            for t in range(tl):
                row = slice(t * nb + SUBLANES * m, t * nb + SUBLANES * (m + 1))
                br = bu_ref[row, 0:BLOCK_STATE]
                bi = bu_ref[row, BLOCK_STATE:2 * BLOCK_STATE]
                hr, hi = lr * hr - li * hi + br, lr * hi + li * hr + bi
                bu_ref[row, 0:BLOCK_STATE] = hr
                bu_ref[row, BLOCK_STATE:2 * BLOCK_STATE] = hi
            hre_ref[rs, ss] = hr
            him_ref[rs, ss] = hi
        y_k = (jnp.dot(bu_ref[:, 0:BLOCK_STATE].astype(BF16), cre_ref[ss, :],
                       preferred_element_type=F32)
               + jnp.dot(bu_ref[:, BLOCK_STATE:2 * BLOCK_STATE].astype(BF16), cim_ref[ss, :],
                         preferred_element_type=F32)
               + u_k * dskip_ref[:, cs])
        _scatter_tb(slab_ref, k, y_k, nb, tl, pitch)

    ys = jnp.concatenate([_load_bt(slab_ref, j, nb, tl, pitch) for j in range(SSM_TILES)], axis=1)
    sg = jax.nn.gelu(ys)
    glu = jnp.dot(sg.astype(BF16), wglu_ref[...], preferred_element_type=F32) + bglu_ref[...]
    s = sg * jax.nn.sigmoid(glu) * jax.nn.silu(z_ref[:, W_SSM:c0])
    mix_ref[:, 0:W_SSM] = s.astype(BF16)

    cv = jnp.concatenate([_load_bt(slab_ref, SSM_TILES + j, nb, tl, pitch)
                          for j in range(CONV_TILES)], axis=1) + bdw_ref[...]
    act = jax.nn.silu(_layernorm(cv, gcl_ref[...], bcl_ref[...]))
    cpw = jnp.dot(act.astype(BF16), wpw_ref[...], preferred_element_type=F32) + bpw_ref[...]
    cg = cpw * jax.nn.silu(z_ref[:, c0 + 2 * W_CONV:IN_COLS])
    mix_ref[:, W_SSM:W_SSM + W_CONV] = cg.astype(BF16)

    mix = jnp.dot(mix_ref[...], wout_ref[...], preferred_element_type=F32)
    pre = alpha * x_ref[...].reshape(rows, D_MODEL) + mix
    y_ref[...] = _layernorm(pre, gpost_ref[...], bpost_ref[...]).reshape(y_ref.shape)


def _layer_call(x, state, consts, *, nb, tl, n_steps, alpha, conv_steps, name):
    rows = nb * tl
    hist = (CONV_K - 1) * nb
    pitch = _slab_pitch(tl)
    assert nb % SUBLANES == 0 and tl % conv_steps == 0 and (nb * pitch) % SUBLANES == 0
    assert n_steps == 1 or (tl >= CONV_K - 1 and tl % SUBLANES == 0)
    out_shape = (
        jax.ShapeDtypeStruct(x.shape, F32),
        jax.ShapeDtypeStruct((nb, N_STATE_ALL), F32),
        jax.ShapeDtypeStruct((nb, N_STATE_ALL), F32),
        jax.ShapeDtypeStruct((hist, W_CONV), F32),
    )
    scratch = [
        pltpu.VMEM((rows, IN_COLS), F32),
        pltpu.VMEM((rows, 2 * BLOCK_STATE), F32),
        pltpu.VMEM((SSM_TILES + CONV_TILES, nb * pitch, LANES), F32),
        pltpu.VMEM((hist + rows, W_CONV), F32),
        pltpu.VMEM((rows, W_SSM + W_CONV), BF16),
    ]
    operands = (x,) + (() if state is None else tuple(state)) + tuple(consts)
    kernel = functools.partial(_layer_kernel, nb, tl, n_steps, alpha, conv_steps, state is None)
    if n_steps == 1:
        assert x.shape == (rows, D_MODEL)
        return pl.pallas_call(
            kernel, out_shape=out_shape, scratch_shapes=scratch, name=name,
            compiler_params=pltpu.CompilerParams(vmem_limit_bytes=VMEM_LIMIT_BYTES),
        )(*operands)

    assert x.shape == (nb, tl * n_steps, D_MODEL)

    def whole(a):
        zeros = (0,) * a.ndim
        return pl.BlockSpec(a.shape, lambda i: zeros)

    x_spec = pl.BlockSpec((nb, tl, D_MODEL), lambda i: (0, i, 0))
    in_specs = [x_spec] + [whole(a) for a in operands[1:]]
    out_specs = (
        x_spec,
        pl.BlockSpec((nb, N_STATE_ALL), lambda i: (0, 0)),
        pl.BlockSpec((nb, N_STATE_ALL), lambda i: (0, 0)),
        pl.BlockSpec((hist, W_CONV), lambda i: (0, 0)),
    )
    return pl.pallas_call(
        kernel, out_shape=out_shape, grid=(n_steps,), in_specs=in_specs, out_specs=out_specs,
        scratch_shapes=scratch, name=name,
        compiler_params=pltpu.CompilerParams(dimension_semantics=("arbitrary",),
                                             vmem_limit_bytes=VMEM_LIMIT_BYTES),
    )(*operands)


def _prep_call(lam_re, lam_im, log_dt, b_re, b_im, c_re, c_im):
    gp = N_GROUPS * GROUP_P
    tile_b = lambda a: jnp.tile(a, (1, GROUPS_PER_BLOCK))
    lam = jnp.stack([tile_b(lam_re), tile_b(lam_im),
                     jnp.broadcast_to(log_dt[:, None], (N_GROUPS, BLOCK_STATE))])
    lamrow = jnp.stack([lam_re.reshape(1, N_STATE_ALL), lam_im.reshape(1, N_STATE_ALL),
                        jnp.repeat(log_dt, N_STATE).reshape(1, N_STATE_ALL)])
    bt = jnp.stack([tile_b(b.transpose(0, 2, 1).reshape(gp, N_STATE)) for b in (b_re, b_im)])
    ct = jnp.stack([tile_b(c.transpose(0, 2, 1).reshape(N_STATE_ALL, GROUP_P))
                    for c in (c_re, c_im)])
    out_shape = (
        jax.ShapeDtypeStruct((gp, 2 * BLOCK_STATE), BF16),
        jax.ShapeDtypeStruct((SUBLANES, N_STATE_ALL), F32),
        jax.ShapeDtypeStruct((SUBLANES, N_STATE_ALL), F32),
        jax.ShapeDtypeStruct((N_STATE_ALL, LANES), BF16),
        jax.ShapeDtypeStruct((N_STATE_ALL, LANES), BF16),
    )
    return pl.pallas_call(_prep_kernel, out_shape=out_shape, name="s5_prep")(lam, lamrow, bt, ct)


def kernel(x_prompt, x_sample, state_ssm_re, state_ssm_im, state_conv, w_in, b_in, lam_re, lam_im, log_dt, b_re, b_im, c_re, c_im, d_skip, w_glu, b_glu, w_dw, b_dw, g_conv_ln, b_conv_ln, w_pw2, b_pw2, w_out, g_post, b_post):
    depth = w_in.shape[0]
    alpha = (2.0 * depth) ** 0.25
    bsz, seq, _ = x_prompt.shape
    dbsz, dseq, _ = x_sample.shape
    tl_prompt = 64

    hp = x_prompt
    hs = x_sample.reshape(dbsz * dseq, D_MODEL)
    outs = {k: [] for k in ("re_p", "im_p", "cv_p", "re_s", "im_s", "cv_s")}
    row = lambda a: a.reshape(1, -1)
    for layer in range(depth):
        bblk, lbre, lbim, cre, cim = _prep_call(
            lam_re[layer], lam_im[layer], log_dt[layer], b_re[layer], b_im[layer],
            c_re[layer], c_im[layer])
        wdw8 = jnp.broadcast_to(w_dw[layer][:, None, :], (CONV_K, SUBLANES, W_CONV))
        consts = (w_in[layer].astype(BF16), row(b_in[layer]), lbre, lbim, bblk, cre, cim,
                  row(d_skip[layer]), w_glu[layer].astype(BF16), row(b_glu[layer]), wdw8,
                  row(b_dw[layer]), row(g_conv_ln[layer]), row(b_conv_ln[layer]),
                  w_pw2[layer].astype(BF16), row(b_pw2[layer]), w_out[layer].astype(BF16),
                  row(g_post[layer]), row(b_post[layer]))

        hp, re_p, im_p, cv_p = _layer_call(
            hp, None, consts, nb=bsz, tl=tl_prompt,
            n_steps=seq // tl_prompt, alpha=alpha, conv_steps=8, name="layer_prompt")

        h0re = state_ssm_re[layer].reshape(dbsz, N_STATE_ALL)
        h0im = state_ssm_im[layer].reshape(dbsz, N_STATE_ALL)
        cbuf0 = state_conv[layer].transpose(1, 0, 2).reshape((CONV_K - 1) * dbsz, W_CONV)
        hs, re_s, im_s, cv_s = _layer_call(
            hs, (h0re, h0im, cbuf0), consts, nb=dbsz, tl=dseq, n_steps=1, alpha=alpha,
            conv_steps=dseq, name="layer_sample")

        unrow = lambda a, n: a.reshape(CONV_K - 1, n, W_CONV).transpose(1, 0, 2)
        outs["re_p"].append(re_p.reshape(bsz, N_GROUPS, N_STATE))
        outs["im_p"].append(im_p.reshape(bsz, N_GROUPS, N_STATE))
        outs["cv_p"].append(unrow(cv_p, bsz))
        outs["re_s"].append(re_s.reshape(dbsz, N_GROUPS, N_STATE))
        outs["im_s"].append(im_s.reshape(dbsz, N_GROUPS, N_STATE))
        outs["cv_s"].append(unrow(cv_s, dbsz))

    y_sample = hs.reshape(dbsz, dseq, D_MODEL)
    return (hp, y_sample,
            jnp.stack(outs["re_p"]), jnp.stack(outs["im_p"]), jnp.stack(outs["cv_p"]),
            jnp.stack(outs["re_s"]), jnp.stack(outs["im_s"]), jnp.stack(outs["cv_s"]))
```

```python
import functools

import jax
import jax.numpy as jnp
from jax import lax
from jax.experimental import pallas as pl
from jax.experimental.pallas import tpu as pltpu

D_MODEL = 1024
W_SSM = 512
W_CONV = 512
GROUP_P = 16
N_GROUPS = 32
N_STATE = 64
CONV_K = 31
IN_COLS = 2 * W_SSM + 3 * W_CONV
LN_EPS = 1e-5

LANES = 128
SUBLANES = 8
GROUPS_PER_BLOCK = LANES // GROUP_P
N_BLOCKS = N_GROUPS // GROUPS_PER_BLOCK
BLOCK_STATE = GROUPS_PER_BLOCK * N_STATE
N_STATE_ALL = N_GROUPS * N_STATE
SSM_TILES = W_SSM // LANES
CONV_TILES = W_CONV // LANES
VMEM_LIMIT_BYTES = 58 * 1024 * 1024

F32 = jnp.float32
BF16 = jnp.bfloat16


def _layernorm(x, g, b):
    mu = jnp.mean(x, axis=-1, keepdims=True)
    xc = x - mu
    var = jnp.mean(xc * xc, axis=-1, keepdims=True)
    return xc * lax.rsqrt(var + LN_EPS) * g + b


def _slab_pitch(tl):
    if tl % SUBLANES:
        return tl
    pitch = tl
    while (pitch // SUBLANES) % 2 == 0:
        pitch += SUBLANES
    return pitch


def _tb_rows(m, t, pitch):
    return pl.ds(SUBLANES * m * pitch + t, SUBLANES, stride=pitch)


def _store_bt(slab_ref, j, val, nb, tl, pitch):
    if pitch == tl:
        slab_ref[j] = val
    else:
        for b in range(nb):
            slab_ref[j, b * pitch:b * pitch + tl, :] = val[b * tl:(b + 1) * tl]


def _load_bt(slab_ref, j, nb, tl, pitch):
    if pitch == tl:
        return slab_ref[j]
    return jnp.concatenate([slab_ref[j, b * pitch:b * pitch + tl, :] for b in range(nb)], axis=0)


def _gather_tb(slab_ref, j, nb, tl, pitch):
    pieces = [slab_ref[j, _tb_rows(m, t, pitch), :]
              for t in range(tl) for m in range(nb // SUBLANES)]
    return jnp.concatenate(pieces, axis=0)


def _prep_kernel(lam_ref, lamrow_ref, bt_ref, cg_ref,
                 sin_ref, sout_ref, sdir_ref, l2re_ref, l2im_ref):
    gp = N_GROUPS * GROUP_P
    hi = lax.Precision.HIGHEST
    abt = (((1,), (1,)), ((), ()))

    def zoh(lr, li, ldt):
        dt = jnp.exp(ldt)
        mag = jnp.exp(lr * dt)
        ang = li * dt
        return mag * jnp.cos(ang), mag * jnp.sin(ang)

    def per_channel(a):
        wide = jnp.broadcast_to(a[:, None, :], (N_GROUPS, GROUP_P, BLOCK_STATE))
        return wide.reshape(gp, BLOCK_STATE)

    lr, li = lam_ref[0], lam_ref[1]
    ar, ai = zoh(lr, li, lam_ref[2])
    nr, ni = ar - 1.0, ai
    den = lr * lr + li * li
    cr = per_channel((nr * lr + ni * li) / den)
    ci = per_channel((ni * lr - nr * li) / den)
    ar, ai = per_channel(ar), per_channel(ai)
    shape = ar.shape
    row_g = lax.broadcasted_iota(jnp.int32, shape, 0) // GROUP_P % GROUPS_PER_BLOCK
    col_g = lax.broadcasted_iota(jnp.int32, shape, 1) // N_STATE
    keep = row_g == col_g
    br, bi = bt_ref[0], bt_ref[1]
    bbr = jnp.where(keep, cr * br - ci * bi, 0.0)
    bbi = jnp.where(keep, cr * bi + ci * br, 0.0)
    lbr = ar * bbr - ai * bbi
    lbi = ar * bbi + ai * bbr
    c0r = jnp.where(keep, cg_ref[0], 0.0)
    c0i = jnp.where(keep, cg_ref[1], 0.0)
    c1r = ar * c0r - ai * c0i
    c1i = ar * c0i + ai * c0r
    c2r = ar * c1r - ai * c1i
    c2i = ar * c1i + ai * c1r

    for k in range(N_BLOCKS):
        rs = slice(LANES * k, LANES * (k + 1))
        b_k = jnp.concatenate([bbr[rs], bbi[rs]], axis=1)
        lb_k = jnp.concatenate([lbr[rs], lbi[rs]], axis=1)
        sin_ref[k, 0:LANES, :] = lb_k.astype(BF16)
        sin_ref[k, LANES:2 * LANES, :] = b_k.astype(BF16)
        c_k = jnp.concatenate([c0r[rs], -c0i[rs]], axis=1)
        c12_k = jnp.concatenate([jnp.concatenate([c1r[rs], -c1i[rs]], axis=1),
                                 jnp.concatenate([c2r[rs], -c2i[rs]], axis=1)], axis=0)
        sout_ref[k] = c12_k.T.astype(BF16)
        cb = lax.dot_general(b_k, c_k, abt, precision=hi, preferred_element_type=F32)
        clb = lax.dot_general(lb_k, c_k, abt, precision=hi, preferred_element_type=F32)
        sdir_ref[k, 0:LANES, 0:LANES] = cb.astype(BF16)
        sdir_ref[k, 0:LANES, LANES:2 * LANES] = clb.astype(BF16)
        sdir_ref[k, LANES:2 * LANES, 0:LANES] = jnp.zeros((LANES, LANES), BF16)
        sdir_ref[k, LANES:2 * LANES, LANES:2 * LANES] = cb.astype(BF16)

    a_r, a_i = zoh(lamrow_ref[0], lamrow_ref[1], lamrow_ref[2])
    l2re_ref[...] = jnp.broadcast_to(a_r * a_r - a_i * a_i, l2re_ref.shape)
    l2im_ref[...] = jnp.broadcast_to(2.0 * a_r * a_i, l2im_ref.shape)


def _layer_kernel(nb, tl, n_steps, alpha, conv_steps, zero_state, x_ref, *refs):
    if zero_state:
        refs = (None, None, None) + refs
    _layer_body(nb, tl, n_steps, alpha, conv_steps, x_ref, *refs)


def _layer_body(nb, tl, n_steps, alpha, conv_steps,
                x_ref, h0re_ref, h0im_ref, cbuf0_ref, win_ref, bin_ref, l2re_ref, l2im_ref,
                sin_ref, sout_ref, sdir_ref, dskip_ref, wglu_ref, bglu_ref, wdw_ref, bdw_ref,
                gcl_ref, bcl_ref, wpw_ref, bpw_ref, wout_ref, gpost_ref, bpost_ref,
                y_ref, hre_ref, him_ref, cbuf_ref,
                z_ref, bu_ref, slab_ref, full_ref, mix_ref):
    rows = nb * tl
    hist = (CONV_K - 1) * nb
    pitch = _slab_pitch(tl)
    nsub = nb // SUBLANES
    c0 = 2 * W_SSM

    def init_state():
        if h0re_ref is None:
            hre_ref[...] = jnp.zeros_like(hre_ref)
            him_ref[...] = jnp.zeros_like(him_ref)
            full_ref[0:hist, :] = jnp.zeros((hist, W_CONV), F32)
        else:
            hre_ref[...] = h0re_ref[...]
            him_ref[...] = h0im_ref[...]
            full_ref[0:hist, :] = cbuf0_ref[...]

    if n_steps > 1:
        pl.when(pl.program_id(0) == 0)(init_state)
    else:
        init_state()

    xb = x_ref[...].reshape(rows, D_MODEL).astype(BF16)
    for lo, hi in ((c0, c0 + 2 * W_CONV), (0, W_SSM), (W_SSM, c0), (c0 + 2 * W_CONV, IN_COLS)):
        z_ref[:, lo:hi] = (jnp.dot(xb, win_ref[:, lo:hi], preferred_element_type=F32)
                           + bin_ref[:, lo:hi])
    v = z_ref[:, c0:c0 + W_CONV] * jax.nn.sigmoid(z_ref[:, c0 + W_CONV:c0 + 2 * W_CONV])
    for j in range(CONV_TILES):
        _store_bt(slab_ref, SSM_TILES + j, v[:, LANES * j:LANES * (j + 1)], nb, tl, pitch)

    for c in range(CONV_TILES):
        lanes = slice(LANES * c, LANES * (c + 1))
        full_ref[hist:hist + rows, lanes] = _gather_tb(slab_ref, SSM_TILES + c, nb, tl, pitch)
    for c in range(CONV_TILES):
        lanes = slice(LANES * c, LANES * (c + 1))
        w = [wdw_ref[tap, :, lanes] for tap in range(CONV_K)]
        for m in range(nsub):
            for t0 in range(0, tl, conv_steps):
                base = t0 * nb + SUBLANES * m
                accs = [None] * conv_steps
                for j in range(conv_steps + CONV_K - 1):
                    xj = full_ref[base + j * nb:base + j * nb + SUBLANES, lanes]
                    for r in range(conv_steps):
                        tap = j - r
                        if 0 <= tap < CONV_K:
                            term = w[tap] * xj
                            accs[r] = term if accs[r] is None else accs[r] + term
                for r in range(conv_steps):
                    slab_ref[SSM_TILES + c, _tb_rows(m, t0 + r, pitch), :] = accs[r]
    cbuf_ref[...] = full_ref[rows:rows + hist, :]
    if n_steps > 1:
        full_ref[0:hist, :] = full_ref[rows:rows + hist, :]
    for j in range(SSM_TILES):
        _store_bt(slab_ref, j, z_ref[:, LANES * j:LANES * (j + 1)], nb, tl, pitch)

    half = rows // 2
    for k in range(N_BLOCKS):
        cs = slice(LANES * k, LANES * (k + 1))
        ss = slice(BLOCK_STATE * k, BLOCK_STATE * (k + 1))
        pieces = {(t, m): slab_ref[k, _tb_rows(m, t, pitch), :]
                  for t in range(tl) for m in range(nsub)}
        u_pair = jnp.concatenate(
            [jnp.concatenate([pieces[2 * j, m], pieces[2 * j + 1, m]], axis=1)
             for j in range(tl // 2) for m in range(nsub)], axis=0)
        u_bf = u_pair.astype(BF16)
        bu_ref[0:nb, 0:BLOCK_STATE] = hre_ref[:, ss]
        bu_ref[0:nb, BLOCK_STATE:2 * BLOCK_STATE] = him_ref[:, ss]
        bu_ref[nb:nb + half, :] = jnp.dot(u_bf, sin_ref[k], preferred_element_type=F32)
        lr = l2re_ref[:, ss]
        li = l2im_ref[:, ss]
        for m in range(nsub):
            rs = slice(SUBLANES * m, SUBLANES * (m + 1))
            hr, hi = bu_ref[rs, 0:BLOCK_STATE], bu_ref[rs, BLOCK_STATE:2 * BLOCK_STATE]
            for j in range(tl // 2):
                row = slice((j + 1) * nb + SUBLANES * m, (j + 1) * nb + SUBLANES * (m + 1))
                br = bu_ref[row, 0:BLOCK_STATE]
                bi = bu_ref[row, BLOCK_STATE:2 * BLOCK_STATE]
                hr, hi = lr * hr - li * hi + br, lr * hi + li * hr + bi
                bu_ref[row, 0:BLOCK_STATE] = hr
                bu_ref[row, BLOCK_STATE:2 * BLOCK_STATE] = hi
            hre_ref[rs, ss] = hr
            him_ref[rs, ss] = hi
        dsk = dskip_ref[:, cs]
        y_pair = (jnp.dot(bu_ref[0:half, :].astype(BF16), sout_ref[k], preferred_element_type=F32)
                  + jnp.dot(u_bf, sdir_ref[k], preferred_element_type=F32)
                  + u_pair * jnp.concatenate([dsk, dsk], axis=1))
        for j in range(tl // 2):
            for m in range(nsub):
                r0 = j * nb + SUBLANES * m
                slab_ref[k, _tb_rows(m, 2 * j, pitch), :] = y_pair[r0:r0 + SUBLANES, 0:LANES]
                slab_ref[k, _tb_rows(m, 2 * j + 1, pitch), :] = (
                    y_pair[r0:r0 + SUBLANES, LANES:2 * LANES])

    ys = jnp.concatenate([_load_bt(slab_ref, j, nb, tl, pitch) for j in range(SSM_TILES)], axis=1)
    sg = jax.nn.gelu(ys)
    glu = jnp.dot(sg.astype(BF16), wglu_ref[...], preferred_element_type=F32) + bglu_ref[...]
    s = sg * jax.nn.sigmoid(glu) * jax.nn.silu(z_ref[:, W_SSM:c0])
    mix_ref[:, 0:W_SSM] = s.astype(BF16)

    cv = jnp.concatenate([_load_bt(slab_ref, SSM_TILES + j, nb, tl, pitch)
                          for j in range(CONV_TILES)], axis=1) + bdw_ref[...]
    act = jax.nn.silu(_layernorm(cv, gcl_ref[...], bcl_ref[...]))
    cpw = jnp.dot(act.astype(BF16), wpw_ref[...], preferred_element_type=F32) + bpw_ref[...]
    cg = cpw * jax.nn.silu(z_ref[:, c0 + 2 * W_CONV:IN_COLS])
    mix_ref[:, W_SSM:W_SSM + W_CONV] = cg.astype(BF16)

    mix = jnp.dot(mix_ref[...], wout_ref[...], preferred_element_type=F32)
    pre = alpha * x_ref[...].reshape(rows, D_MODEL) + mix
    y_ref[...] = _layernorm(pre, gpost_ref[...], bpost_ref[...]).reshape(y_ref.shape)


def _layer_call(x, state, consts, *, nb, tl, n_steps, alpha, conv_steps, name):
    rows = nb * tl
    hist = (CONV_K - 1) * nb
    pitch = _slab_pitch(tl)
    assert nb % SUBLANES == 0 and tl % conv_steps == 0 and (nb * pitch) % SUBLANES == 0
    assert tl % 2 == 0 and (n_steps == 1 or (tl >= CONV_K - 1 and tl % SUBLANES == 0))
    out_shape = (
        jax.ShapeDtypeStruct(x.shape, F32),
        jax.ShapeDtypeStruct((nb, N_STATE_ALL), F32),
        jax.ShapeDtypeStruct((nb, N_STATE_ALL), F32),
        jax.ShapeDtypeStruct((hist, W_CONV), F32),
    )
    scratch = [
        pltpu.VMEM((rows, IN_COLS), F32),
        pltpu.VMEM((rows // 2 + nb, 2 * BLOCK_STATE), F32),
        pltpu.VMEM((SSM_TILES + CONV_TILES, nb * pitch, LANES), F32),
        pltpu.VMEM((hist + rows, W_CONV), F32),
        pltpu.VMEM((rows, W_SSM + W_CONV), BF16),
    ]
    operands = (x,) + (() if state is None else tuple(state)) + tuple(consts)
    kernel = functools.partial(_layer_kernel, nb, tl, n_steps, alpha, conv_steps, state is None)
    if n_steps == 1:
        assert x.shape == (rows, D_MODEL)
        return pl.pallas_call(
            kernel, out_shape=out_shape, scratch_shapes=scratch, name=name,
            compiler_params=pltpu.CompilerParams(vmem_limit_bytes=VMEM_LIMIT_BYTES),
        )(*operands)

    assert x.shape == (nb, tl * n_steps, D_MODEL)

    def whole(a):
        zeros = (0,) * a.ndim
        return pl.BlockSpec(a.shape, lambda i: zeros)

    x_spec = pl.BlockSpec((nb, tl, D_MODEL), lambda i: (0, i, 0))
    in_specs = [x_spec] + [whole(a) for a in operands[1:]]
    out_specs = (
        x_spec,
        pl.BlockSpec((nb, N_STATE_ALL), lambda i: (0, 0)),
        pl.BlockSpec((nb, N_STATE_ALL), lambda i: (0, 0)),
        pl.BlockSpec((hist, W_CONV), lambda i: (0, 0)),
    )
    return pl.pallas_call(
        kernel, out_shape=out_shape, grid=(n_steps,), in_specs=in_specs, out_specs=out_specs,
        scratch_shapes=scratch, name=name,
        compiler_params=pltpu.CompilerParams(dimension_semantics=("arbitrary",),
                                             vmem_limit_bytes=VMEM_LIMIT_BYTES),
    )(*operands)


def _prep_call(lam_re, lam_im, log_dt, b_re, b_im, c_re, c_im):
    gp = N_GROUPS * GROUP_P
    tile_b = lambda a: jnp.tile(a, (1, GROUPS_PER_BLOCK))
    ldt_n = jnp.broadcast_to(log_dt[:, None], (N_GROUPS, N_STATE))
    lam = jnp.stack([tile_b(a) for a in (lam_re, lam_im, ldt_n)])
    lamrow = jnp.stack([a.reshape(1, N_STATE_ALL) for a in (lam_re, lam_im, ldt_n)])
    bt = jnp.stack([tile_b(b.transpose(0, 2, 1).reshape(gp, N_STATE)) for b in (b_re, b_im)])
    cg = jnp.stack([tile_b(c.reshape(gp, N_STATE)) for c in (c_re, c_im)])
    out_shape = (
        jax.ShapeDtypeStruct((N_BLOCKS, 2 * LANES, 2 * BLOCK_STATE), BF16),
        jax.ShapeDtypeStruct((N_BLOCKS, 2 * BLOCK_STATE, 2 * LANES), BF16),
        jax.ShapeDtypeStruct((N_BLOCKS, 2 * LANES, 2 * LANES), BF16),
        jax.ShapeDtypeStruct((SUBLANES, N_STATE_ALL), F32),
        jax.ShapeDtypeStruct((SUBLANES, N_STATE_ALL), F32),
    )
    return pl.pallas_call(_prep_kernel, out_shape=out_shape, name="s5_prep")(lam, lamrow, bt, cg)


def kernel(x_prompt, x_sample, state_ssm_re, state_ssm_im, state_conv, w_in, b_in, lam_re, lam_im, log_dt, b_re, b_im, c_re, c_im, d_skip, w_glu, b_glu, w_dw, b_dw, g_conv_ln, b_conv_ln, w_pw2, b_pw2, w_out, g_post, b_post):
    depth = w_in.shape[0]
    alpha = (2.0 * depth) ** 0.25
    bsz, seq, _ = x_prompt.shape
    dbsz, dseq, _ = x_sample.shape
    tl_prompt = 64

    hp = x_prompt
    hs = x_sample.reshape(dbsz * dseq, D_MODEL)
    outs = {k: [] for k in ("re_p", "im_p", "cv_p", "re_s", "im_s", "cv_s")}
    row = lambda a: a.reshape(1, -1)
    for layer in range(depth):
        s_in, s_out, s_dir, l2re, l2im = _prep_call(
            lam_re[layer], lam_im[layer], log_dt[layer], b_re[layer], b_im[layer],
            c_re[layer], c_im[layer])
        wdw8 = jnp.broadcast_to(w_dw[layer][:, None, :], (CONV_K, SUBLANES, W_CONV))
        consts = (w_in[layer].astype(BF16), row(b_in[layer]), l2re, l2im, s_in, s_out, s_dir,
                  row(d_skip[layer]), w_glu[layer].astype(BF16), row(b_glu[layer]), wdw8,
                  row(b_dw[layer]), row(g_conv_ln[layer]), row(b_conv_ln[layer]),
                  w_pw2[layer].astype(BF16), row(b_pw2[layer]), w_out[layer].astype(BF16),
                  row(g_post[layer]), row(b_post[layer]))

        hp, re_p, im_p, cv_p = _layer_call(
            hp, None, consts, nb=bsz, tl=tl_prompt,
            n_steps=seq // tl_prompt, alpha=alpha, conv_steps=8, name="layer_prompt")

        h0re = state_ssm_re[layer].reshape(dbsz, N_STATE_ALL)
        h0im = state_ssm_im[layer].reshape(dbsz, N_STATE_ALL)
        cbuf0 = state_conv[layer].transpose(1, 0, 2).reshape((CONV_K - 1) * dbsz, W_CONV)
        hs, re_s, im_s, cv_s = _layer_call(
            hs, (h0re, h0im, cbuf0), consts, nb=dbsz, tl=dseq, n_steps=1, alpha=alpha,
            conv_steps=dseq, name="layer_sample")

        unrow = lambda a, n: a.reshape(CONV_K - 1, n, W_CONV).transpose(1, 0, 2)
        outs["re_p"].append(re_p.reshape(bsz, N_GROUPS, N_STATE))
        outs["im_p"].append(im_p.reshape(bsz, N_GROUPS, N_STATE))
        outs["cv_p"].append(unrow(cv_p, bsz))
        outs["re_s"].append(re_s.reshape(dbsz, N_GROUPS, N_STATE))
        outs["im_s"].append(im_s.reshape(dbsz, N_GROUPS, N_STATE))
        outs["cv_s"].append(unrow(cv_s, dbsz))

    y_sample = hs.reshape(dbsz, dseq, D_MODEL)
    return (hp, y_sample,
            jnp.stack(outs["re_p"]), jnp.stack(outs["im_p"]), jnp.stack(outs["cv_p"]),
            jnp.stack(outs["re_s"]), jnp.stack(outs["im_s"]), jnp.stack(outs["cv_s"]))
```

```python
import functools

import jax
import jax.numpy as jnp
from jax import lax
from jax.experimental import pallas as pl
from jax.experimental.pallas import tpu as pltpu

D_MODEL = 1024
W_SSM = 512
W_CONV = 512
GROUP_P = 16
N_GROUPS = 32
N_STATE = 64
CONV_K = 31
IN_COLS = 2 * W_SSM + 3 * W_CONV
LN_EPS = 1e-5

LANES = 128
SUBLANES = 8
GROUPS_PER_BLOCK = LANES // GROUP_P
N_BLOCKS = N_GROUPS // GROUPS_PER_BLOCK
BLOCK_STATE = GROUPS_PER_BLOCK * N_STATE
N_STATE_ALL = N_GROUPS * N_STATE
SSM_TILES = W_SSM // LANES
CONV_TILES = W_CONV // LANES
VMEM_LIMIT_BYTES = 58 * 1024 * 1024

F32 = jnp.float32
BF16 = jnp.bfloat16


def _layernorm(x, g, b):
    mu = jnp.mean(x, axis=-1, keepdims=True)
    xc = x - mu
    var = jnp.mean(xc * xc, axis=-1, keepdims=True)
    return xc * lax.rsqrt(var + LN_EPS) * g + b


def _slab_pitch(tl):
    if tl % SUBLANES:
        return tl
    pitch = tl
    while (pitch // SUBLANES) % 2 == 0:
        pitch += SUBLANES
    return pitch


def _tb_rows(m, t, pitch):
    return pl.ds(SUBLANES * m * pitch + t, SUBLANES, stride=pitch)


def _store_bt(slab_ref, j, val, nb, tl, pitch):
    if pitch == tl:
        slab_ref[j] = val
    else:
        for b in range(nb):
            slab_ref[j, b * pitch:b * pitch + tl, :] = val[b * tl:(b + 1) * tl]


def _load_bt(slab_ref, j, nb, tl, pitch):
    if pitch == tl:
        return slab_ref[j]
    return jnp.concatenate([slab_ref[j, b * pitch:b * pitch + tl, :] for b in range(nb)], axis=0)


def _gather_tb(slab_ref, j, nb, tl, pitch):
    pieces = [slab_ref[j, _tb_rows(m, t, pitch), :]
              for t in range(tl) for m in range(nb // SUBLANES)]
    return jnp.concatenate(pieces, axis=0)


def _prep_kernel(lam_ref, lamrow_ref, bt_ref, cg_ref,
                 sin_ref, sout_ref, sdir_ref, l2re_ref, l2im_ref):
    gp = N_GROUPS * GROUP_P
    abt = (((1,), (1,)), ((), ()))

    def zoh(lr, li, ldt):
        dt = jnp.exp(ldt)
        mag = jnp.exp(lr * dt)
        ang = li * dt
        return mag * jnp.cos(ang), mag * jnp.sin(ang)

    def per_channel(a):
        wide = jnp.broadcast_to(a[:, None, :], (N_GROUPS, GROUP_P, BLOCK_STATE))
        return wide.reshape(gp, BLOCK_STATE)

    lr, li = lam_ref[0], lam_ref[1]
    ar, ai = zoh(lr, li, lam_ref[2])
    nr, ni = ar - 1.0, ai
    den = lr * lr + li * li
    cr = per_channel((nr * lr + ni * li) / den)
    ci = per_channel((ni * lr - nr * li) / den)
    ar, ai = per_channel(ar), per_channel(ai)
    shape = ar.shape
    row_g = lax.broadcasted_iota(jnp.int32, shape, 0) // GROUP_P % GROUPS_PER_BLOCK
    col_g = lax.broadcasted_iota(jnp.int32, shape, 1) // N_STATE
    keep = row_g == col_g
    br, bi = bt_ref[0], bt_ref[1]
    bbr = jnp.where(keep, cr * br - ci * bi, 0.0)
    bbi = jnp.where(keep, cr * bi + ci * br, 0.0)
    lbr = ar * bbr - ai * bbi
    lbi = ar * bbi + ai * bbr
    c0r = jnp.where(keep, cg_ref[0], 0.0)
    c0i = jnp.where(keep, cg_ref[1], 0.0)
    c1r = ar * c0r - ai * c0i
    c1i = ar * c0i + ai * c0r
    c2r = ar * c1r - ai * c1i
    c2i = ar * c1i + ai * c1r

    for k in range(N_BLOCKS):
        rs = slice(LANES * k, LANES * (k + 1))
        b_k = jnp.concatenate([bbr[rs], bbi[rs]], axis=1)
        lb_k = jnp.concatenate([lbr[rs], lbi[rs]], axis=1)
        sin_ref[k, 0:LANES, :] = lb_k.astype(BF16)
        sin_ref[k, LANES:2 * LANES, :] = b_k.astype(BF16)
        c_k = jnp.concatenate([c0r[rs], -c0i[rs]], axis=1)
        c12_k = jnp.concatenate([jnp.concatenate([c1r[rs], -c1i[rs]], axis=1),
                                 jnp.concatenate([c2r[rs], -c2i[rs]], axis=1)], axis=0)
        sout_ref[k] = c12_k.T.astype(BF16)
        c_bf = c_k.astype(BF16)
        cb = lax.dot_general(b_k.astype(BF16), c_bf, abt, preferred_element_type=F32)
        clb = lax.dot_general(lb_k.astype(BF16), c_bf, abt, preferred_element_type=F32)
        sdir_ref[k, 0:LANES, 0:LANES] = cb.astype(BF16)
        sdir_ref[k, 0:LANES, LANES:2 * LANES] = clb.astype(BF16)
        sdir_ref[k, LANES:2 * LANES, 0:LANES] = jnp.zeros((LANES, LANES), BF16)
        sdir_ref[k, LANES:2 * LANES, LANES:2 * LANES] = cb.astype(BF16)

    a_r, a_i = zoh(lamrow_ref[0], lamrow_ref[1], lamrow_ref[2])
    l2re_ref[...] = jnp.broadcast_to(a_r * a_r - a_i * a_i, l2re_ref.shape)
    l2im_ref[...] = jnp.broadcast_to(2.0 * a_r * a_i, l2im_ref.shape)


def _layer_kernel(nb, tl, n_steps, alpha, conv_steps, zero_state, x_ref, *refs):
    if zero_state:
        refs = (None, None, None) + refs
    _layer_body(nb, tl, n_steps, alpha, conv_steps, x_ref, *refs)


def _layer_body(nb, tl, n_steps, alpha, conv_steps,
                x_ref, h0re_ref, h0im_ref, cbuf0_ref, win_ref, bin_ref, l2re_ref, l2im_ref,
                sin_ref, sout_ref, sdir_ref, dskip_ref, wglu_ref, bglu_ref, wdw_ref, bdw_ref,
                gcl_ref, bcl_ref, wpw_ref, bpw_ref, wout_ref, gpost_ref, bpost_ref,
                y_ref, hre_ref, him_ref, cbuf_ref,
                z_ref, bu_ref, slab_ref, full_ref, mix_ref):
    rows = nb * tl
    hist = (CONV_K - 1) * nb
    pitch = _slab_pitch(tl)
    nsub = nb // SUBLANES
    c0 = 2 * W_SSM

    def init_state():
        if h0re_ref is None:
            hre_ref[...] = jnp.zeros_like(hre_ref)
            him_ref[...] = jnp.zeros_like(him_ref)
            full_ref[0:hist, :] = jnp.zeros((hist, W_CONV), F32)
        else:
            hre_ref[...] = h0re_ref[...]
            him_ref[...] = h0im_ref[...]
            full_ref[0:hist, :] = cbuf0_ref[...]

    if n_steps > 1:
        pl.when(pl.program_id(0) == 0)(init_state)
    else:
        init_state()

    xb = x_ref[...].reshape(rows, D_MODEL).astype(BF16)

    def in_proj(lo, hi):
        z_ref[:, lo:hi] = (jnp.dot(xb, win_ref[:, lo:hi], preferred_element_type=F32)
                           + bin_ref[:, lo:hi])

    in_proj(c0, c0 + 2 * W_CONV)
    in_proj(0, W_SSM)
    v = z_ref[:, c0:c0 + W_CONV] * jax.nn.sigmoid(z_ref[:, c0 + W_CONV:c0 + 2 * W_CONV])
    for j in range(CONV_TILES):
        _store_bt(slab_ref, SSM_TILES + j, v[:, LANES * j:LANES * (j + 1)], nb, tl, pitch)

    for c in range(CONV_TILES):
        lanes = slice(LANES * c, LANES * (c + 1))
        full_ref[hist:hist + rows, lanes] = _gather_tb(slab_ref, SSM_TILES + c, nb, tl, pitch)
    for c in range(CONV_TILES):
        lanes = slice(LANES * c, LANES * (c + 1))
        w = [wdw_ref[tap, :, lanes] for tap in range(CONV_K)]
        for m in range(nsub):
            for t0 in range(0, tl, conv_steps):
                base = t0 * nb + SUBLANES * m
                accs = [None] * conv_steps
                for j in range(conv_steps + CONV_K - 1):
                    xj = full_ref[base + j * nb:base + j * nb + SUBLANES, lanes]
                    for r in range(conv_steps):
                        tap = j - r
                        if 0 <= tap < CONV_K:
                            term = w[tap] * xj
                            accs[r] = term if accs[r] is None else accs[r] + term
                for r in range(conv_steps):
                    slab_ref[SSM_TILES + c, _tb_rows(m, t0 + r, pitch), :] = accs[r]
    cbuf_ref[...] = full_ref[rows:rows + hist, :]
    if n_steps > 1:
        full_ref[0:hist, :] = full_ref[rows:rows + hist, :]
    for j in range(SSM_TILES):
        _store_bt(slab_ref, j, z_ref[:, LANES * j:LANES * (j + 1)], nb, tl, pitch)

    half = rows // 2
    for k in range(N_BLOCKS):
        cs = slice(LANES * k, LANES * (k + 1))
        ss = slice(BLOCK_STATE * k, BLOCK_STATE * (k + 1))
        pieces = {(t, m): slab_ref[k, _tb_rows(m, t, pitch), :]
                  for t in range(tl) for m in range(nsub)}
        u_pair = jnp.concatenate(
            [jnp.concatenate([pieces[2 * j, m], pieces[2 * j + 1, m]], axis=1)
             for j in range(tl // 2) for m in range(nsub)], axis=0)
        u_bf = u_pair.astype(BF16)
        bu_ref[0:nb, 0:BLOCK_STATE] = hre_ref[:, ss]
        bu_ref[0:nb, BLOCK_STATE:2 * BLOCK_STATE] = him_ref[:, ss]
        bu_ref[nb:nb + half, :] = jnp.dot(u_bf, sin_ref[k], preferred_element_type=F32)
        lr = l2re_ref[:, ss]
        li = l2im_ref[:, ss]
        for m in range(nsub):
            rs = slice(SUBLANES * m, SUBLANES * (m + 1))
            hr, hi = bu_ref[rs, 0:BLOCK_STATE], bu_ref[rs, BLOCK_STATE:2 * BLOCK_STATE]
            for j in range(tl // 2):
                row = slice((j + 1) * nb + SUBLANES * m, (j + 1) * nb + SUBLANES * (m + 1))
                br = bu_ref[row, 0:BLOCK_STATE]
                bi = bu_ref[row, BLOCK_STATE:2 * BLOCK_STATE]
                hr, hi = lr * hr - li * hi + br, lr * hi + li * hr + bi
                bu_ref[row, 0:BLOCK_STATE] = hr
                bu_ref[row, BLOCK_STATE:2 * BLOCK_STATE] = hi
            hre_ref[rs, ss] = hr
            him_ref[rs, ss] = hi
        dsk = dskip_ref[:, cs]
        y_pair = (jnp.dot(bu_ref[0:half, :].astype(BF16), sout_ref[k], preferred_element_type=F32)
                  + jnp.dot(u_bf, sdir_ref[k], preferred_element_type=F32)
                  + u_pair * jnp.concatenate([dsk, dsk], axis=1))
        for j in range(tl // 2):
            for m in range(nsub):
                r0 = j * nb + SUBLANES * m
                slab_ref[k, _tb_rows(m, 2 * j, pitch), :] = y_pair[r0:r0 + SUBLANES, 0:LANES]
                slab_ref[k, _tb_rows(m, 2 * j + 1, pitch), :] = (
                    y_pair[r0:r0 + SUBLANES, LANES:2 * LANES])

    ys = jnp.concatenate([_load_bt(slab_ref, j, nb, tl, pitch) for j in range(SSM_TILES)], axis=1)
    sg = jax.nn.gelu(ys)
    in_proj(W_SSM, c0)
    glu = jnp.dot(sg.astype(BF16), wglu_ref[...], preferred_element_type=F32) + bglu_ref[...]
    s = sg * jax.nn.sigmoid(glu) * jax.nn.silu(z_ref[:, W_SSM:c0])
    mix_ref[:, 0:W_SSM] = s.astype(BF16)

    cv = jnp.concatenate([_load_bt(slab_ref, SSM_TILES + j, nb, tl, pitch)
                          for j in range(CONV_TILES)], axis=1) + bdw_ref[...]
    act = jax.nn.silu(_layernorm(cv, gcl_ref[...], bcl_ref[...]))
    in_proj(c0 + 2 * W_CONV, IN_COLS)
    cpw = jnp.dot(act.astype(BF16), wpw_ref[...], preferred_element_type=F32) + bpw_ref[...]
    cg = cpw * jax.nn.silu(z_ref[:, c0 + 2 * W_CONV:IN_COLS])
    mix_ref[:, W_SSM:W_SSM + W_CONV] = cg.astype(BF16)

    mix = jnp.dot(mix_ref[...], wout_ref[...], preferred_element_type=F32)
    pre = alpha * x_ref[...].reshape(rows, D_MODEL) + mix
    y_ref[...] = _layernorm(pre, gpost_ref[...], bpost_ref[...]).reshape(y_ref.shape)


def _layer_call(x, state, consts, *, nb, tl, n_steps, alpha, conv_steps, name):
    rows = nb * tl
    hist = (CONV_K - 1) * nb
    pitch = _slab_pitch(tl)
    assert nb % SUBLANES == 0 and tl % conv_steps == 0 and (nb * pitch) % SUBLANES == 0
    assert tl % 2 == 0 and (n_steps == 1 or (tl >= CONV_K - 1 and tl % SUBLANES == 0))
    out_shape = (
        jax.ShapeDtypeStruct(x.shape, F32),
        jax.ShapeDtypeStruct((nb, N_STATE_ALL), F32),
        jax.ShapeDtypeStruct((nb, N_STATE_ALL), F32),
        jax.ShapeDtypeStruct((hist, W_CONV), F32),
    )
    scratch = [
        pltpu.VMEM((rows, IN_COLS), F32),
        pltpu.VMEM((rows // 2 + nb, 2 * BLOCK_STATE), F32),
        pltpu.VMEM((SSM_TILES + CONV_TILES, nb * pitch, LANES), F32),
        pltpu.VMEM((hist + rows, W_CONV), F32),
        pltpu.VMEM((rows, W_SSM + W_CONV), BF16),
    ]
    operands = (x,) + (() if state is None else tuple(state)) + tuple(consts)
    kernel = functools.partial(_layer_kernel, nb, tl, n_steps, alpha, conv_steps, state is None)
    if n_steps == 1:
        assert x.shape == (rows, D_MODEL)
        return pl.pallas_call(
            kernel, out_shape=out_shape, scratch_shapes=scratch, name=name,
            compiler_params=pltpu.CompilerParams(vmem_limit_bytes=VMEM_LIMIT_BYTES),
        )(*operands)

    assert x.shape == (nb, tl * n_steps, D_MODEL)

    def whole(a):
        zeros = (0,) * a.ndim
        return pl.BlockSpec(a.shape, lambda i: zeros)

    x_spec = pl.BlockSpec((nb, tl, D_MODEL), lambda i: (0, i, 0))
    in_specs = [x_spec] + [whole(a) for a in operands[1:]]
    out_specs = (
        x_spec,
        pl.BlockSpec((nb, N_STATE_ALL), lambda i: (0, 0)),
        pl.BlockSpec((nb, N_STATE_ALL), lambda i: (0, 0)),
        pl.BlockSpec((hist, W_CONV), lambda i: (0, 0)),
    )
    return pl.pallas_call(
        kernel, out_shape=out_shape, grid=(n_steps,), in_specs=in_specs, out_specs=out_specs,
        scratch_shapes=scratch, name=name,
        compiler_params=pltpu.CompilerParams(dimension_semantics=("arbitrary",),
                                             vmem_limit_bytes=VMEM_LIMIT_BYTES),
    )(*operands)


def _prep_call(lam_re, lam_im, log_dt, b_re, b_im, c_re, c_im):
    gp = N_GROUPS * GROUP_P
    tile_b = lambda a: jnp.tile(a, (1, GROUPS_PER_BLOCK))
    ldt_n = jnp.broadcast_to(log_dt[:, None], (N_GROUPS, N_STATE))
    lam = jnp.stack([tile_b(a) for a in (lam_re, lam_im, ldt_n)])
    lamrow = jnp.stack([a.reshape(1, N_STATE_ALL) for a in (lam_re, lam_im, ldt_n)])
    bt = jnp.stack([tile_b(b.transpose(0, 2, 1).reshape(gp, N_STATE)) for b in (b_re, b_im)])
    cg = jnp.stack([tile_b(c.reshape(gp, N_STATE)) for c in (c_re, c_im)])
    out_shape = (
        jax.ShapeDtypeStruct((N_BLOCKS, 2 * LANES, 2 * BLOCK_STATE), BF16),
        jax.ShapeDtypeStruct((N_BLOCKS, 2 * BLOCK_STATE, 2 * LANES), BF16),
        jax.ShapeDtypeStruct((N_BLOCKS, 2 * LANES, 2 * LANES), BF16),
        jax.ShapeDtypeStruct((SUBLANES, N_STATE_ALL), F32),
        jax.ShapeDtypeStruct((SUBLANES, N_STATE_ALL), F32),
    )
    return pl.pallas_call(_prep_kernel, out_shape=out_shape, name="s5_prep")(lam, lamrow, bt, cg)


def kernel(x_prompt, x_sample, state_ssm_re, state_ssm_im, state_conv, w_in, b_in, lam_re, lam_im, log_dt, b_re, b_im, c_re, c_im, d_skip, w_glu, b_glu, w_dw, b_dw, g_conv_ln, b_conv_ln, w_pw2, b_pw2, w_out, g_post, b_post):
    depth = w_in.shape[0]
    alpha = (2.0 * depth) ** 0.25
    bsz, seq, _ = x_prompt.shape
    dbsz, dseq, _ = x_sample.shape
    tl_prompt = 64

    hp = x_prompt
    hs = x_sample.reshape(dbsz * dseq, D_MODEL)
    outs = {k: [] for k in ("re_p", "im_p", "cv_p", "re_s", "im_s", "cv_s")}
    row = lambda a: a.reshape(1, -1)
    for layer in range(depth):
        s_in, s_out, s_dir, l2re, l2im = _prep_call(
            lam_re[layer], lam_im[layer], log_dt[layer], b_re[layer], b_im[layer],
            c_re[layer], c_im[layer])
        wdw8 = jnp.broadcast_to(w_dw[layer][:, None, :], (CONV_K, SUBLANES, W_CONV))
        consts = (w_in[layer].astype(BF16), row(b_in[layer]), l2re, l2im, s_in, s_out, s_dir,
                  row(d_skip[layer]), w_glu[layer].astype(BF16), row(b_glu[layer]), wdw8,
                  row(b_dw[layer]), row(g_conv_ln[layer]), row(b_conv_ln[layer]),
                  w_pw2[layer].astype(BF16), row(b_pw2[layer]), w_out[layer].astype(BF16),
                  row(g_post[layer]), row(b_post[layer]))

        hp, re_p, im_p, cv_p = _layer_call(
            hp, None, consts, nb=bsz, tl=tl_prompt,
            n_steps=seq // tl_prompt, alpha=alpha, conv_steps=8, name="layer_prompt")

        h0re = state_ssm_re[layer].reshape(dbsz, N_STATE_ALL)
        h0im = state_ssm_im[layer].reshape(dbsz, N_STATE_ALL)
        cbuf0 = state_conv[layer].transpose(1, 0, 2).reshape((CONV_K - 1) * dbsz, W_CONV)
        hs, re_s, im_s, cv_s = _layer_call(
            hs, (h0re, h0im, cbuf0), consts, nb=dbsz, tl=dseq, n_steps=1, alpha=alpha,
            conv_steps=dseq, name="layer_sample")

        unrow = lambda a, n: a.reshape(CONV_K - 1, n, W_CONV).transpose(1, 0, 2)
        outs["re_p"].append(re_p.reshape(bsz, N_GROUPS, N_STATE))
        outs["im_p"].append(im_p.reshape(bsz, N_GROUPS, N_STATE))
        outs["cv_p"].append(unrow(cv_p, bsz))
        outs["re_s"].append(re_s.reshape(dbsz, N_GROUPS, N_STATE))
        outs["im_s"].append(im_s.reshape(dbsz, N_GROUPS, N_STATE))
        outs["cv_s"].append(unrow(cv_s, dbsz))

    y_sample = hs.reshape(dbsz, dseq, D_MODEL)
    return (hp, y_sample,
            jnp.stack(outs["re_p"]), jnp.stack(outs["im_p"]), jnp.stack(outs["cv_p"]),
            jnp.stack(outs["re_s"]), jnp.stack(outs["im_s"]), jnp.stack(outs["cv_s"]))
```

```python
import functools

import jax
import jax.numpy as jnp
from jax import lax
from jax.experimental import pallas as pl
from jax.experimental.pallas import tpu as pltpu

D_MODEL = 1024
W_SSM = 512
W_CONV = 512
GROUP_P = 16
N_GROUPS = 32
N_STATE = 64
CONV_K = 31
IN_COLS = 2 * W_SSM + 3 * W_CONV
LN_EPS = 1e-5

LANES = 128
SUBLANES = 8
GROUPS_PER_BLOCK = LANES // GROUP_P
N_BLOCKS = N_GROUPS // GROUPS_PER_BLOCK
BLOCK_STATE = GROUPS_PER_BLOCK * N_STATE
N_STATE_ALL = N_GROUPS * N_STATE
SSM_TILES = W_SSM // LANES
CONV_TILES = W_CONV // LANES
VMEM_LIMIT_BYTES = 58 * 1024 * 1024

F32 = jnp.float32
BF16 = jnp.bfloat16


def _layernorm(x, g, b):
    mu = jnp.mean(x, axis=-1, keepdims=True)
    xc = x - mu
    var = jnp.mean(xc * xc, axis=-1, keepdims=True)
    return xc * lax.rsqrt(var + LN_EPS) * g + b


def _slab_pitch(tl):
    if tl % SUBLANES:
        return tl
    pitch = tl
    while (pitch // SUBLANES) % 2 == 0:
        pitch += SUBLANES
    return pitch


def _tb_rows(m, t, pitch):
    return pl.ds(SUBLANES * m * pitch + t, SUBLANES, stride=pitch)


def _store_bt(slab_ref, j, val, nb, tl, pitch):
    if pitch == tl:
        slab_ref[j] = val
    else:
        for b in range(nb):
            slab_ref[j, b * pitch:b * pitch + tl, :] = val[b * tl:(b + 1) * tl]


def _load_bt(slab_ref, j, nb, tl, pitch):
    if pitch == tl:
        return slab_ref[j]
    return jnp.concatenate([slab_ref[j, b * pitch:b * pitch + tl, :] for b in range(nb)], axis=0)


def _gather_tb(slab_ref, j, nb, tl, pitch):
    pieces = [slab_ref[j, _tb_rows(m, t, pitch), :]
              for t in range(tl) for m in range(nb // SUBLANES)]
    return jnp.concatenate(pieces, axis=0)


def _prep_kernel(lam_ref, lamrow_ref, bt_ref, cg_ref,
                 sin_ref, sout_ref, sdir_ref, l2re_ref, l2im_ref):
    gp = N_GROUPS * GROUP_P
    abt = (((1,), (1,)), ((), ()))

    def zoh(lr, li, ldt):
        dt = jnp.exp(ldt)
        mag = jnp.exp(lr * dt)
        ang = li * dt
        return mag * jnp.cos(ang), mag * jnp.sin(ang)

    def per_channel(a):
        wide = jnp.broadcast_to(a[:, None, :], (N_GROUPS, GROUP_P, BLOCK_STATE))
        return wide.reshape(gp, BLOCK_STATE)

    lr, li = lam_ref[0], lam_ref[1]
    ar, ai = zoh(lr, li, lam_ref[2])
    nr, ni = ar - 1.0, ai
    den = lr * lr + li * li
    cr = per_channel((nr * lr + ni * li) / den)
    ci = per_channel((ni * lr - nr * li) / den)
    ar, ai = per_channel(ar), per_channel(ai)
    shape = ar.shape
    row_g = lax.broadcasted_iota(jnp.int32, shape, 0) // GROUP_P % GROUPS_PER_BLOCK
    col_g = lax.broadcasted_iota(jnp.int32, shape, 1) // N_STATE
    keep = row_g == col_g
    br, bi = bt_ref[0], bt_ref[1]
    bbr = jnp.where(keep, cr * br - ci * bi, 0.0)
    bbi = jnp.where(keep, cr * bi + ci * br, 0.0)
    lbr = ar * bbr - ai * bbi
    lbi = ar * bbi + ai * bbr
    c0r = jnp.where(keep, cg_ref[0], 0.0)
    c0i = jnp.where(keep, cg_ref[1], 0.0)
    c1r = ar * c0r - ai * c0i
    c1i = ar * c0i + ai * c0r
    c2r = ar * c1r - ai * c1i
    c2i = ar * c1i + ai * c1r

    for k in range(N_BLOCKS):
        rs = slice(LANES * k, LANES * (k + 1))
        b_k = jnp.concatenate([bbr[rs], bbi[rs]], axis=1)
        lb_k = jnp.concatenate([lbr[rs], lbi[rs]], axis=1)
        sin_ref[k, 0:LANES, :] = lb_k.astype(BF16)
        sin_ref[k, LANES:2 * LANES, :] = b_k.astype(BF16)
        c_k = jnp.concatenate([c0r[rs], -c0i[rs]], axis=1)
        c12_k = jnp.concatenate([jnp.concatenate([c1r[rs], -c1i[rs]], axis=1),
                                 jnp.concatenate([c2r[rs], -c2i[rs]], axis=1)], axis=0)
        sout_ref[k] = c12_k.T.astype(BF16)
        c_bf = c_k.astype(BF16)
        cb = lax.dot_general(b_k.astype(BF16), c_bf, abt, preferred_element_type=F32)
        clb = lax.dot_general(lb_k.astype(BF16), c_bf, abt, preferred_element_type=F32)
        sdir_ref[k, 0:LANES, 0:LANES] = cb.astype(BF16)
        sdir_ref[k, 0:LANES, LANES:2 * LANES] = clb.astype(BF16)
        sdir_ref[k, LANES:2 * LANES, 0:LANES] = jnp.zeros((LANES, LANES), BF16)
        sdir_ref[k, LANES:2 * LANES, LANES:2 * LANES] = cb.astype(BF16)

    a_r, a_i = zoh(lamrow_ref[0], lamrow_ref[1], lamrow_ref[2])
    l2re_ref[...] = jnp.broadcast_to(a_r * a_r - a_i * a_i, l2re_ref.shape)
    l2im_ref[...] = jnp.broadcast_to(2.0 * a_r * a_i, l2im_ref.shape)


def _layer_kernel(nb, tl, n_steps, alpha, conv_steps, zero_state, x_ref, *refs):
    if zero_state:
        refs = (None, None, None) + refs
    _layer_body(nb, tl, n_steps, alpha, conv_steps, x_ref, *refs)


def _layer_body(nb, tl, n_steps, alpha, conv_steps,
                x_ref, h0re_ref, h0im_ref, cbuf0_ref, win_ref, bin_ref, l2re_ref, l2im_ref,
                sin_ref, sout_ref, sdir_ref, dskip_ref, wglu_ref, bglu_ref, wdw_ref, bdw_ref,
                gcl_ref, bcl_ref, wpw_ref, bpw_ref, wout_ref, gpost_ref, bpost_ref,
                y_ref, hre_ref, him_ref, cbuf_ref,
                z_ref, bu_ref, slab_ref, full_ref, mix_ref):
    rows = nb * tl
    hist = (CONV_K - 1) * nb
    pitch = _slab_pitch(tl)
    nsub = nb // SUBLANES
    c0 = 2 * W_SSM

    def init_state():
        if h0re_ref is None:
            hre_ref[...] = jnp.zeros_like(hre_ref)
            him_ref[...] = jnp.zeros_like(him_ref)
            full_ref[0:hist, :] = jnp.zeros((hist, W_CONV), F32)
        else:
            hre_ref[...] = h0re_ref[...]
            him_ref[...] = h0im_ref[...]
            full_ref[0:hist, :] = cbuf0_ref[...]

    if n_steps > 1:
        pl.when(pl.program_id(0) == 0)(init_state)
    else:
        init_state()

    xb = x_ref[...].reshape(rows, D_MODEL).astype(BF16)

    def in_proj(lo, hi):
        z_ref[:, lo:hi] = (jnp.dot(xb, win_ref[:, lo:hi], preferred_element_type=F32)
                           + bin_ref[:, lo:hi])

    cw = 2 * LANES
    for h in range(CONV_TILES // 2):
        a_lo, b_lo = c0 + h * cw, c0 + W_CONV + h * cw
        in_proj(a_lo, a_lo + cw)
        in_proj(b_lo, b_lo + cw)
        v = z_ref[:, a_lo:a_lo + cw] * jax.nn.sigmoid(z_ref[:, b_lo:b_lo + cw])
        for j in range(2):
            _store_bt(slab_ref, SSM_TILES + 2 * h + j, v[:, LANES * j:LANES * (j + 1)],
                      nb, tl, pitch)
        if h == 0:
            in_proj(0, W_SSM)

    for c in range(CONV_TILES):
        lanes = slice(LANES * c, LANES * (c + 1))
        full_ref[hist:hist + rows, lanes] = _gather_tb(slab_ref, SSM_TILES + c, nb, tl, pitch)
    for c in range(CONV_TILES):
        lanes = slice(LANES * c, LANES * (c + 1))
        w = [wdw_ref[tap, :, lanes] for tap in range(CONV_K)]
        for m in range(nsub):
            for t0 in range(0, tl, conv_steps):
                base = t0 * nb + SUBLANES * m
                accs = [None] * conv_steps
                for j in range(conv_steps + CONV_K - 1):
                    xj = full_ref[base + j * nb:base + j * nb + SUBLANES, lanes]
                    for r in range(conv_steps):
                        tap = j - r
                        if 0 <= tap < CONV_K:
                            term = w[tap] * xj
                            accs[r] = term if accs[r] is None else accs[r] + term
                for r in range(conv_steps):
                    slab_ref[SSM_TILES + c, _tb_rows(m, t0 + r, pitch), :] = accs[r]
    cbuf_ref[...] = full_ref[rows:rows + hist, :]
    if n_steps > 1:
        full_ref[0:hist, :] = full_ref[rows:rows + hist, :]
    for j in range(SSM_TILES):
        _store_bt(slab_ref, j, z_ref[:, LANES * j:LANES * (j + 1)], nb, tl, pitch)

    half = rows // 2
    for k in range(N_BLOCKS):
        cs = slice(LANES * k, LANES * (k + 1))
        ss = slice(BLOCK_STATE * k, BLOCK_STATE * (k + 1))
        pieces = {(t, m): slab_ref[k, _tb_rows(m, t, pitch), :]
                  for t in range(tl) for m in range(nsub)}
        u_pair = jnp.concatenate(
            [jnp.concatenate([pieces[2 * j, m], pieces[2 * j + 1, m]], axis=1)
             for j in range(tl // 2) for m in range(nsub)], axis=0)
        u_bf = u_pair.astype(BF16)
        bu_ref[0:nb, 0:BLOCK_STATE] = hre_ref[:, ss]
        bu_ref[0:nb, BLOCK_STATE:2 * BLOCK_STATE] = him_ref[:, ss]
        bu_ref[nb:nb + half, :] = jnp.dot(u_bf, sin_ref[k], preferred_element_type=F32)
        lr = l2re_ref[:, ss]
        li = l2im_ref[:, ss]
        for m in range(nsub):
            rs = slice(SUBLANES * m, SUBLANES * (m + 1))
            hr, hi = bu_ref[rs, 0:BLOCK_STATE], bu_ref[rs, BLOCK_STATE:2 * BLOCK_STATE]
            for j in range(tl // 2):
                row = slice((j + 1) * nb + SUBLANES * m, (j + 1) * nb + SUBLANES * (m + 1))
                br = bu_ref[row, 0:BLOCK_STATE]
                bi = bu_ref[row, BLOCK_STATE:2 * BLOCK_STATE]
                hr, hi = lr * hr - li * hi + br, lr * hi + li * hr + bi
                bu_ref[row, 0:BLOCK_STATE] = hr
                bu_ref[row, BLOCK_STATE:2 * BLOCK_STATE] = hi
            hre_ref[rs, ss] = hr
            him_ref[rs, ss] = hi
        dsk = dskip_ref[:, cs]
        y_pair = (jnp.dot(bu_ref[0:half, :].astype(BF16), sout_ref[k], preferred_element_type=F32)
                  + jnp.dot(u_bf, sdir_ref[k], preferred_element_type=F32)
                  + u_pair * jnp.concatenate([dsk, dsk], axis=1))
        for j in range(tl // 2):
            for m in range(nsub):
                r0 = j * nb + SUBLANES * m
                slab_ref[k, _tb_rows(m, 2 * j, pitch), :] = y_pair[r0:r0 + SUBLANES, 0:LANES]
                slab_ref[k, _tb_rows(m, 2 * j + 1, pitch), :] = (
                    y_pair[r0:r0 + SUBLANES, LANES:2 * LANES])

    ys = jnp.concatenate([_load_bt(slab_ref, j, nb, tl, pitch) for j in range(SSM_TILES)], axis=1)
    sg = jax.nn.gelu(ys)
    in_proj(W_SSM, c0)
    glu = jnp.dot(sg.astype(BF16), wglu_ref[...], preferred_element_type=F32) + bglu_ref[...]
    s = sg * jax.nn.sigmoid(glu) * jax.nn.silu(z_ref[:, W_SSM:c0])
    mix_ref[:, 0:W_SSM] = s.astype(BF16)

    cv = jnp.concatenate([_load_bt(slab_ref, SSM_TILES + j, nb, tl, pitch)
                          for j in range(CONV_TILES)], axis=1) + bdw_ref[...]
    act = jax.nn.silu(_layernorm(cv, gcl_ref[...], bcl_ref[...]))
    in_proj(c0 + 2 * W_CONV, IN_COLS)
    cpw = jnp.dot(act.astype(BF16), wpw_ref[...], preferred_element_type=F32) + bpw_ref[...]
    cg = cpw * jax.nn.silu(z_ref[:, c0 + 2 * W_CONV:IN_COLS])
    mix_ref[:, W_SSM:W_SSM + W_CONV] = cg.astype(BF16)

    mix = jnp.dot(mix_ref[...], wout_ref[...], preferred_element_type=F32)
    pre = alpha * x_ref[...].reshape(rows, D_MODEL) + mix
    y_ref[...] = _layernorm(pre, gpost_ref[...], bpost_ref[...]).reshape(y_ref.shape)


def _layer_call(x, state, consts, *, nb, tl, n_steps, alpha, conv_steps, name):
    rows = nb * tl
    hist = (CONV_K - 1) * nb
    pitch = _slab_pitch(tl)
    assert nb % SUBLANES == 0 and tl % conv_steps == 0 and (nb * pitch) % SUBLANES == 0
    assert tl % 2 == 0 and (n_steps == 1 or (tl >= CONV_K - 1 and tl % SUBLANES == 0))
    out_shape = (
        jax.ShapeDtypeStruct(x.shape, F32),
        jax.ShapeDtypeStruct((nb, N_STATE_ALL), F32),
        jax.ShapeDtypeStruct((nb, N_STATE_ALL), F32),
        jax.ShapeDtypeStruct((hist, W_CONV), F32),
    )
    scratch = [
        pltpu.VMEM((rows, IN_COLS), F32),
        pltpu.VMEM((rows // 2 + nb, 2 * BLOCK_STATE), F32),
        pltpu.VMEM((SSM_TILES + CONV_TILES, nb * pitch, LANES), F32),
        pltpu.VMEM((hist + rows, W_CONV), F32),
        pltpu.VMEM((rows, W_SSM + W_CONV), BF16),
    ]
    operands = (x,) + (() if state is None else tuple(state)) + tuple(consts)
    kernel = functools.partial(_layer_kernel, nb, tl, n_steps, alpha, conv_steps, state is None)
    if n_steps == 1:
        assert x.shape == (rows, D_MODEL)
        return pl.pallas_call(
            kernel, out_shape=out_shape, scratch_shapes=scratch, name=name,
            compiler_params=pltpu.CompilerParams(vmem_limit_bytes=VMEM_LIMIT_BYTES),
        )(*operands)

    assert x.shape == (nb, tl * n_steps, D_MODEL)

    def whole(a):
        zeros = (0,) * a.ndim
        return pl.BlockSpec(a.shape, lambda i: zeros)

    x_spec = pl.BlockSpec((nb, tl, D_MODEL), lambda i: (0, i, 0))
    in_specs = [x_spec] + [whole(a) for a in operands[1:]]
    out_specs = (
        x_spec,
        pl.BlockSpec((nb, N_STATE_ALL), lambda i: (0, 0)),
        pl.BlockSpec((nb, N_STATE_ALL), lambda i: (0, 0)),
        pl.BlockSpec((hist, W_CONV), lambda i: (0, 0)),
    )
    return pl.pallas_call(
        kernel, out_shape=out_shape, grid=(n_steps,), in_specs=in_specs, out_specs=out_specs,
        scratch_shapes=scratch, name=name,
        compiler_params=pltpu.CompilerParams(dimension_semantics=("arbitrary",),
                                             vmem_limit_bytes=VMEM_LIMIT_BYTES),
    )(*operands)


def _prep_call(lam_re, lam_im, log_dt, b_re, b_im, c_re, c_im):
    gp = N_GROUPS * GROUP_P
    tile_b = lambda a: jnp.tile(a, (1, GROUPS_PER_BLOCK))
    ldt_n = jnp.broadcast_to(log_dt[:, None], (N_GROUPS, N_STATE))
    lam = jnp.stack([tile_b(a) for a in (lam_re, lam_im, ldt_n)])
    lamrow = jnp.stack([a.reshape(1, N_STATE_ALL) for a in (lam_re, lam_im, ldt_n)])
    bt = jnp.stack([tile_b(b.transpose(0, 2, 1).reshape(gp, N_STATE)) for b in (b_re, b_im)])
    cg = jnp.stack([tile_b(c.reshape(gp, N_STATE)) for c in (c_re, c_im)])
    out_shape = (
        jax.ShapeDtypeStruct((N_BLOCKS, 2 * LANES, 2 * BLOCK_STATE), BF16),
        jax.ShapeDtypeStruct((N_BLOCKS, 2 * BLOCK_STATE, 2 * LANES), BF16),
        jax.ShapeDtypeStruct((N_BLOCKS, 2 * LANES, 2 * LANES), BF16),
        jax.ShapeDtypeStruct((SUBLANES, N_STATE_ALL), F32),
        jax.ShapeDtypeStruct((SUBLANES, N_STATE_ALL), F32),
    )
    return pl.pallas_call(_prep_kernel, out_shape=out_shape, name="s5_prep")(lam, lamrow, bt, cg)


def kernel(x_prompt, x_sample, state_ssm_re, state_ssm_im, state_conv, w_in, b_in, lam_re, lam_im, log_dt, b_re, b_im, c_re, c_im, d_skip, w_glu, b_glu, w_dw, b_dw, g_conv_ln, b_conv_ln, w_pw2, b_pw2, w_out, g_post, b_post):
    depth = w_in.shape[0]
    alpha = (2.0 * depth) ** 0.25
    bsz, seq, _ = x_prompt.shape
    dbsz, dseq, _ = x_sample.shape
    tl_prompt = 64

    hp = x_prompt
    hs = x_sample.reshape(dbsz * dseq, D_MODEL)
    outs = {k: [] for k in ("re_p", "im_p", "cv_p", "re_s", "im_s", "cv_s")}
    row = lambda a: a.reshape(1, -1)
    for layer in range(depth):
        s_in, s_out, s_dir, l2re, l2im = _prep_call(
            lam_re[layer], lam_im[layer], log_dt[layer], b_re[layer], b_im[layer],
            c_re[layer], c_im[layer])
        wdw8 = jnp.broadcast_to(w_dw[layer][:, None, :], (CONV_K, SUBLANES, W_CONV))
        consts = (w_in[layer].astype(BF16), row(b_in[layer]), l2re, l2im, s_in, s_out, s_dir,
                  row(d_skip[layer]), w_glu[layer].astype(BF16), row(b_glu[layer]), wdw8,
                  row(b_dw[layer]), row(g_conv_ln[layer]), row(b_conv_ln[layer]),
                  w_pw2[layer].astype(BF16), row(b_pw2[layer]), w_out[layer].astype(BF16),
                  row(g_post[layer]), row(b_post[layer]))

        hp, re_p, im_p, cv_p = _layer_call(
            hp, None, consts, nb=bsz, tl=tl_prompt,
            n_steps=seq // tl_prompt, alpha=alpha, conv_steps=8, name="layer_prompt")

        h0re = state_ssm_re[layer].reshape(dbsz, N_STATE_ALL)
        h0im = state_ssm_im[layer].reshape(dbsz, N_STATE_ALL)
        cbuf0 = state_conv[layer].transpose(1, 0, 2).reshape((CONV_K - 1) * dbsz, W_CONV)
        hs, re_s, im_s, cv_s = _layer_call(
            hs, (h0re, h0im, cbuf0), consts, nb=dbsz, tl=dseq, n_steps=1, alpha=alpha,
            conv_steps=dseq, name="layer_sample")

        unrow = lambda a, n: a.reshape(CONV_K - 1, n, W_CONV).transpose(1, 0, 2)
        outs["re_p"].append(re_p.reshape(bsz, N_GROUPS, N_STATE))
        outs["im_p"].append(im_p.reshape(bsz, N_GROUPS, N_STATE))
        outs["cv_p"].append(unrow(cv_p, bsz))
        outs["re_s"].append(re_s.reshape(dbsz, N_GROUPS, N_STATE))
        outs["im_s"].append(im_s.reshape(dbsz, N_GROUPS, N_STATE))
        outs["cv_s"].append(unrow(cv_s, dbsz))

    y_sample = hs.reshape(dbsz, dseq, D_MODEL)
    return (hp, y_sample,
            jnp.stack(outs["re_p"]), jnp.stack(outs["im_p"]), jnp.stack(outs["cv_p"]),
            jnp.stack(outs["re_s"]), jnp.stack(outs["im_s"]), jnp.stack(outs["cv_s"]))
```

```python
import functools

import jax
import jax.numpy as jnp
from jax import lax
from jax.experimental import pallas as pl
from jax.experimental.pallas import tpu as pltpu

D_MODEL = 1024
W_SSM = 512
W_CONV = 512
GROUP_P = 16
N_GROUPS = 32
N_STATE = 64
CONV_K = 31
IN_COLS = 2 * W_SSM + 3 * W_CONV
LN_EPS = 1e-5

LANES = 128
SUBLANES = 8
GROUPS_PER_BLOCK = LANES // GROUP_P
N_BLOCKS = N_GROUPS // GROUPS_PER_BLOCK
BLOCK_STATE = GROUPS_PER_BLOCK * N_STATE
N_STATE_ALL = N_GROUPS * N_STATE
SSM_TILES = W_SSM // LANES
CONV_TILES = W_CONV // LANES
OUT_CHUNKS = 2
VMEM_LIMIT_BYTES = 58 * 1024 * 1024

F32 = jnp.float32
BF16 = jnp.bfloat16


def _layernorm(x, g, b):
    mu = jnp.mean(x, axis=-1, keepdims=True)
    xc = x - mu
    var = jnp.mean(xc * xc, axis=-1, keepdims=True)
    return xc * lax.rsqrt(var + LN_EPS) * g + b


def _slab_pitch(tl):
    if tl % SUBLANES:
        return tl
    pitch = tl
    while (pitch // SUBLANES) % 2 == 0:
        pitch += SUBLANES
    return pitch


def _tb_rows(m, t, pitch):
    return pl.ds(SUBLANES * m * pitch + t, SUBLANES, stride=pitch)


def _store_bt(slab_ref, j, val, nb, tl, pitch):
    if pitch == tl:
        slab_ref[j] = val
    else:
        for b in range(nb):
            slab_ref[j, b * pitch:b * pitch + tl, :] = val[b * tl:(b + 1) * tl]


def _load_bt(slab_ref, j, nb, tl, pitch):
    if pitch == tl:
        return slab_ref[j]
    return jnp.concatenate([slab_ref[j, b * pitch:b * pitch + tl, :] for b in range(nb)], axis=0)


def _gather_tb(slab_ref, j, nb, tl, pitch):
    pieces = [slab_ref[j, _tb_rows(m, t, pitch), :]
              for t in range(tl) for m in range(nb // SUBLANES)]
    return jnp.concatenate(pieces, axis=0)


def _prep_kernel(lam_ref, lamrow_ref, bt_ref, cg_ref,
                 sin_ref, sout_ref, sdir_ref, l2re_ref, l2im_ref):
    gp = N_GROUPS * GROUP_P
    abt = (((1,), (1,)), ((), ()))

    def zoh(lr, li, ldt):
        dt = jnp.exp(ldt)
        mag = jnp.exp(lr * dt)
        ang = li * dt
        return mag * jnp.cos(ang), mag * jnp.sin(ang)

    def per_channel(a):
        wide = jnp.broadcast_to(a[:, None, :], (N_GROUPS, GROUP_P, BLOCK_STATE))
        return wide.reshape(gp, BLOCK_STATE)

    lr, li = lam_ref[0], lam_ref[1]
    ar, ai = zoh(lr, li, lam_ref[2])
    nr, ni = ar - 1.0, ai
    den = lr * lr + li * li
    cr = per_channel((nr * lr + ni * li) / den)
    ci = per_channel((ni * lr - nr * li) / den)
    ar, ai = per_channel(ar), per_channel(ai)
    shape = ar.shape
    row_g = lax.broadcasted_iota(jnp.int32, shape, 0) // GROUP_P % GROUPS_PER_BLOCK
    col_g = lax.broadcasted_iota(jnp.int32, shape, 1) // N_STATE
    keep = row_g == col_g
    br, bi = bt_ref[0], bt_ref[1]
    bbr = jnp.where(keep, cr * br - ci * bi, 0.0)
    bbi = jnp.where(keep, cr * bi + ci * br, 0.0)
    lbr = ar * bbr - ai * bbi
    lbi = ar * bbi + ai * bbr
    c0r = jnp.where(keep, cg_ref[0], 0.0)
    c0i = jnp.where(keep, cg_ref[1], 0.0)
    c1r = ar * c0r - ai * c0i
    c1i = ar * c0i + ai * c0r
    c2r = ar * c1r - ai * c1i
    c2i = ar * c1i + ai * c1r

    for k in range(N_BLOCKS):
        rs = slice(LANES * k, LANES * (k + 1))
        b_k = jnp.concatenate([bbr[rs], bbi[rs]], axis=1)
        lb_k = jnp.concatenate([lbr[rs], lbi[rs]], axis=1)
        sin_ref[k, 0:LANES, :] = lb_k.astype(BF16)
        sin_ref[k, LANES:2 * LANES, :] = b_k.astype(BF16)
        c_k = jnp.concatenate([c0r[rs], -c0i[rs]], axis=1)
        c12_k = jnp.concatenate([jnp.concatenate([c1r[rs], -c1i[rs]], axis=1),
                                 jnp.concatenate([c2r[rs], -c2i[rs]], axis=1)], axis=0)
        sout_ref[k] = c12_k.T.astype(BF16)
        c_bf = c_k.astype(BF16)
        cb = lax.dot_general(b_k.astype(BF16), c_bf, abt, preferred_element_type=F32)
        clb = lax.dot_general(lb_k.astype(BF16), c_bf, abt, preferred_element_type=F32)
        sdir_ref[k, 0:LANES, 0:LANES] = cb.astype(BF16)
        sdir_ref[k, 0:LANES, LANES:2 * LANES] = clb.astype(BF16)
        sdir_ref[k, LANES:2 * LANES, 0:LANES] = jnp.zeros((LANES, LANES), BF16)
        sdir_ref[k, LANES:2 * LANES, LANES:2 * LANES] = cb.astype(BF16)

    a_r, a_i = zoh(lamrow_ref[0], lamrow_ref[1], lamrow_ref[2])
    l2re_ref[...] = jnp.broadcast_to(a_r * a_r - a_i * a_i, l2re_ref.shape)
    l2im_ref[...] = jnp.broadcast_to(2.0 * a_r * a_i, l2im_ref.shape)


def _layer_kernel(nb, tl, n_steps, alpha, conv_steps, zero_state, x_ref, *refs):
    if zero_state:
        refs = (None, None, None) + refs
    _layer_body(nb, tl, n_steps, alpha, conv_steps, x_ref, *refs)


def _layer_body(nb, tl, n_steps, alpha, conv_steps,
                x_ref, h0re_ref, h0im_ref, cbuf0_ref, win_ref, bin_ref, l2re_ref, l2im_ref,
                sin_ref, sout_ref, sdir_ref, dskip_ref, wglu_ref, bglu_ref, wdw_ref, bdw_ref,
                gcl_ref, bcl_ref, wpw_ref, bpw_ref, wout_ref, gpost_ref, bpost_ref,
                y_ref, hre_ref, him_ref, cbuf_ref,
                z_ref, bu_ref, slab_ref, full_ref, mix_ref):
    rows = nb * tl
    hist = (CONV_K - 1) * nb
    pitch = _slab_pitch(tl)
    nsub = nb // SUBLANES
    c0 = 2 * W_SSM

    def init_state():
        if h0re_ref is None:
            hre_ref[...] = jnp.zeros_like(hre_ref)
            him_ref[...] = jnp.zeros_like(him_ref)
            full_ref[0:hist, :] = jnp.zeros((hist, W_CONV), F32)
        else:
            hre_ref[...] = h0re_ref[...]
            him_ref[...] = h0im_ref[...]
            full_ref[0:hist, :] = cbuf0_ref[...]

    if n_steps > 1:
        pl.when(pl.program_id(0) == 0)(init_state)
    else:
        init_state()

    xb = x_ref[...].reshape(rows, D_MODEL).astype(BF16)

    def in_proj(lo, hi):
        z_ref[:, lo:hi] = (jnp.dot(xb, win_ref[:, lo:hi], preferred_element_type=F32)
                           + bin_ref[:, lo:hi])

    cw = 2 * LANES
    for h in range(CONV_TILES // 2):
        a_lo, b_lo = c0 + h * cw, c0 + W_CONV + h * cw
        in_proj(a_lo, a_lo + cw)
        in_proj(b_lo, b_lo + cw)
        v = z_ref[:, a_lo:a_lo + cw] * jax.nn.sigmoid(z_ref[:, b_lo:b_lo + cw])
        for j in range(2):
            _store_bt(slab_ref, SSM_TILES + 2 * h + j, v[:, LANES * j:LANES * (j + 1)],
                      nb, tl, pitch)
        if h == 0:
            in_proj(0, W_SSM)

    for c in range(CONV_TILES):
        lanes = slice(LANES * c, LANES * (c + 1))
        full_ref[hist:hist + rows, lanes] = _gather_tb(slab_ref, SSM_TILES + c, nb, tl, pitch)
    for c in range(CONV_TILES):
        lanes = slice(LANES * c, LANES * (c + 1))
        w = [wdw_ref[tap, :, lanes] for tap in range(CONV_K)]
        for m in range(nsub):
            for t0 in range(0, tl, conv_steps):
                base = t0 * nb + SUBLANES * m
                accs = [None] * conv_steps
                for j in range(conv_steps + CONV_K - 1):
                    xj = full_ref[base + j * nb:base + j * nb + SUBLANES, lanes]
                    for r in range(conv_steps):
                        tap = j - r
                        if 0 <= tap < CONV_K:
                            term = w[tap] * xj
                            accs[r] = term if accs[r] is None else accs[r] + term
                for r in range(conv_steps):
                    slab_ref[SSM_TILES + c, _tb_rows(m, t0 + r, pitch), :] = accs[r]
    cbuf_ref[...] = full_ref[rows:rows + hist, :]
    if n_steps > 1:
        full_ref[0:hist, :] = full_ref[rows:rows + hist, :]
    for j in range(SSM_TILES):
        _store_bt(slab_ref, j, z_ref[:, LANES * j:LANES * (j + 1)], nb, tl, pitch)

    half = rows // 2
    for k in range(N_BLOCKS):
        cs = slice(LANES * k, LANES * (k + 1))
        ss = slice(BLOCK_STATE * k, BLOCK_STATE * (k + 1))
        pieces = {(t, m): slab_ref[k, _tb_rows(m, t, pitch), :]
                  for t in range(tl) for m in range(nsub)}
        u_pair = jnp.concatenate(
            [jnp.concatenate([pieces[2 * j, m], pieces[2 * j + 1, m]], axis=1)
             for j in range(tl // 2) for m in range(nsub)], axis=0)
        u_bf = u_pair.astype(BF16)
        bu_ref[0:nb, 0:BLOCK_STATE] = hre_ref[:, ss]
        bu_ref[0:nb, BLOCK_STATE:2 * BLOCK_STATE] = him_ref[:, ss]
        bu_ref[nb:nb + half, :] = jnp.dot(u_bf, sin_ref[k], preferred_element_type=F32)
        lr = l2re_ref[:, ss]
        li = l2im_ref[:, ss]
        for m in range(nsub):
            rs = slice(SUBLANES * m, SUBLANES * (m + 1))
            hr, hi = bu_ref[rs, 0:BLOCK_STATE], bu_ref[rs, BLOCK_STATE:2 * BLOCK_STATE]
            for j in range(tl // 2):
                row = slice((j + 1) * nb + SUBLANES * m, (j + 1) * nb + SUBLANES * (m + 1))
                br = bu_ref[row, 0:BLOCK_STATE]
                bi = bu_ref[row, BLOCK_STATE:2 * BLOCK_STATE]
                hr, hi = lr * hr - li * hi + br, lr * hi + li * hr + bi
                bu_ref[row, 0:BLOCK_STATE] = hr
                bu_ref[row, BLOCK_STATE:2 * BLOCK_STATE] = hi
            hre_ref[rs, ss] = hr
            him_ref[rs, ss] = hi
        dsk = dskip_ref[:, cs]
        y_pair = (jnp.dot(bu_ref[0:half, :].astype(BF16), sout_ref[k], preferred_element_type=F32)
                  + jnp.dot(u_bf, sdir_ref[k], preferred_element_type=F32)
                  + u_pair * jnp.concatenate([dsk, dsk], axis=1))
        for j in range(tl // 2):
            for m in range(nsub):
                r0 = j * nb + SUBLANES * m
                slab_ref[k, _tb_rows(m, 2 * j, pitch), :] = y_pair[r0:r0 + SUBLANES, 0:LANES]
                slab_ref[k, _tb_rows(m, 2 * j + 1, pitch), :] = (
                    y_pair[r0:r0 + SUBLANES, LANES:2 * LANES])

    ys = jnp.concatenate([_load_bt(slab_ref, j, nb, tl, pitch) for j in range(SSM_TILES)], axis=1)
    sg = jax.nn.gelu(ys)
    in_proj(W_SSM, c0)
    glu = jnp.dot(sg.astype(BF16), wglu_ref[...], preferred_element_type=F32) + bglu_ref[...]
    s = sg * jax.nn.sigmoid(glu) * jax.nn.silu(z_ref[:, W_SSM:c0])
    mix_ref[:, 0:W_SSM] = s.astype(BF16)

    cv = jnp.concatenate([_load_bt(slab_ref, SSM_TILES + j, nb, tl, pitch)
                          for j in range(CONV_TILES)], axis=1) + bdw_ref[...]
    act = jax.nn.silu(_layernorm(cv, gcl_ref[...], bcl_ref[...]))
    in_proj(c0 + 2 * W_CONV, IN_COLS)
    cpw = jnp.dot(act.astype(BF16), wpw_ref[...], preferred_element_type=F32) + bpw_ref[...]
    cg = cpw * jax.nn.silu(z_ref[:, c0 + 2 * W_CONV:IN_COLS])
    mix_ref[:, W_SSM:W_SSM + W_CONV] = cg.astype(BF16)

    x2 = x_ref[...].reshape(rows, D_MODEL)
    rc = rows // OUT_CHUNKS
    for r in range(OUT_CHUNKS):
        rsl = slice(r * rc, (r + 1) * rc)
        mix = jnp.dot(mix_ref[rsl, :], wout_ref[...], preferred_element_type=F32)
        out = _layernorm(alpha * x2[rsl] + mix, gpost_ref[...], bpost_ref[...])
        if len(y_ref.shape) == 3:
            bsl = slice(r * (nb // OUT_CHUNKS), (r + 1) * (nb // OUT_CHUNKS))
            y_ref[bsl] = out.reshape(nb // OUT_CHUNKS, tl, D_MODEL)
        else:
            y_ref[rsl, :] = out


def _layer_call(x, state, consts, *, nb, tl, n_steps, alpha, conv_steps, name):
    rows = nb * tl
    hist = (CONV_K - 1) * nb
    pitch = _slab_pitch(tl)
    assert nb % SUBLANES == 0 and tl % conv_steps == 0 and (nb * pitch) % SUBLANES == 0
    assert tl % 2 == 0 and (n_steps == 1 or (tl >= CONV_K - 1 and tl % SUBLANES == 0))
    out_shape = (
        jax.ShapeDtypeStruct(x.shape, F32),
        jax.ShapeDtypeStruct((nb, N_STATE_ALL), F32),
        jax.ShapeDtypeStruct((nb, N_STATE_ALL), F32),
        jax.ShapeDtypeStruct((hist, W_CONV), F32),
    )
    scratch = [
        pltpu.VMEM((rows, IN_COLS), F32),
        pltpu.VMEM((rows // 2 + nb, 2 * BLOCK_STATE), F32),
        pltpu.VMEM((SSM_TILES + CONV_TILES, nb * pitch, LANES), F32),
        pltpu.VMEM((hist + rows, W_CONV), F32),
        pltpu.VMEM((rows, W_SSM + W_CONV), BF16),
    ]
    operands = (x,) + (() if state is None else tuple(state)) + tuple(consts)
    kernel = functools.partial(_layer_kernel, nb, tl, n_steps, alpha, conv_steps, state is None)
    if n_steps == 1:
        assert x.shape == (rows, D_MODEL)
        return pl.pallas_call(
            kernel, out_shape=out_shape, scratch_shapes=scratch, name=name,
            compiler_params=pltpu.CompilerParams(vmem_limit_bytes=VMEM_LIMIT_BYTES),
        )(*operands)

    assert x.shape == (nb, tl * n_steps, D_MODEL)

    def whole(a):
        zeros = (0,) * a.ndim
        return pl.BlockSpec(a.shape, lambda i: zeros)

    x_spec = pl.BlockSpec((nb, tl, D_MODEL), lambda i: (0, i, 0))
    in_specs = [x_spec] + [whole(a) for a in operands[1:]]
    out_specs = (
        x_spec,
        pl.BlockSpec((nb, N_STATE_ALL), lambda i: (0, 0)),
        pl.BlockSpec((nb, N_STATE_ALL), lambda i: (0, 0)),
        pl.BlockSpec((hist, W_CONV), lambda i: (0, 0)),
    )
    return pl.pallas_call(
        kernel, out_shape=out_shape, grid=(n_steps,), in_specs=in_specs, out_specs=out_specs,
        scratch_shapes=scratch, name=name,
        compiler_params=pltpu.CompilerParams(dimension_semantics=("arbitrary",),
                                             vmem_limit_bytes=VMEM_LIMIT_BYTES),
    )(*operands)


def _prep_call(lam_re, lam_im, log_dt, b_re, b_im, c_re, c_im):
    gp = N_GROUPS * GROUP_P
    tile_b = lambda a: jnp.tile(a, (1, GROUPS_PER_BLOCK))
    ldt_n = jnp.broadcast_to(log_dt[:, None], (N_GROUPS, N_STATE))
    lam = jnp.stack([tile_b(a) for a in (lam_re, lam_im, ldt_n)])
    lamrow = jnp.stack([a.reshape(1, N_STATE_ALL) for a in (lam_re, lam_im, ldt_n)])
    bt = jnp.stack([tile_b(b.transpose(0, 2, 1).reshape(gp, N_STATE)) for b in (b_re, b_im)])
    cg = jnp.stack([tile_b(c.reshape(gp, N_STATE)) for c in (c_re, c_im)])
    out_shape = (
        jax.ShapeDtypeStruct((N_BLOCKS, 2 * LANES, 2 * BLOCK_STATE), BF16),
        jax.ShapeDtypeStruct((N_BLOCKS, 2 * BLOCK_STATE, 2 * LANES), BF16),
        jax.ShapeDtypeStruct((N_BLOCKS, 2 * LANES, 2 * LANES), BF16),
        jax.ShapeDtypeStruct((SUBLANES, N_STATE_ALL), F32),
        jax.ShapeDtypeStruct((SUBLANES, N_STATE_ALL), F32),
    )
    return pl.pallas_call(_prep_kernel, out_shape=out_shape, name="s5_prep")(lam, lamrow, bt, cg)


def kernel(x_prompt, x_sample, state_ssm_re, state_ssm_im, state_conv, w_in, b_in, lam_re, lam_im, log_dt, b_re, b_im, c_re, c_im, d_skip, w_glu, b_glu, w_dw, b_dw, g_conv_ln, b_conv_ln, w_pw2, b_pw2, w_out, g_post, b_post):
    depth = w_in.shape[0]
    alpha = (2.0 * depth) ** 0.25
    bsz, seq, _ = x_prompt.shape
    dbsz, dseq, _ = x_sample.shape
    tl_prompt = 64

    hp = x_prompt
    hs = x_sample.reshape(dbsz * dseq, D_MODEL)
    outs = {k: [] for k in ("re_p", "im_p", "cv_p", "re_s", "im_s", "cv_s")}
    row = lambda a: a.reshape(1, -1)
    for layer in range(depth):
        s_in, s_out, s_dir, l2re, l2im = _prep_call(
            lam_re[layer], lam_im[layer], log_dt[layer], b_re[layer], b_im[layer],
            c_re[layer], c_im[layer])
        wdw8 = jnp.broadcast_to(w_dw[layer][:, None, :], (CONV_K, SUBLANES, W_CONV))
        consts = (w_in[layer].astype(BF16), row(b_in[layer]), l2re, l2im, s_in, s_out, s_dir,
                  row(d_skip[layer]), w_glu[layer].astype(BF16), row(b_glu[layer]), wdw8,
                  row(b_dw[layer]), row(g_conv_ln[layer]), row(b_conv_ln[layer]),
                  w_pw2[layer].astype(BF16), row(b_pw2[layer]), w_out[layer].astype(BF16),
                  row(g_post[layer]), row(b_post[layer]))

        hp, re_p, im_p, cv_p = _layer_call(
            hp, None, consts, nb=bsz, tl=tl_prompt,
            n_steps=seq // tl_prompt, alpha=alpha, conv_steps=8, name="layer_prompt")

        h0re = state_ssm_re[layer].reshape(dbsz, N_STATE_ALL)
        h0im = state_ssm_im[layer].reshape(dbsz, N_STATE_ALL)
        cbuf0 = state_conv[layer].transpose(1, 0, 2).reshape((CONV_K - 1) * dbsz, W_CONV)
        hs, re_s, im_s, cv_s = _layer_call(
            hs, (h0re, h0im, cbuf0), consts, nb=dbsz, tl=dseq, n_steps=1, alpha=alpha,
            conv_steps=dseq, name="layer_sample")

        unrow = lambda a, n: a.reshape(CONV_K - 1, n, W_CONV).transpose(1, 0, 2)
        outs["re_p"].append(re_p.reshape(bsz, N_GROUPS, N_STATE))
        outs["im_p"].append(im_p.reshape(bsz, N_GROUPS, N_STATE))
        outs["cv_p"].append(unrow(cv_p, bsz))
        outs["re_s"].append(re_s.reshape(dbsz, N_GROUPS, N_STATE))
        outs["im_s"].append(im_s.reshape(dbsz, N_GROUPS, N_STATE))
        outs["cv_s"].append(unrow(cv_s, dbsz))

    y_sample = hs.reshape(dbsz, dseq, D_MODEL)
    return (hp, y_sample,
            jnp.stack(outs["re_p"]), jnp.stack(outs["im_p"]), jnp.stack(outs["cv_p"]),
            jnp.stack(outs["re_s"]), jnp.stack(outs["im_s"]), jnp.stack(outs["cv_s"]))
```

```python
import functools

import jax
import jax.numpy as jnp
from jax import lax
from jax.experimental import pallas as pl
from jax.experimental.pallas import tpu as pltpu

D_MODEL = 1024
W_SSM = 512
W_CONV = 512
GROUP_P = 16
N_GROUPS = 32
N_STATE = 64
CONV_K = 31
IN_COLS = 2 * W_SSM + 3 * W_CONV
LN_EPS = 1e-5

LANES = 128
SUBLANES = 8
GROUPS_PER_BLOCK = LANES // GROUP_P
N_BLOCKS = N_GROUPS // GROUPS_PER_BLOCK
BLOCK_STATE = GROUPS_PER_BLOCK * N_STATE
N_STATE_ALL = N_GROUPS * N_STATE
SSM_TILES = W_SSM // LANES
CONV_TILES = W_CONV // LANES
OUT_CHUNKS = 2
VMEM_LIMIT_BYTES = 58 * 1024 * 1024

F32 = jnp.float32
BF16 = jnp.bfloat16


def _layernorm(x, g, b):
    mu = jnp.mean(x, axis=-1, keepdims=True)
    xc = x - mu
    var = jnp.mean(xc * xc, axis=-1, keepdims=True)
    return xc * lax.rsqrt(var + LN_EPS) * g + b


def _slab_pitch(tl):
    if tl % SUBLANES:
        return tl
    pitch = tl
    while (pitch // SUBLANES) % 2 == 0:
        pitch += SUBLANES
    return pitch


def _tb_rows(m, t, pitch):
    return pl.ds(SUBLANES * m * pitch + t, SUBLANES, stride=pitch)


def _store_bt(slab_ref, j, val, nb, tl, pitch):
    if pitch == tl:
        slab_ref[j] = val
    else:
        for b in range(nb):
            slab_ref[j, b * pitch:b * pitch + tl, :] = val[b * tl:(b + 1) * tl]


def _load_bt(slab_ref, j, nb, tl, pitch):
    if pitch == tl:
        return slab_ref[j]
    return jnp.concatenate([slab_ref[j, b * pitch:b * pitch + tl, :] for b in range(nb)], axis=0)


def _gather_tb(slab_ref, j, nb, tl, pitch):
    pieces = [slab_ref[j, _tb_rows(m, t, pitch), :]
              for t in range(tl) for m in range(nb // SUBLANES)]
    return jnp.concatenate(pieces, axis=0)


def _prep_kernel(lam_ref, lamrow_ref, bt_ref, cg_ref,
                 sin_ref, sout_ref, sdir_ref, l2re_ref, l2im_ref):
    gp = N_GROUPS * GROUP_P
    abt = (((1,), (1,)), ((), ()))

    def zoh(lr, li, ldt):
        dt = jnp.exp(ldt)
        mag = jnp.exp(lr * dt)
        ang = li * dt
        return mag * jnp.cos(ang), mag * jnp.sin(ang)

    def per_channel(a):
        wide = jnp.broadcast_to(a[:, None, :], (N_GROUPS, GROUP_P, BLOCK_STATE))
        return wide.reshape(gp, BLOCK_STATE)

    lr, li = lam_ref[0], lam_ref[1]
    ar, ai = zoh(lr, li, lam_ref[2])
    nr, ni = ar - 1.0, ai
    den = lr * lr + li * li
    cr = per_channel((nr * lr + ni * li) / den)
    ci = per_channel((ni * lr - nr * li) / den)
    ar, ai = per_channel(ar), per_channel(ai)
    shape = ar.shape
    row_g = lax.broadcasted_iota(jnp.int32, shape, 0) // GROUP_P % GROUPS_PER_BLOCK
    col_g = lax.broadcasted_iota(jnp.int32, shape, 1) // N_STATE
    keep = row_g == col_g
    br, bi = bt_ref[0], bt_ref[1]
    bbr = jnp.where(keep, cr * br - ci * bi, 0.0)
    bbi = jnp.where(keep, cr * bi + ci * br, 0.0)
    lbr = ar * bbr - ai * bbi
    lbi = ar * bbi + ai * bbr
    c0r = jnp.where(keep, cg_ref[0], 0.0)
    c0i = jnp.where(keep, cg_ref[1], 0.0)
    c1r = ar * c0r - ai * c0i
    c1i = ar * c0i + ai * c0r
    c2r = ar * c1r - ai * c1i
    c2i = ar * c1i + ai * c1r

    for k in range(N_BLOCKS):
        rs = slice(LANES * k, LANES * (k + 1))
        b_k = jnp.concatenate([bbr[rs], bbi[rs]], axis=1)
        lb_k = jnp.concatenate([lbr[rs], lbi[rs]], axis=1)
        sin_ref[k, 0:LANES, :] = lb_k.astype(BF16)
        sin_ref[k, LANES:2 * LANES, :] = b_k.astype(BF16)
        c_k = jnp.concatenate([c0r[rs], -c0i[rs]], axis=1)
        c12_k = jnp.concatenate([jnp.concatenate([c1r[rs], -c1i[rs]], axis=1),
                                 jnp.concatenate([c2r[rs], -c2i[rs]], axis=1)], axis=0)
        sout_ref[k] = c12_k.T.astype(BF16)
        c_bf = c_k.astype(BF16)
        cb = lax.dot_general(b_k.astype(BF16), c_bf, abt, preferred_element_type=F32)
        clb = lax.dot_general(lb_k.astype(BF16), c_bf, abt, preferred_element_type=F32)
        sdir_ref[k, 0:LANES, 0:LANES] = cb.astype(BF16)
        sdir_ref[k, 0:LANES, LANES:2 * LANES] = clb.astype(BF16)
        sdir_ref[k, LANES:2 * LANES, 0:LANES] = jnp.zeros((LANES, LANES), BF16)
        sdir_ref[k, LANES:2 * LANES, LANES:2 * LANES] = cb.astype(BF16)

    a_r, a_i = zoh(lamrow_ref[0], lamrow_ref[1], lamrow_ref[2])
    l2re_ref[...] = jnp.broadcast_to(a_r * a_r - a_i * a_i, l2re_ref.shape)
    l2im_ref[...] = jnp.broadcast_to(2.0 * a_r * a_i, l2im_ref.shape)


def _layer_kernel(nb, tl, n_steps, alpha, conv_steps, zero_state, x_ref, *refs):
    if zero_state:
        refs = (None, None, None) + refs
    _layer_body(nb, tl, n_steps, alpha, conv_steps, x_ref, *refs)


def _layer_body(nb, tl, n_steps, alpha, conv_steps,
                x_ref, h0re_ref, h0im_ref, cbuf0_ref, win_ref, bin_ref, l2re_ref, l2im_ref,
                sin_ref, sout_ref, sdir_ref, dskip_ref, wglu_ref, bglu_ref, wdw_ref, bdw_ref,
                gcl_ref, bcl_ref, wpw_ref, bpw_ref, wout_ref, gpost_ref, bpost_ref,
                y_ref, hre_ref, him_ref, cbuf_ref,
                z_ref, bu_ref, slab_ref, full_ref, mix_ref):
    rows = nb * tl
    hist = (CONV_K - 1) * nb
    pitch = _slab_pitch(tl)
    nsub = nb // SUBLANES
    c0 = 2 * W_SSM

    def init_state():
        if h0re_ref is None:
            hre_ref[...] = jnp.zeros_like(hre_ref)
            him_ref[...] = jnp.zeros_like(him_ref)
            full_ref[0:hist, :] = jnp.zeros((hist, W_CONV), F32)
        else:
            hre_ref[...] = h0re_ref[...]
            him_ref[...] = h0im_ref[...]
            full_ref[0:hist, :] = cbuf0_ref[...]

    if n_steps > 1:
        pl.when(pl.program_id(0) == 0)(init_state)
    else:
        init_state()

    half = rows // 2

    def s5_block(k):
        cs = slice(LANES * k, LANES * (k + 1))
        ss = slice(BLOCK_STATE * k, BLOCK_STATE * (k + 1))
        pieces = {(t, m): slab_ref[k, _tb_rows(m, t, pitch), :]
                  for t in range(tl) for m in range(nsub)}
        u_pair = jnp.concatenate(
            [jnp.concatenate([pieces[2 * j, m], pieces[2 * j + 1, m]], axis=1)
             for j in range(tl // 2) for m in range(nsub)], axis=0)
        u_bf = u_pair.astype(BF16)
        bu_ref[0:nb, 0:BLOCK_STATE] = hre_ref[:, ss]
        bu_ref[0:nb, BLOCK_STATE:2 * BLOCK_STATE] = him_ref[:, ss]
        bu_ref[nb:nb + half, :] = jnp.dot(u_bf, sin_ref[k], preferred_element_type=F32)
        lr = l2re_ref[:, ss]
        li = l2im_ref[:, ss]
        for m in range(nsub):
            rs = slice(SUBLANES * m, SUBLANES * (m + 1))
            hr, hi = bu_ref[rs, 0:BLOCK_STATE], bu_ref[rs, BLOCK_STATE:2 * BLOCK_STATE]
            for j in range(tl // 2):
                row = slice((j + 1) * nb + SUBLANES * m, (j + 1) * nb + SUBLANES * (m + 1))
                br = bu_ref[row, 0:BLOCK_STATE]
                bi = bu_ref[row, BLOCK_STATE:2 * BLOCK_STATE]
                hr, hi = lr * hr - li * hi + br, lr * hi + li * hr + bi
                bu_ref[row, 0:BLOCK_STATE] = hr
                bu_ref[row, BLOCK_STATE:2 * BLOCK_STATE] = hi
            hre_ref[rs, ss] = hr
            him_ref[rs, ss] = hi
        dsk = dskip_ref[:, cs]
        y_pair = (jnp.dot(bu_ref[0:half, :].astype(BF16), sout_ref[k], preferred_element_type=F32)
                  + jnp.dot(u_bf, sdir_ref[k], preferred_element_type=F32)
                  + u_pair * jnp.concatenate([dsk, dsk], axis=1))
        for j in range(tl // 2):
            for m in range(nsub):
                r0 = j * nb + SUBLANES * m
                slab_ref[k, _tb_rows(m, 2 * j, pitch), :] = y_pair[r0:r0 + SUBLANES, 0:LANES]
                slab_ref[k, _tb_rows(m, 2 * j + 1, pitch), :] = (
                    y_pair[r0:r0 + SUBLANES, LANES:2 * LANES])

    xb = x_ref[...].reshape(rows, D_MODEL).astype(BF16)

    def in_proj(lo, hi):
        z_ref[:, lo:hi] = (jnp.dot(xb, win_ref[:, lo:hi], preferred_element_type=F32)
                           + bin_ref[:, lo:hi])

    def s5_branch():
        ys = jnp.concatenate([_load_bt(slab_ref, j, nb, tl, pitch) for j in range(SSM_TILES)],
                             axis=1)
        sg = jax.nn.gelu(ys)
        in_proj(W_SSM, c0)
        glu = jnp.dot(sg.astype(BF16), wglu_ref[...], preferred_element_type=F32) + bglu_ref[...]
        s = sg * jax.nn.sigmoid(glu) * jax.nn.silu(z_ref[:, W_SSM:c0])
        mix_ref[:, 0:W_SSM] = s.astype(BF16)

    cw = 2 * LANES
    for h in range(CONV_TILES // 2):
        a_lo, b_lo = c0 + h * cw, c0 + W_CONV + h * cw
        in_proj(a_lo, a_lo + cw)
        in_proj(b_lo, b_lo + cw)
        v = z_ref[:, a_lo:a_lo + cw] * jax.nn.sigmoid(z_ref[:, b_lo:b_lo + cw])
        for j in range(2):
            _store_bt(slab_ref, SSM_TILES + 2 * h + j, v[:, LANES * j:LANES * (j + 1)],
                      nb, tl, pitch)
        if h == 0:
            in_proj(0, W_SSM)
            for j in range(SSM_TILES):
                _store_bt(slab_ref, j, z_ref[:, LANES * j:LANES * (j + 1)], nb, tl, pitch)
            for k in range(N_BLOCKS):
                s5_block(k)
            s5_branch()

    for c in range(CONV_TILES):
        lanes = slice(LANES * c, LANES * (c + 1))
        full_ref[hist:hist + rows, lanes] = _gather_tb(slab_ref, SSM_TILES + c, nb, tl, pitch)
    for c in range(CONV_TILES):
        lanes = slice(LANES * c, LANES * (c + 1))
        w = [wdw_ref[tap, :, lanes] for tap in range(CONV_K)]
        for m in range(nsub):
            for t0 in range(0, tl, conv_steps):
                base = t0 * nb + SUBLANES * m
                accs = [None] * conv_steps
                for j in range(conv_steps + CONV_K - 1):
                    xj = full_ref[base + j * nb:base + j * nb + SUBLANES, lanes]
                    for r in range(conv_steps):
                        tap = j - r
                        if 0 <= tap < CONV_K:
                            term = w[tap] * xj
                            accs[r] = term if accs[r] is None else accs[r] + term
                for r in range(conv_steps):
                    slab_ref[SSM_TILES + c, _tb_rows(m, t0 + r, pitch), :] = accs[r]
    cbuf_ref[...] = full_ref[rows:rows + hist, :]
    if n_steps > 1:
        full_ref[0:hist, :] = full_ref[rows:rows + hist, :]
    cv = jnp.concatenate([_load_bt(slab_ref, SSM_TILES + j, nb, tl, pitch)
                          for j in range(CONV_TILES)], axis=1) + bdw_ref[...]
    act = jax.nn.silu(_layernorm(cv, gcl_ref[...], bcl_ref[...]))
    in_proj(c0 + 2 * W_CONV, IN_COLS)
    cpw = jnp.dot(act.astype(BF16), wpw_ref[...], preferred_element_type=F32) + bpw_ref[...]
    cg = cpw * jax.nn.silu(z_ref[:, c0 + 2 * W_CONV:IN_COLS])
    mix_ref[:, W_SSM:W_SSM + W_CONV] = cg.astype(BF16)

    x2 = x_ref[...].reshape(rows, D_MODEL)
    rc = rows // OUT_CHUNKS
    for r in range(OUT_CHUNKS):
        rsl = slice(r * rc, (r + 1) * rc)
        mix = jnp.dot(mix_ref[rsl, :], wout_ref[...], preferred_element_type=F32)
        out = _layernorm(alpha * x2[rsl] + mix, gpost_ref[...], bpost_ref[...])
        if len(y_ref.shape) == 3:
            bsl = slice(r * (nb // OUT_CHUNKS), (r + 1) * (nb // OUT_CHUNKS))
            y_ref[bsl] = out.reshape(nb // OUT_CHUNKS, tl, D_MODEL)
        else:
            y_ref[rsl, :] = out


def _layer_call(x, state, consts, *, nb, tl, n_steps, alpha, conv_steps, name):
    rows = nb * tl
    hist = (CONV_K - 1) * nb
    pitch = _slab_pitch(tl)
    assert nb % SUBLANES == 0 and tl % conv_steps == 0 and (nb * pitch) % SUBLANES == 0
    assert tl % 2 == 0 and (n_steps == 1 or (tl >= CONV_K - 1 and tl % SUBLANES == 0))
    out_shape = (
        jax.ShapeDtypeStruct(x.shape, F32),
        jax.ShapeDtypeStruct((nb, N_STATE_ALL), F32),
        jax.ShapeDtypeStruct((nb, N_STATE_ALL), F32),
        jax.ShapeDtypeStruct((hist, W_CONV), F32),
    )
    scratch = [
        pltpu.VMEM((rows, IN_COLS), F32),
        pltpu.VMEM((rows // 2 + nb, 2 * BLOCK_STATE), F32),
        pltpu.VMEM((SSM_TILES + CONV_TILES, nb * pitch, LANES), F32),
        pltpu.VMEM((hist + rows, W_CONV), F32),
        pltpu.VMEM((rows, W_SSM + W_CONV), BF16),
    ]
    operands = (x,) + (() if state is None else tuple(state)) + tuple(consts)
    kernel = functools.partial(_layer_kernel, nb, tl, n_steps, alpha, conv_steps, state is None)
    if n_steps == 1:
        assert x.shape == (rows, D_MODEL)
        return pl.pallas_call(
            kernel, out_shape=out_shape, scratch_shapes=scratch, name=name,
            compiler_params=pltpu.CompilerParams(vmem_limit_bytes=VMEM_LIMIT_BYTES),
        )(*operands)

    assert x.shape == (nb, tl * n_steps, D_MODEL)

    def whole(a):
        zeros = (0,) * a.ndim
        return pl.BlockSpec(a.shape, lambda i: zeros)

    x_spec = pl.BlockSpec((nb, tl, D_MODEL), lambda i: (0, i, 0))
    in_specs = [x_spec] + [whole(a) for a in operands[1:]]
    out_specs = (
        x_spec,
        pl.BlockSpec((nb, N_STATE_ALL), lambda i: (0, 0)),
        pl.BlockSpec((nb, N_STATE_ALL), lambda i: (0, 0)),
        pl.BlockSpec((hist, W_CONV), lambda i: (0, 0)),
    )
    return pl.pallas_call(
        kernel, out_shape=out_shape, grid=(n_steps,), in_specs=in_specs, out_specs=out_specs,
        scratch_shapes=scratch, name=name,
        compiler_params=pltpu.CompilerParams(dimension_semantics=("arbitrary",),
                                             vmem_limit_bytes=VMEM_LIMIT_BYTES),
    )(*operands)


def _prep_call(lam_re, lam_im, log_dt, b_re, b_im, c_re, c_im):
    gp = N_GROUPS * GROUP_P
    tile_b = lambda a: jnp.tile(a, (1, GROUPS_PER_BLOCK))
    ldt_n = jnp.broadcast_to(log_dt[:, None], (N_GROUPS, N_STATE))
    lam = jnp.stack([tile_b(a) for a in (lam_re, lam_im, ldt_n)])
    lamrow = jnp.stack([a.reshape(1, N_STATE_ALL) for a in (lam_re, lam_im, ldt_n)])
    bt = jnp.stack([tile_b(b.transpose(0, 2, 1).reshape(gp, N_STATE)) for b in (b_re, b_im)])
    cg = jnp.stack([tile_b(c.reshape(gp, N_STATE)) for c in (c_re, c_im)])
    out_shape = (
        jax.ShapeDtypeStruct((N_BLOCKS, 2 * LANES, 2 * BLOCK_STATE), BF16),
        jax.ShapeDtypeStruct((N_BLOCKS, 2 * BLOCK_STATE, 2 * LANES), BF16),
        jax.ShapeDtypeStruct((N_BLOCKS, 2 * LANES, 2 * LANES), BF16),
        jax.ShapeDtypeStruct((SUBLANES, N_STATE_ALL), F32),
        jax.ShapeDtypeStruct((SUBLANES, N_STATE_ALL), F32),
    )
    return pl.pallas_call(_prep_kernel, out_shape=out_shape, name="s5_prep")(lam, lamrow, bt, cg)


def kernel(x_prompt, x_sample, state_ssm_re, state_ssm_im, state_conv, w_in, b_in, lam_re, lam_im, log_dt, b_re, b_im, c_re, c_im, d_skip, w_glu, b_glu, w_dw, b_dw, g_conv_ln, b_conv_ln, w_pw2, b_pw2, w_out, g_post, b_post):
    depth = w_in.shape[0]
    alpha = (2.0 * depth) ** 0.25
    bsz, seq, _ = x_prompt.shape
    dbsz, dseq, _ = x_sample.shape
    tl_prompt = 64

    hp = x_prompt
    hs = x_sample.reshape(dbsz * dseq, D_MODEL)
    outs = {k: [] for k in ("re_p", "im_p", "cv_p", "re_s", "im_s", "cv_s")}
    row = lambda a: a.reshape(1, -1)
    for layer in range(depth):
        s_in, s_out, s_dir, l2re, l2im = _prep_call(
            lam_re[layer], lam_im[layer], log_dt[layer], b_re[layer], b_im[layer],
            c_re[layer], c_im[layer])
        wdw8 = jnp.broadcast_to(w_dw[layer][:, None, :], (CONV_K, SUBLANES, W_CONV))
        consts = (w_in[layer].astype(BF16), row(b_in[layer]), l2re, l2im, s_in, s_out, s_dir,
                  row(d_skip[layer]), w_glu[layer].astype(BF16), row(b_glu[layer]), wdw8,
                  row(b_dw[layer]), row(g_conv_ln[layer]), row(b_conv_ln[layer]),
                  w_pw2[layer].astype(BF16), row(b_pw2[layer]), w_out[layer].astype(BF16),
                  row(g_post[layer]), row(b_post[layer]))

        hp, re_p, im_p, cv_p = _layer_call(
            hp, None, consts, nb=bsz, tl=tl_prompt,
            n_steps=seq // tl_prompt, alpha=alpha, conv_steps=8, name="layer_prompt")

        h0re = state_ssm_re[layer].reshape(dbsz, N_STATE_ALL)
        h0im = state_ssm_im[layer].reshape(dbsz, N_STATE_ALL)
        cbuf0 = state_conv[layer].transpose(1, 0, 2).reshape((CONV_K - 1) * dbsz, W_CONV)
        hs, re_s, im_s, cv_s = _layer_call(
            hs, (h0re, h0im, cbuf0), consts, nb=dbsz, tl=dseq, n_steps=1, alpha=alpha,
            conv_steps=dseq, name="layer_sample")

        unrow = lambda a, n: a.reshape(CONV_K - 1, n, W_CONV).transpose(1, 0, 2)
        outs["re_p"].append(re_p.reshape(bsz, N_GROUPS, N_STATE))
        outs["im_p"].append(im_p.reshape(bsz, N_GROUPS, N_STATE))
        outs["cv_p"].append(unrow(cv_p, bsz))
        outs["re_s"].append(re_s.reshape(dbsz, N_GROUPS, N_STATE))
        outs["im_s"].append(im_s.reshape(dbsz, N_GROUPS, N_STATE))
        outs["cv_s"].append(unrow(cv_s, dbsz))

    y_sample = hs.reshape(dbsz, dseq, D_MODEL)
    return (hp, y_sample,
            jnp.stack(outs["re_p"]), jnp.stack(outs["im_p"]), jnp.stack(outs["cv_p"]),
            jnp.stack(outs["re_s"]), jnp.stack(outs["im_s"]), jnp.stack(outs["cv_s"]))
```

```python
import functools

import jax
import jax.numpy as jnp
from jax import lax
from jax.experimental import pallas as pl
from jax.experimental.pallas import tpu as pltpu

D_MODEL = 1024
W_SSM = 512
W_CONV = 512
GROUP_P = 16
N_GROUPS = 32
N_STATE = 64
CONV_K = 31
IN_COLS = 2 * W_SSM + 3 * W_CONV
LN_EPS = 1e-5

LANES = 128
SUBLANES = 8
GROUPS_PER_BLOCK = LANES // GROUP_P
N_BLOCKS = N_GROUPS // GROUPS_PER_BLOCK
BLOCK_STATE = GROUPS_PER_BLOCK * N_STATE
N_STATE_ALL = N_GROUPS * N_STATE
SSM_TILES = W_SSM // LANES
CONV_TILES = W_CONV // LANES
OUT_CHUNKS = 2
VMEM_LIMIT_BYTES = 58 * 1024 * 1024

F32 = jnp.float32
BF16 = jnp.bfloat16


def _layernorm(x, g, b):
    mu = jnp.mean(x, axis=-1, keepdims=True)
    xc = x - mu
    var = jnp.mean(xc * xc, axis=-1, keepdims=True)
    return xc * lax.rsqrt(var + LN_EPS) * g + b


def _slab_pitch(tl):
    if tl % SUBLANES:
        return tl
    pitch = tl
    while (pitch // SUBLANES) % 2 == 0:
        pitch += SUBLANES
    return pitch


def _tb_rows(m, t, pitch):
    return pl.ds(SUBLANES * m * pitch + t, SUBLANES, stride=pitch)


def _store_bt(slab_ref, j, val, nb, tl, pitch):
    if pitch == tl:
        slab_ref[j] = val
    else:
        for b in range(nb):
            slab_ref[j, b * pitch:b * pitch + tl, :] = val[b * tl:(b + 1) * tl]


def _load_bt(slab_ref, j, nb, tl, pitch):
    if pitch == tl:
        return slab_ref[j]
    return jnp.concatenate([slab_ref[j, b * pitch:b * pitch + tl, :] for b in range(nb)], axis=0)


def _gather_tb(slab_ref, j, nb, tl, pitch):
    pieces = [slab_ref[j, _tb_rows(m, t, pitch), :]
              for t in range(tl) for m in range(nb // SUBLANES)]
    return jnp.concatenate(pieces, axis=0)


def _prep_kernel(lam_ref, lamrow_ref, bt_ref, cg_ref,
                 sin_ref, sout_ref, sdir_ref, l2re_ref, l2im_ref):
    gp = N_GROUPS * GROUP_P
    abt = (((1,), (1,)), ((), ()))

    def zoh(lr, li, ldt):
        dt = jnp.exp(ldt)
        mag = jnp.exp(lr * dt)
        ang = li * dt
        return mag * jnp.cos(ang), mag * jnp.sin(ang)

    def per_channel(a):
        wide = jnp.broadcast_to(a[:, None, :], (N_GROUPS, GROUP_P, BLOCK_STATE))
        return wide.reshape(gp, BLOCK_STATE)

    lr, li = lam_ref[0], lam_ref[1]
    ar, ai = zoh(lr, li, lam_ref[2])
    nr, ni = ar - 1.0, ai
    den = lr * lr + li * li
    cr = per_channel((nr * lr + ni * li) / den)
    ci = per_channel((ni * lr - nr * li) / den)
    ar, ai = per_channel(ar), per_channel(ai)
    shape = ar.shape
    row_g = lax.broadcasted_iota(jnp.int32, shape, 0) // GROUP_P % GROUPS_PER_BLOCK
    col_g = lax.broadcasted_iota(jnp.int32, shape, 1) // N_STATE
    keep = row_g == col_g
    br, bi = bt_ref[0], bt_ref[1]
    bbr = jnp.where(keep, cr * br - ci * bi, 0.0)
    bbi = jnp.where(keep, cr * bi + ci * br, 0.0)
    lbr = ar * bbr - ai * bbi
    lbi = ar * bbi + ai * bbr
    c0r = jnp.where(keep, cg_ref[0], 0.0)
    c0i = jnp.where(keep, cg_ref[1], 0.0)
    c1r = ar * c0r - ai * c0i
    c1i = ar * c0i + ai * c0r
    c2r = ar * c1r - ai * c1i
    c2i = ar * c1i + ai * c1r

    for k in range(N_BLOCKS):
        rs = slice(LANES * k, LANES * (k + 1))
        b_k = jnp.concatenate([bbr[rs], bbi[rs]], axis=1)
        lb_k = jnp.concatenate([lbr[rs], lbi[rs]], axis=1)
        sin_ref[k, 0:LANES, :] = lb_k.astype(BF16)
        sin_ref[k, LANES:2 * LANES, :] = b_k.astype(BF16)
        c_k = jnp.concatenate([c0r[rs], -c0i[rs]], axis=1)
        c12_k = jnp.concatenate([jnp.concatenate([c1r[rs], -c1i[rs]], axis=1),
                                 jnp.concatenate([c2r[rs], -c2i[rs]], axis=1)], axis=0)
        sout_ref[k] = c12_k.T.astype(BF16)
        c_bf = c_k.astype(BF16)
        cb = lax.dot_general(b_k.astype(BF16), c_bf, abt, preferred_element_type=F32)
        clb = lax.dot_general(lb_k.astype(BF16), c_bf, abt, preferred_element_type=F32)
        sdir_ref[k, 0:LANES, 0:LANES] = cb.astype(BF16)
        sdir_ref[k, 0:LANES, LANES:2 * LANES] = clb.astype(BF16)
        sdir_ref[k, LANES:2 * LANES, 0:LANES] = jnp.zeros((LANES, LANES), BF16)
        sdir_ref[k, LANES:2 * LANES, LANES:2 * LANES] = cb.astype(BF16)

    a_r, a_i = zoh(lamrow_ref[0], lamrow_ref[1], lamrow_ref[2])
    l2re_ref[...] = jnp.broadcast_to(a_r * a_r - a_i * a_i, l2re_ref.shape)
    l2im_ref[...] = jnp.broadcast_to(2.0 * a_r * a_i, l2im_ref.shape)


def _layer_kernel(nb, tl, n_steps, alpha, conv_steps, zero_state, x_ref, *refs):
    if zero_state:
        refs = (None, None, None) + refs
    _layer_body(nb, tl, n_steps, alpha, conv_steps, x_ref, *refs)


def _layer_body(nb, tl, n_steps, alpha, conv_steps,
                x_ref, h0re_ref, h0im_ref, cbuf0_ref, win_ref, bin_ref, l2re_ref, l2im_ref,
                sin_ref, sout_ref, sdir_ref, dskip_ref, wglu_ref, bglu_ref, wdw_ref, bdw_ref,
                gcl_ref, bcl_ref, wpw_ref, bpw_ref, wout_ref, gpost_ref, bpost_ref,
                y_ref, hre_ref, him_ref, cbuf_ref,
                z_ref, bu_ref, slab_ref, full_ref, mix_ref):
    rows = nb * tl
    hist = (CONV_K - 1) * nb
    pitch = _slab_pitch(tl)
    nsub = nb // SUBLANES
    c0 = 2 * W_SSM

    def init_state():
        if h0re_ref is None:
            hre_ref[...] = jnp.zeros_like(hre_ref)
            him_ref[...] = jnp.zeros_like(him_ref)
            full_ref[0:hist, :] = jnp.zeros((hist, W_CONV), F32)
        else:
            hre_ref[...] = h0re_ref[...]
            him_ref[...] = h0im_ref[...]
            full_ref[0:hist, :] = cbuf0_ref[...]

    if n_steps > 1:
        pl.when(pl.program_id(0) == 0)(init_state)
    else:
        init_state()

    half = rows // 2

    def s5_block(k):
        cs = slice(LANES * k, LANES * (k + 1))
        ss = slice(BLOCK_STATE * k, BLOCK_STATE * (k + 1))
        pieces = {(t, m): slab_ref[k, _tb_rows(m, t, pitch), :]
                  for t in range(tl) for m in range(nsub)}
        u_pair = jnp.concatenate(
            [jnp.concatenate([pieces[2 * j, m], pieces[2 * j + 1, m]], axis=1)
             for j in range(tl // 2) for m in range(nsub)], axis=0)
        u_bf = u_pair.astype(BF16)
        bu_ref[0:nb, 0:BLOCK_STATE] = hre_ref[:, ss]
        bu_ref[0:nb, BLOCK_STATE:2 * BLOCK_STATE] = him_ref[:, ss]
        bu_ref[nb:nb + half, :] = jnp.dot(u_bf, sin_ref[k], preferred_element_type=F32)
        lr = l2re_ref[:, ss]
        li = l2im_ref[:, ss]
        for m in range(nsub):
            rs = slice(SUBLANES * m, SUBLANES * (m + 1))
            hr, hi = bu_ref[rs, 0:BLOCK_STATE], bu_ref[rs, BLOCK_STATE:2 * BLOCK_STATE]
            for j in range(tl // 2):
                row = slice((j + 1) * nb + SUBLANES * m, (j + 1) * nb + SUBLANES * (m + 1))
                br = bu_ref[row, 0:BLOCK_STATE]
                bi = bu_ref[row, BLOCK_STATE:2 * BLOCK_STATE]
                hr, hi = lr * hr - li * hi + br, lr * hi + li * hr + bi
                bu_ref[row, 0:BLOCK_STATE] = hr
                bu_ref[row, BLOCK_STATE:2 * BLOCK_STATE] = hi
            hre_ref[rs, ss] = hr
            him_ref[rs, ss] = hi
        dsk = dskip_ref[:, cs]
        y_pair = (jnp.dot(bu_ref[0:half, :].astype(BF16), sout_ref[k], preferred_element_type=F32)
                  + jnp.dot(u_bf, sdir_ref[k], preferred_element_type=F32)
                  + u_pair * jnp.concatenate([dsk, dsk], axis=1))
        for j in range(tl // 2):
            for m in range(nsub):
                r0 = j * nb + SUBLANES * m
                slab_ref[k, _tb_rows(m, 2 * j, pitch), :] = y_pair[r0:r0 + SUBLANES, 0:LANES]
                slab_ref[k, _tb_rows(m, 2 * j + 1, pitch), :] = (
                    y_pair[r0:r0 + SUBLANES, LANES:2 * LANES])

    xb = x_ref[...].reshape(rows, D_MODEL).astype(BF16)

    def in_proj(lo, hi):
        z_ref[:, lo:hi] = (jnp.dot(xb, win_ref[:, lo:hi], preferred_element_type=F32)
                           + bin_ref[:, lo:hi])

    def s5_branch():
        ys = jnp.concatenate([_load_bt(slab_ref, j, nb, tl, pitch) for j in range(SSM_TILES)],
                             axis=1)
        sg = jax.nn.gelu(ys)
        in_proj(W_SSM, c0)
        glu = jnp.dot(sg.astype(BF16), wglu_ref[...], preferred_element_type=F32) + bglu_ref[...]
        s = sg * jax.nn.sigmoid(glu) * jax.nn.silu(z_ref[:, W_SSM:c0])
        mix_ref[:, 0:W_SSM] = s.astype(BF16)

    cw = 2 * LANES
    for h in range(CONV_TILES // 2):
        a_lo, b_lo = c0 + h * cw, c0 + W_CONV + h * cw
        in_proj(a_lo, a_lo + cw)
        in_proj(b_lo, b_lo + cw)
        v = z_ref[:, a_lo:a_lo + cw] * jax.nn.sigmoid(z_ref[:, b_lo:b_lo + cw])
        for j in range(2):
            _store_bt(slab_ref, SSM_TILES + 2 * h + j, v[:, LANES * j:LANES * (j + 1)],
                      nb, tl, pitch)
        if h == 0:
            in_proj(0, W_SSM)
            for j in range(SSM_TILES):
                _store_bt(slab_ref, j, z_ref[:, LANES * j:LANES * (j + 1)], nb, tl, pitch)
            for k in range(N_BLOCKS):
                s5_block(k)
            s5_branch()

    for c in range(CONV_TILES):
        lanes = slice(LANES * c, LANES * (c + 1))
        full_ref[hist:hist + rows, lanes] = _gather_tb(slab_ref, SSM_TILES + c, nb, tl, pitch)
    for c in range(CONV_TILES):
        lanes = slice(LANES * c, LANES * (c + 1))
        w = [wdw_ref[tap, :, lanes] for tap in range(CONV_K)]
        for m in range(nsub):
            for t0 in range(0, tl, conv_steps):
                base = t0 * nb + SUBLANES * m
                accs = [None] * conv_steps
                for j in range(conv_steps + CONV_K - 1):
                    xj = full_ref[base + j * nb:base + j * nb + SUBLANES, lanes]
                    for r in range(conv_steps):
                        tap = j - r
                        if 0 <= tap < CONV_K:
                            term = w[tap] * xj
                            accs[r] = term if accs[r] is None else accs[r] + term
                for r in range(conv_steps):
                    slab_ref[SSM_TILES + c, _tb_rows(m, t0 + r, pitch), :] = accs[r]
    cbuf_ref[...] = full_ref[rows:rows + hist, :]
    if n_steps > 1:
        full_ref[0:hist, :] = full_ref[rows:rows + hist, :]
    cv = jnp.concatenate([_load_bt(slab_ref, SSM_TILES + j, nb, tl, pitch)
                          for j in range(CONV_TILES)], axis=1) + bdw_ref[...]
    act = jax.nn.silu(_layernorm(cv, gcl_ref[...], bcl_ref[...]))
    in_proj(c0 + 2 * W_CONV, IN_COLS)
    cpw = jnp.dot(act.astype(BF16), wpw_ref[...], preferred_element_type=F32) + bpw_ref[...]
    cg = cpw * jax.nn.silu(z_ref[:, c0 + 2 * W_CONV:IN_COLS])
    mix_ref[:, W_SSM:W_SSM + W_CONV] = cg.astype(BF16)

    x2 = x_ref[...].reshape(rows, D_MODEL)
    rc = rows // OUT_CHUNKS
    for r in range(OUT_CHUNKS):
        rsl = slice(r * rc, (r + 1) * rc)
        mix = jnp.dot(mix_ref[rsl, :], wout_ref[...], preferred_element_type=F32)
        out = _layernorm(alpha * x2[rsl] + mix, gpost_ref[...], bpost_ref[...])
        if len(y_ref.shape) == 3:
            bsl = slice(r * (nb // OUT_CHUNKS), (r + 1) * (nb // OUT_CHUNKS))
            y_ref[bsl] = out.reshape(nb // OUT_CHUNKS, tl, D_MODEL)
        else:
            y_ref[rsl, :] = out


def _layer_call(x, state, consts, *, nb, tl, n_steps, alpha, conv_steps, name):
    rows = nb * tl
    hist = (CONV_K - 1) * nb
    pitch = _slab_pitch(tl)
    assert nb % SUBLANES == 0 and tl % conv_steps == 0 and (nb * pitch) % SUBLANES == 0
    assert tl % 2 == 0 and (n_steps == 1 or (tl >= CONV_K - 1 and tl % SUBLANES == 0))
    out_shape = (
        jax.ShapeDtypeStruct(x.shape, F32),
        jax.ShapeDtypeStruct((nb, N_STATE_ALL), F32),
        jax.ShapeDtypeStruct((nb, N_STATE_ALL), F32),
        jax.ShapeDtypeStruct((hist, W_CONV), F32),
    )
    scratch = [
        pltpu.VMEM((rows, IN_COLS), F32),
        pltpu.VMEM((rows // 2 + nb, 2 * BLOCK_STATE), F32),
        pltpu.VMEM((SSM_TILES + CONV_TILES, nb * pitch, LANES), F32),
        pltpu.VMEM((hist + rows, W_CONV), F32),
        pltpu.VMEM((rows, W_SSM + W_CONV), BF16),
    ]
    operands = (x,) + (() if state is None else tuple(state)) + tuple(consts)
    kernel = functools.partial(_layer_kernel, nb, tl, n_steps, alpha, conv_steps, state is None)
    if n_steps == 1:
        assert x.shape == (rows, D_MODEL)
        return pl.pallas_call(
            kernel, out_shape=out_shape, scratch_shapes=scratch, name=name,
            compiler_params=pltpu.CompilerParams(vmem_limit_bytes=VMEM_LIMIT_BYTES),
        )(*operands)

    assert x.shape == (nb, tl * n_steps, D_MODEL)

    def whole(a):
        zeros = (0,) * a.ndim
        return pl.BlockSpec(a.shape, lambda i: zeros, pipeline_mode=pl.Buffered(1))

    x_spec = pl.BlockSpec((nb, tl, D_MODEL), lambda i: (0, i, 0))
    in_specs = [x_spec] + [whole(a) for a in operands[1:]]
    out_specs = (
        x_spec,
        pl.BlockSpec((nb, N_STATE_ALL), lambda i: (0, 0)),
        pl.BlockSpec((nb, N_STATE_ALL), lambda i: (0, 0)),
        pl.BlockSpec((hist, W_CONV), lambda i: (0, 0)),
    )
    return pl.pallas_call(
        kernel, out_shape=out_shape, grid=(n_steps,), in_specs=in_specs, out_specs=out_specs,
        scratch_shapes=scratch, name=name,
        compiler_params=pltpu.CompilerParams(dimension_semantics=("arbitrary",),
                                             vmem_limit_bytes=VMEM_LIMIT_BYTES),
    )(*operands)


def _prep_call(lam_re, lam_im, log_dt, b_re, b_im, c_re, c_im):
    gp = N_GROUPS * GROUP_P
    tile_b = lambda a: jnp.tile(a, (1, GROUPS_PER_BLOCK))
    ldt_n = jnp.broadcast_to(log_dt[:, None], (N_GROUPS, N_STATE))
    lam = jnp.stack([tile_b(a) for a in (lam_re, lam_im, ldt_n)])
    lamrow = jnp.stack([a.reshape(1, N_STATE_ALL) for a in (lam_re, lam_im, ldt_n)])
    bt = jnp.stack([tile_b(b.transpose(0, 2, 1).reshape(gp, N_STATE)) for b in (b_re, b_im)])
    cg = jnp.stack([tile_b(c.reshape(gp, N_STATE)) for c in (c_re, c_im)])
    out_shape = (
        jax.ShapeDtypeStruct((N_BLOCKS, 2 * LANES, 2 * BLOCK_STATE), BF16),
        jax.ShapeDtypeStruct((N_BLOCKS, 2 * BLOCK_STATE, 2 * LANES), BF16),
        jax.ShapeDtypeStruct((N_BLOCKS, 2 * LANES, 2 * LANES), BF16),
        jax.ShapeDtypeStruct((SUBLANES, N_STATE_ALL), F32),
        jax.ShapeDtypeStruct((SUBLANES, N_STATE_ALL), F32),
    )
    return pl.pallas_call(_prep_kernel, out_shape=out_shape, name="s5_prep")(lam, lamrow, bt, cg)


def kernel(x_prompt, x_sample, state_ssm_re, state_ssm_im, state_conv, w_in, b_in, lam_re, lam_im, log_dt, b_re, b_im, c_re, c_im, d_skip, w_glu, b_glu, w_dw, b_dw, g_conv_ln, b_conv_ln, w_pw2, b_pw2, w_out, g_post, b_post):
    depth = w_in.shape[0]
    alpha = (2.0 * depth) ** 0.25
    bsz, seq, _ = x_prompt.shape
    dbsz, dseq, _ = x_sample.shape
    tl_prompt = 128

    hp = x_prompt
    hs = x_sample.reshape(dbsz * dseq, D_MODEL)
    outs = {k: [] for k in ("re_p", "im_p", "cv_p", "re_s", "im_s", "cv_s")}
    row = lambda a: a.reshape(1, -1)
    for layer in range(depth):
        s_in, s_out, s_dir, l2re, l2im = _prep_call(
            lam_re[layer], lam_im[layer], log_dt[layer], b_re[layer], b_im[layer],
            c_re[layer], c_im[layer])
        wdw8 = jnp.broadcast_to(w_dw[layer][:, None, :], (CONV_K, SUBLANES, W_CONV))
        consts = (w_in[layer].astype(BF16), row(b_in[layer]), l2re, l2im, s_in, s_out, s_dir,
                  row(d_skip[layer]), w_glu[layer].astype(BF16), row(b_glu[layer]), wdw8,
                  row(b_dw[layer]), row(g_conv_ln[layer]), row(b_conv_ln[layer]),
                  w_pw2[layer].astype(BF16), row(b_pw2[layer]), w_out[layer].astype(BF16),
                  row(g_post[layer]), row(b_post[layer]))

        hp, re_p, im_p, cv_p = _layer_call(
            hp, None, consts, nb=bsz, tl=tl_prompt,
            n_steps=seq // tl_prompt, alpha=alpha, conv_steps=8, name="layer_prompt")

        h0re = state_ssm_re[layer].reshape(dbsz, N_STATE_ALL)
        h0im = state_ssm_im[layer].reshape(dbsz, N_STATE_ALL)
        cbuf0 = state_conv[layer].transpose(1, 0, 2).reshape((CONV_K - 1) * dbsz, W_CONV)
        hs, re_s, im_s, cv_s = _layer_call(
            hs, (h0re, h0im, cbuf0), consts, nb=dbsz, tl=dseq, n_steps=1, alpha=alpha,
            conv_steps=dseq, name="layer_sample")

        unrow = lambda a, n: a.reshape(CONV_K - 1, n, W_CONV).transpose(1, 0, 2)
        outs["re_p"].append(re_p.reshape(bsz, N_GROUPS, N_STATE))
        outs["im_p"].append(im_p.reshape(bsz, N_GROUPS, N_STATE))
        outs["cv_p"].append(unrow(cv_p, bsz))
        outs["re_s"].append(re_s.reshape(dbsz, N_GROUPS, N_STATE))
        outs["im_s"].append(im_s.reshape(dbsz, N_GROUPS, N_STATE))
        outs["cv_s"].append(unrow(cv_s, dbsz))

    y_sample = hs.reshape(dbsz, dseq, D_MODEL)
    return (hp, y_sample,
            jnp.stack(outs["re_p"]), jnp.stack(outs["im_p"]), jnp.stack(outs["cv_p"]),
            jnp.stack(outs["re_s"]), jnp.stack(outs["im_s"]), jnp.stack(outs["cv_s"]))
```

```python
import functools

import jax
import jax.numpy as jnp
from jax import lax
from jax.experimental import pallas as pl
from jax.experimental.pallas import tpu as pltpu

D_MODEL = 1024
W_SSM = 512
W_CONV = 512
GROUP_P = 16
N_GROUPS = 32
N_STATE = 64
CONV_K = 31
IN_COLS = 2 * W_SSM + 3 * W_CONV
LN_EPS = 1e-5

LANES = 128
SUBLANES = 8
GROUPS_PER_BLOCK = LANES // GROUP_P
N_BLOCKS = N_GROUPS // GROUPS_PER_BLOCK
BLOCK_STATE = GROUPS_PER_BLOCK * N_STATE
N_STATE_ALL = N_GROUPS * N_STATE
SSM_TILES = W_SSM // LANES
CONV_TILES = W_CONV // LANES
OUT_CHUNK_ROWS = 256
VMEM_LIMIT_BYTES = 58 * 1024 * 1024

F32 = jnp.float32
BF16 = jnp.bfloat16


def _layernorm(x, g, b):
    mu = jnp.mean(x, axis=-1, keepdims=True)
    xc = x - mu
    var = jnp.mean(xc * xc, axis=-1, keepdims=True)
    return xc * lax.rsqrt(var + LN_EPS) * g + b


def _slab_pitch(tl):
    if tl % SUBLANES:
        return tl
    pitch = tl
    while (pitch // SUBLANES) % 2 == 0:
        pitch += SUBLANES
    return pitch


def _tb_rows(m, t, pitch):
    return pl.ds(SUBLANES * m * pitch + t, SUBLANES, stride=pitch)


def _store_bt(slab_ref, j, val, nb, tl, pitch):
    if pitch == tl:
        slab_ref[j] = val
    else:
        for b in range(nb):
            slab_ref[j, b * pitch:b * pitch + tl, :] = val[b * tl:(b + 1) * tl]


def _load_bt(slab_ref, j, nb, tl, pitch):
    if pitch == tl:
        return slab_ref[j]
    return jnp.concatenate([slab_ref[j, b * pitch:b * pitch + tl, :] for b in range(nb)], axis=0)


def _gather_tb(slab_ref, j, nb, tl, pitch):
    pieces = [slab_ref[j, _tb_rows(m, t, pitch), :]
              for t in range(tl) for m in range(nb // SUBLANES)]
    return jnp.concatenate(pieces, axis=0)


def _prep_kernel(lam_ref, lamrow_ref, bt_ref, cg_ref,
                 sin_ref, sout_ref, sdir_ref, l2re_ref, l2im_ref):
    gp = N_GROUPS * GROUP_P
    abt = (((1,), (1,)), ((), ()))

    def zoh(lr, li, ldt):
        dt = jnp.exp(ldt)
        mag = jnp.exp(lr * dt)
        ang = li * dt
        return mag * jnp.cos(ang), mag * jnp.sin(ang)

    def per_channel(a):
        wide = jnp.broadcast_to(a[:, None, :], (N_GROUPS, GROUP_P, BLOCK_STATE))
        return wide.reshape(gp, BLOCK_STATE)

    lr, li = lam_ref[0], lam_ref[1]
    ar, ai = zoh(lr, li, lam_ref[2])
    nr, ni = ar - 1.0, ai
    den = lr * lr + li * li
    cr = per_channel((nr * lr + ni * li) / den)
    ci = per_channel((ni * lr - nr * li) / den)
    ar, ai = per_channel(ar), per_channel(ai)
    shape = ar.shape
    row_g = lax.broadcasted_iota(jnp.int32, shape, 0) // GROUP_P % GROUPS_PER_BLOCK
    col_g = lax.broadcasted_iota(jnp.int32, shape, 1) // N_STATE
    keep = row_g == col_g
    br, bi = bt_ref[0], bt_ref[1]
    bbr = jnp.where(keep, cr * br - ci * bi, 0.0)
    bbi = jnp.where(keep, cr * bi + ci * br, 0.0)
    lbr = ar * bbr - ai * bbi
    lbi = ar * bbi + ai * bbr
    c0r = jnp.where(keep, cg_ref[0], 0.0)
    c0i = jnp.where(keep, cg_ref[1], 0.0)
    c1r = ar * c0r - ai * c0i
    c1i = ar * c0i + ai * c0r
    c2r = ar * c1r - ai * c1i
    c2i = ar * c1i + ai * c1r

    for k in range(N_BLOCKS):
        rs = slice(LANES * k, LANES * (k + 1))
        b_k = jnp.concatenate([bbr[rs], bbi[rs]], axis=1)
        lb_k = jnp.concatenate([lbr[rs], lbi[rs]], axis=1)
        sin_ref[k, 0:LANES, :] = lb_k.astype(BF16)
        sin_ref[k, LANES:2 * LANES, :] = b_k.astype(BF16)
        c_k = jnp.concatenate([c0r[rs], -c0i[rs]], axis=1)
        c12_k = jnp.concatenate([jnp.concatenate([c1r[rs], -c1i[rs]], axis=1),
                                 jnp.concatenate([c2r[rs], -c2i[rs]], axis=1)], axis=0)
        sout_ref[k] = c12_k.T.astype(BF16)
        c_bf = c_k.astype(BF16)
        cb = lax.dot_general(b_k.astype(BF16), c_bf, abt, preferred_element_type=F32)
        clb = lax.dot_general(lb_k.astype(BF16), c_bf, abt, preferred_element_type=F32)
        sdir_ref[k, 0:LANES, 0:LANES] = cb.astype(BF16)
        sdir_ref[k, 0:LANES, LANES:2 * LANES] = clb.astype(BF16)
        sdir_ref[k, LANES:2 * LANES, 0:LANES] = jnp.zeros((LANES, LANES), BF16)
        sdir_ref[k, LANES:2 * LANES, LANES:2 * LANES] = cb.astype(BF16)

    a_r, a_i = zoh(lamrow_ref[0], lamrow_ref[1], lamrow_ref[2])
    l2re_ref[...] = jnp.broadcast_to(a_r * a_r - a_i * a_i, l2re_ref.shape)
    l2im_ref[...] = jnp.broadcast_to(2.0 * a_r * a_i, l2im_ref.shape)


def _layer_kernel(nb, tl, n_steps, alpha, conv_steps, zero_state, x_ref, *refs):
    if zero_state:
        refs = (None, None, None) + refs
    _layer_body(nb, tl, n_steps, alpha, conv_steps, x_ref, *refs)


def _layer_body(nb, tl, n_steps, alpha, conv_steps,
                x_ref, h0re_ref, h0im_ref, cbuf0_ref, win_ref, bin_ref, l2re_ref, l2im_ref,
                sin_ref, sout_ref, sdir_ref, dskip_ref, wglu_ref, bglu_ref, wdw_ref, bdw_ref,
                gcl_ref, bcl_ref, wpw_ref, bpw_ref, wout_ref, gpost_ref, bpost_ref,
                y_ref, hre_ref, him_ref, cbuf_ref,
                z_ref, bu_ref, slab_ref, full_ref, mix_ref):
    rows = nb * tl
    hist = (CONV_K - 1) * nb
    pitch = _slab_pitch(tl)
    nsub = nb // SUBLANES
    c0 = 2 * W_SSM

    def init_state():
        if h0re_ref is None:
            hre_ref[...] = jnp.zeros_like(hre_ref)
            him_ref[...] = jnp.zeros_like(him_ref)
            full_ref[0:hist, :] = jnp.zeros((hist, W_CONV), F32)
        else:
            hre_ref[...] = h0re_ref[...]
            him_ref[...] = h0im_ref[...]
            full_ref[0:hist, :] = cbuf0_ref[...]

    if n_steps > 1:
        pl.when(pl.program_id(0) == 0)(init_state)
    else:
        init_state()

    half = rows // 2

    def s5_block(k):
        cs = slice(LANES * k, LANES * (k + 1))
        ss = slice(BLOCK_STATE * k, BLOCK_STATE * (k + 1))
        pieces = {(t, m): slab_ref[k, _tb_rows(m, t, pitch), :]
                  for t in range(tl) for m in range(nsub)}
        u_pair = jnp.concatenate(
            [jnp.concatenate([pieces[2 * j, m], pieces[2 * j + 1, m]], axis=1)
             for j in range(tl // 2) for m in range(nsub)], axis=0)
        u_bf = u_pair.astype(BF16)
        bu_ref[0:nb, 0:BLOCK_STATE] = hre_ref[:, ss]
        bu_ref[0:nb, BLOCK_STATE:2 * BLOCK_STATE] = him_ref[:, ss]
        bu_ref[nb:nb + half, :] = jnp.dot(u_bf, sin_ref[k], preferred_element_type=F32)
        lr = l2re_ref[:, ss]
        li = l2im_ref[:, ss]
        for m in range(nsub):
            rs = slice(SUBLANES * m, SUBLANES * (m + 1))
            hr, hi = bu_ref[rs, 0:BLOCK_STATE], bu_ref[rs, BLOCK_STATE:2 * BLOCK_STATE]
            for j in range(tl // 2):
                row = slice((j + 1) * nb + SUBLANES * m, (j + 1) * nb + SUBLANES * (m + 1))
                br = bu_ref[row, 0:BLOCK_STATE]
                bi = bu_ref[row, BLOCK_STATE:2 * BLOCK_STATE]
                hr, hi = lr * hr - li * hi + br, lr * hi + li * hr + bi
                bu_ref[row, 0:BLOCK_STATE] = hr
                bu_ref[row, BLOCK_STATE:2 * BLOCK_STATE] = hi
            hre_ref[rs, ss] = hr
            him_ref[rs, ss] = hi
        dsk = dskip_ref[:, cs]
        y_pair = (jnp.dot(bu_ref[0:half, :].astype(BF16), sout_ref[k], preferred_element_type=F32)
                  + jnp.dot(u_bf, sdir_ref[k], preferred_element_type=F32)
                  + u_pair * jnp.concatenate([dsk, dsk], axis=1))
        for j in range(tl // 2):
            for m in range(nsub):
                r0 = j * nb + SUBLANES * m
                slab_ref[k, _tb_rows(m, 2 * j, pitch), :] = y_pair[r0:r0 + SUBLANES, 0:LANES]
                slab_ref[k, _tb_rows(m, 2 * j + 1, pitch), :] = (
                    y_pair[r0:r0 + SUBLANES, LANES:2 * LANES])

    xb = x_ref[...].reshape(rows, D_MODEL).astype(BF16)

    def in_proj(lo, hi):
        z_ref[:, lo:hi] = (jnp.dot(xb, win_ref[:, lo:hi], preferred_element_type=F32)
                           + bin_ref[:, lo:hi])

    def s5_branch():
        ys = jnp.concatenate([_load_bt(slab_ref, j, nb, tl, pitch) for j in range(SSM_TILES)],
                             axis=1)
        sg = jax.nn.gelu(ys)
        in_proj(W_SSM, c0)
        glu = jnp.dot(sg.astype(BF16), wglu_ref[...], preferred_element_type=F32) + bglu_ref[...]
        s = sg * jax.nn.sigmoid(glu) * jax.nn.silu(z_ref[:, W_SSM:c0])
        mix_ref[:, 0:W_SSM] = s.astype(BF16)

    cw = 2 * LANES
    for h in range(CONV_TILES // 2):
        a_lo, b_lo = c0 + h * cw, c0 + W_CONV + h * cw
        in_proj(a_lo, a_lo + cw)
        in_proj(b_lo, b_lo + cw)
        v = z_ref[:, a_lo:a_lo + cw] * jax.nn.sigmoid(z_ref[:, b_lo:b_lo + cw])
        for j in range(2):
            _store_bt(slab_ref, SSM_TILES + 2 * h + j, v[:, LANES * j:LANES * (j + 1)],
                      nb, tl, pitch)
        if h == 0:
            in_proj(0, W_SSM)
            for j in range(SSM_TILES):
                _store_bt(slab_ref, j, z_ref[:, LANES * j:LANES * (j + 1)], nb, tl, pitch)
            for k in range(N_BLOCKS):
                s5_block(k)
            s5_branch()

    for c in range(CONV_TILES):
        lanes = slice(LANES * c, LANES * (c + 1))
        full_ref[hist:hist + rows, lanes] = _gather_tb(slab_ref, SSM_TILES + c, nb, tl, pitch)
    for c in range(CONV_TILES):
        lanes = slice(LANES * c, LANES * (c + 1))
        w = [wdw_ref[tap, :, lanes] for tap in range(CONV_K)]
        for m in range(nsub):
            for t0 in range(0, tl, conv_steps):
                base = t0 * nb + SUBLANES * m
                accs = [None] * conv_steps
                for j in range(conv_steps + CONV_K - 1):
                    xj = full_ref[base + j * nb:base + j * nb + SUBLANES, lanes]
                    for r in range(conv_steps):
                        tap = j - r
                        if 0 <= tap < CONV_K:
                            term = w[tap] * xj
                            accs[r] = term if accs[r] is None else accs[r] + term
                for r in range(conv_steps):
                    slab_ref[SSM_TILES + c, _tb_rows(m, t0 + r, pitch), :] = accs[r]
    cbuf_ref[...] = full_ref[rows:rows + hist, :]
    if n_steps > 1:
        full_ref[0:hist, :] = full_ref[rows:rows + hist, :]
    cv = jnp.concatenate([_load_bt(slab_ref, SSM_TILES + j, nb, tl, pitch)
                          for j in range(CONV_TILES)], axis=1) + bdw_ref[...]
    act = jax.nn.silu(_layernorm(cv, gcl_ref[...], bcl_ref[...]))
    in_proj(c0 + 2 * W_CONV, IN_COLS)
    cpw = jnp.dot(act.astype(BF16), wpw_ref[...], preferred_element_type=F32) + bpw_ref[...]
    cg = cpw * jax.nn.silu(z_ref[:, c0 + 2 * W_CONV:IN_COLS])
    mix_ref[:, W_SSM:W_SSM + W_CONV] = cg.astype(BF16)

    x2 = x_ref[...].reshape(rows, D_MODEL)
    n_chunks = max(1, rows // OUT_CHUNK_ROWS)
    rc = rows // n_chunks
    for r in range(n_chunks):
        rsl = slice(r * rc, (r + 1) * rc)
        mix = jnp.dot(mix_ref[rsl, :], wout_ref[...], preferred_element_type=F32)
        out = _layernorm(alpha * x2[rsl] + mix, gpost_ref[...], bpost_ref[...])
        if len(y_ref.shape) == 3:
            bsl = slice(r * (nb // n_chunks), (r + 1) * (nb // n_chunks))
            y_ref[bsl] = out.reshape(nb // n_chunks, tl, D_MODEL)
        else:
            y_ref[rsl, :] = out


def _layer_call(x, state, consts, *, nb, tl, n_steps, alpha, conv_steps, name):
    rows = nb * tl
    hist = (CONV_K - 1) * nb
    pitch = _slab_pitch(tl)
    assert nb % SUBLANES == 0 and tl % conv_steps == 0 and (nb * pitch) % SUBLANES == 0
    assert tl % 2 == 0 and (n_steps == 1 or (tl >= CONV_K - 1 and tl % SUBLANES == 0))
    out_shape = (
        jax.ShapeDtypeStruct(x.shape, F32),
        jax.ShapeDtypeStruct((nb, N_STATE_ALL), F32),
        jax.ShapeDtypeStruct((nb, N_STATE_ALL), F32),
        jax.ShapeDtypeStruct((hist, W_CONV), F32),
    )
    scratch = [
        pltpu.VMEM((rows, IN_COLS), F32),
        pltpu.VMEM((rows // 2 + nb, 2 * BLOCK_STATE), F32),
        pltpu.VMEM((SSM_TILES + CONV_TILES, nb * pitch, LANES), F32),
        pltpu.VMEM((hist + rows, W_CONV), F32),
        pltpu.VMEM((rows, W_SSM + W_CONV), BF16),
    ]
    operands = (x,) + (() if state is None else tuple(state)) + tuple(consts)
    kernel = functools.partial(_layer_kernel, nb, tl, n_steps, alpha, conv_steps, state is None)
    if n_steps == 1:
        assert x.shape == (nb, tl, D_MODEL)
        return pl.pallas_call(
            kernel, out_shape=out_shape, scratch_shapes=scratch, name=name,
            compiler_params=pltpu.CompilerParams(vmem_limit_bytes=VMEM_LIMIT_BYTES),
        )(*operands)

    assert x.shape == (nb, tl * n_steps, D_MODEL)

    def whole(a):
        zeros = (0,) * a.ndim
        return pl.BlockSpec(a.shape, lambda i: zeros, pipeline_mode=pl.Buffered(1))

    x_spec = pl.BlockSpec((nb, tl, D_MODEL), lambda i: (0, i, 0))
    in_specs = [x_spec] + [whole(a) for a in operands[1:]]
    out_specs = (
        x_spec,
        pl.BlockSpec((nb, N_STATE_ALL), lambda i: (0, 0)),
        pl.BlockSpec((nb, N_STATE_ALL), lambda i: (0, 0)),
        pl.BlockSpec((hist, W_CONV), lambda i: (0, 0)),
    )
    return pl.pallas_call(
        kernel, out_shape=out_shape, grid=(n_steps,), in_specs=in_specs, out_specs=out_specs,
        scratch_shapes=scratch, name=name,
        compiler_params=pltpu.CompilerParams(dimension_semantics=("arbitrary",),
                                             vmem_limit_bytes=VMEM_LIMIT_BYTES),
    )(*operands)


def _prep_call(lam_re, lam_im, log_dt, b_re, b_im, c_re, c_im):
    gp = N_GROUPS * GROUP_P
    tile_b = lambda a: jnp.tile(a, (1, GROUPS_PER_BLOCK))
    ldt_n = jnp.broadcast_to(log_dt[:, None], (N_GROUPS, N_STATE))
    lam = jnp.stack([tile_b(a) for a in (lam_re, lam_im, ldt_n)])
    lamrow = jnp.stack([a.reshape(1, N_STATE_ALL) for a in (lam_re, lam_im, ldt_n)])
    bt = jnp.stack([tile_b(b.transpose(0, 2, 1).reshape(gp, N_STATE)) for b in (b_re, b_im)])
    cg = jnp.stack([tile_b(c.reshape(gp, N_STATE)) for c in (c_re, c_im)])
    out_shape = (
        jax.ShapeDtypeStruct((N_BLOCKS, 2 * LANES, 2 * BLOCK_STATE), BF16),
        jax.ShapeDtypeStruct((N_BLOCKS, 2 * BLOCK_STATE, 2 * LANES), BF16),
        jax.ShapeDtypeStruct((N_BLOCKS, 2 * LANES, 2 * LANES), BF16),
        jax.ShapeDtypeStruct((SUBLANES, N_STATE_ALL), F32),
        jax.ShapeDtypeStruct((SUBLANES, N_STATE_ALL), F32),
    )
    return pl.pallas_call(_prep_kernel, out_shape=out_shape, name="s5_prep")(lam, lamrow, bt, cg)


def kernel(x_prompt, x_sample, state_ssm_re, state_ssm_im, state_conv, w_in, b_in, lam_re, lam_im, log_dt, b_re, b_im, c_re, c_im, d_skip, w_glu, b_glu, w_dw, b_dw, g_conv_ln, b_conv_ln, w_pw2, b_pw2, w_out, g_post, b_post):
    depth = w_in.shape[0]
    alpha = (2.0 * depth) ** 0.25
    bsz, seq, _ = x_prompt.shape
    dbsz, dseq, _ = x_sample.shape
    tl_prompt = 128

    hp = x_prompt
    hs = x_sample
    outs = {k: [] for k in ("re_p", "im_p", "cv_p", "re_s", "im_s", "cv_s")}
    row = lambda a: a.reshape(1, -1)
    for layer in range(depth):
        s_in, s_out, s_dir, l2re, l2im = _prep_call(
            lam_re[layer], lam_im[layer], log_dt[layer], b_re[layer], b_im[layer],
            c_re[layer], c_im[layer])
        wdw8 = jnp.broadcast_to(w_dw[layer][:, None, :], (CONV_K, SUBLANES, W_CONV))
        consts = (w_in[layer].astype(BF16), row(b_in[layer]), l2re, l2im, s_in, s_out, s_dir,
                  row(d_skip[layer]), w_glu[layer].astype(BF16), row(b_glu[layer]), wdw8,
                  row(b_dw[layer]), row(g_conv_ln[layer]), row(b_conv_ln[layer]),
                  w_pw2[layer].astype(BF16), row(b_pw2[layer]), w_out[layer].astype(BF16),
                  row(g_post[layer]), row(b_post[layer]))

        hp, re_p, im_p, cv_p = _layer_call(
            hp, None, consts, nb=bsz, tl=tl_prompt,
            n_steps=seq // tl_prompt, alpha=alpha, conv_steps=8, name="layer_prompt")

        h0re = state_ssm_re[layer].reshape(dbsz, N_STATE_ALL)
        h0im = state_ssm_im[layer].reshape(dbsz, N_STATE_ALL)
        cbuf0 = state_conv[layer].transpose(1, 0, 2).reshape((CONV_K - 1) * dbsz, W_CONV)
        hs, re_s, im_s, cv_s = _layer_call(
            hs, (h0re, h0im, cbuf0), consts, nb=dbsz, tl=dseq, n_steps=1, alpha=alpha,
            conv_steps=dseq, name="layer_sample")

        unrow = lambda a, n: a.reshape(CONV_K - 1, n, W_CONV).transpose(1, 0, 2)
        outs["re_p"].append(re_p.reshape(bsz, N_GROUPS, N_STATE))
        outs["im_p"].append(im_p.reshape(bsz, N_GROUPS, N_STATE))
        outs["cv_p"].append(unrow(cv_p, bsz))
        outs["re_s"].append(re_s.reshape(dbsz, N_GROUPS, N_STATE))
        outs["im_s"].append(im_s.reshape(dbsz, N_GROUPS, N_STATE))
        outs["cv_s"].append(unrow(cv_s, dbsz))

    return (hp, hs,
            jnp.stack(outs["re_p"]), jnp.stack(outs["im_p"]), jnp.stack(outs["cv_p"]),
            jnp.stack(outs["re_s"]), jnp.stack(outs["im_s"]), jnp.stack(outs["cv_s"]))
```

```python
import functools

import jax
import jax.numpy as jnp
from jax import lax
from jax.experimental import pallas as pl
from jax.experimental.pallas import tpu as pltpu

D_MODEL = 1024
W_SSM = 512
W_CONV = 512
GROUP_P = 16
N_GROUPS = 32
N_STATE = 64
CONV_K = 31
IN_COLS = 2 * W_SSM + 3 * W_CONV
LN_EPS = 1e-5

LANES = 128
SUBLANES = 8
GROUPS_PER_BLOCK = LANES // GROUP_P
N_BLOCKS = N_GROUPS // GROUPS_PER_BLOCK
BLOCK_STATE = GROUPS_PER_BLOCK * N_STATE
N_STATE_ALL = N_GROUPS * N_STATE
SSM_TILES = W_SSM // LANES
CONV_TILES = W_CONV // LANES
OUT_CHUNK_ROWS = 256
VMEM_LIMIT_BYTES = 58 * 1024 * 1024

F32 = jnp.float32
BF16 = jnp.bfloat16


def _layernorm(x, g, b):
    mu = jnp.mean(x, axis=-1, keepdims=True)
    xc = x - mu
    var = jnp.mean(xc * xc, axis=-1, keepdims=True)
    return xc * lax.rsqrt(var + LN_EPS) * g + b


def _slab_pitch(tl):
    if tl % SUBLANES:
        return tl
    pitch = tl
    while (pitch // SUBLANES) % 2 == 0:
        pitch += SUBLANES
    return pitch


def _tb_rows(m, t, pitch):
    return pl.ds(SUBLANES * m * pitch + t, SUBLANES, stride=pitch)


def _store_bt(slab_ref, j, val, nb, tl, pitch):
    if pitch == tl:
        slab_ref[j] = val
    else:
        for b in range(nb):
            slab_ref[j, b * pitch:b * pitch + tl, :] = val[b * tl:(b + 1) * tl]


def _load_bt(slab_ref, j, nb, tl, pitch):
    if pitch == tl:
        return slab_ref[j]
    return jnp.concatenate([slab_ref[j, b * pitch:b * pitch + tl, :] for b in range(nb)], axis=0)


def _gather_tb(slab_ref, j, nb, tl, pitch):
    pieces = [slab_ref[j, _tb_rows(m, t, pitch), :]
              for t in range(tl) for m in range(nb // SUBLANES)]
    return jnp.concatenate(pieces, axis=0)


def _prep_kernel(lam_ref, lamrow_ref, bt_ref, cg_ref,
                 sin_ref, sout_ref, sdir_ref, l2re_ref, l2im_ref):
    gp = N_GROUPS * GROUP_P
    abt = (((1,), (1,)), ((), ()))

    def zoh(lr, li, ldt):
        dt = jnp.exp(ldt)
        mag = jnp.exp(lr * dt)
        ang = li * dt
        return mag * jnp.cos(ang), mag * jnp.sin(ang)

    def per_channel(a):
        wide = jnp.broadcast_to(a[:, None, :], (N_GROUPS, GROUP_P, a.shape[-1]))
        return wide.reshape(gp, a.shape[-1])

    def per_block(a):
        return jnp.tile(a, (1, GROUPS_PER_BLOCK))

    lr, li = lam_ref[0], lam_ref[1]
    ar, ai = zoh(lr, li, lam_ref[2])
    nr, ni = ar - 1.0, ai
    den = lr * lr + li * li
    cr = per_channel(per_block((nr * lr + ni * li) / den))
    ci = per_channel(per_block((ni * lr - nr * li) / den))
    ar, ai = per_channel(per_block(ar)), per_channel(per_block(ai))
    shape = ar.shape
    row_g = lax.broadcasted_iota(jnp.int32, shape, 0) // GROUP_P % GROUPS_PER_BLOCK
    col_g = lax.broadcasted_iota(jnp.int32, shape, 1) // N_STATE
    keep = row_g == col_g
    br, bi = per_block(bt_ref[0]), per_block(bt_ref[1])
    bbr = jnp.where(keep, cr * br - ci * bi, 0.0)
    bbi = jnp.where(keep, cr * bi + ci * br, 0.0)
    lbr = ar * bbr - ai * bbi
    lbi = ar * bbi + ai * bbr
    c0r = jnp.where(keep, per_block(cg_ref[0]), 0.0)
    c0i = jnp.where(keep, per_block(cg_ref[1]), 0.0)
    c1r = ar * c0r - ai * c0i
    c1i = ar * c0i + ai * c0r
    c2r = ar * c1r - ai * c1i
    c2i = ar * c1i + ai * c1r

    for k in range(N_BLOCKS):
        rs = slice(LANES * k, LANES * (k + 1))
        b_k = jnp.concatenate([bbr[rs], bbi[rs]], axis=1)
        lb_k = jnp.concatenate([lbr[rs], lbi[rs]], axis=1)
        sin_ref[k, 0:LANES, :] = lb_k.astype(BF16)
        sin_ref[k, LANES:2 * LANES, :] = b_k.astype(BF16)
        c_k = jnp.concatenate([c0r[rs], -c0i[rs]], axis=1)
        c12_k = jnp.concatenate([jnp.concatenate([c1r[rs], -c1i[rs]], axis=1),
                                 jnp.concatenate([c2r[rs], -c2i[rs]], axis=1)], axis=0)
        sout_ref[k] = c12_k.T.astype(BF16)
        c_bf = c_k.astype(BF16)
        cb = lax.dot_general(b_k.astype(BF16), c_bf, abt, preferred_element_type=F32)
        clb = lax.dot_general(lb_k.astype(BF16), c_bf, abt, preferred_element_type=F32)
        sdir_ref[k, 0:LANES, 0:LANES] = cb.astype(BF16)
        sdir_ref[k, 0:LANES, LANES:2 * LANES] = clb.astype(BF16)
        sdir_ref[k, LANES:2 * LANES, 0:LANES] = jnp.zeros((LANES, LANES), BF16)
        sdir_ref[k, LANES:2 * LANES, LANES:2 * LANES] = cb.astype(BF16)

    a_r, a_i = zoh(lamrow_ref[0], lamrow_ref[1], lamrow_ref[2])
    l2re_ref[...] = jnp.broadcast_to(a_r * a_r - a_i * a_i, l2re_ref.shape)
    l2im_ref[...] = jnp.broadcast_to(2.0 * a_r * a_i, l2im_ref.shape)


def _layer_kernel(nb, tl, n_steps, alpha, conv_steps, zero_state, x_ref, *refs):
    if zero_state:
        refs = (None, None, None) + refs
    _layer_body(nb, tl, n_steps, alpha, conv_steps, x_ref, *refs)


def _layer_body(nb, tl, n_steps, alpha, conv_steps,
                x_ref, h0re_ref, h0im_ref, cbuf0_ref, win_ref, bin_ref, l2re_ref, l2im_ref,
                sin_ref, sout_ref, sdir_ref, dskip_ref, wglu_ref, bglu_ref, wdw_ref, bdw_ref,
                gcl_ref, bcl_ref, wpw_ref, bpw_ref, wout_ref, gpost_ref, bpost_ref,
                y_ref, hre_ref, him_ref, cbuf_ref,
                z_ref, bu_ref, slab_ref, full_ref, mix_ref):
    rows = nb * tl
    hist = (CONV_K - 1) * nb
    pitch = _slab_pitch(tl)
    nsub = nb // SUBLANES
    c0 = 2 * W_SSM

    def init_state():
        if h0re_ref is None:
            hre_ref[...] = jnp.zeros_like(hre_ref)
            him_ref[...] = jnp.zeros_like(him_ref)
            full_ref[0:hist, :] = jnp.zeros((hist, W_CONV), F32)
        else:
            hre_ref[...] = h0re_ref[...]
            him_ref[...] = h0im_ref[...]
            full_ref[0:hist, :] = cbuf0_ref[...]

    if n_steps > 1:
        pl.when(pl.program_id(0) == 0)(init_state)
    else:
        init_state()

    half = rows // 2

    def s5_block(k):
        cs = slice(LANES * k, LANES * (k + 1))
        ss = slice(BLOCK_STATE * k, BLOCK_STATE * (k + 1))
        pieces = {(t, m): slab_ref[k, _tb_rows(m, t, pitch), :]
                  for t in range(tl) for m in range(nsub)}
        u_pair = jnp.concatenate(
            [jnp.concatenate([pieces[2 * j, m], pieces[2 * j + 1, m]], axis=1)
             for j in range(tl // 2) for m in range(nsub)], axis=0)
        u_bf = u_pair.astype(BF16)
        bu_ref[0:nb, 0:BLOCK_STATE] = hre_ref[:, ss]
        bu_ref[0:nb, BLOCK_STATE:2 * BLOCK_STATE] = him_ref[:, ss]
        bu_ref[nb:nb + half, :] = jnp.dot(u_bf, sin_ref[k], preferred_element_type=F32)
        lr = l2re_ref[:, ss]
        li = l2im_ref[:, ss]
        for m in range(nsub):
            rs = slice(SUBLANES * m, SUBLANES * (m + 1))
            hr, hi = bu_ref[rs, 0:BLOCK_STATE], bu_ref[rs, BLOCK_STATE:2 * BLOCK_STATE]
            for j in range(tl // 2):
                row = slice((j + 1) * nb + SUBLANES * m, (j + 1) * nb + SUBLANES * (m + 1))
                br = bu_ref[row, 0:BLOCK_STATE]
                bi = bu_ref[row, BLOCK_STATE:2 * BLOCK_STATE]
                hr, hi = lr * hr - li * hi + br, lr * hi + li * hr + bi
                bu_ref[row, 0:BLOCK_STATE] = hr
                bu_ref[row, BLOCK_STATE:2 * BLOCK_STATE] = hi
            hre_ref[rs, ss] = hr
            him_ref[rs, ss] = hi
        dsk = dskip_ref[:, cs]
        y_pair = (jnp.dot(bu_ref[0:half, :].astype(BF16), sout_ref[k], preferred_element_type=F32)
                  + jnp.dot(u_bf, sdir_ref[k], preferred_element_type=F32)
                  + u_pair * jnp.concatenate([dsk, dsk], axis=1))
        for j in range(tl // 2):
            for m in range(nsub):
                r0 = j * nb + SUBLANES * m
                slab_ref[k, _tb_rows(m, 2 * j, pitch), :] = y_pair[r0:r0 + SUBLANES, 0:LANES]
                slab_ref[k, _tb_rows(m, 2 * j + 1, pitch), :] = (
                    y_pair[r0:r0 + SUBLANES, LANES:2 * LANES])

    xb = x_ref[...].reshape(rows, D_MODEL).astype(BF16)

    def in_proj(lo, hi):
        z_ref[:, lo:hi] = (jnp.dot(xb, win_ref[:, lo:hi], preferred_element_type=F32)
                           + bin_ref[:, lo:hi])

    def s5_branch():
        ys = jnp.concatenate([_load_bt(slab_ref, j, nb, tl, pitch) for j in range(SSM_TILES)],
                             axis=1)
        sg = jax.nn.gelu(ys)
        in_proj(W_SSM, c0)
        glu = jnp.dot(sg.astype(BF16), wglu_ref[...], preferred_element_type=F32) + bglu_ref[...]
        s = sg * jax.nn.sigmoid(glu) * jax.nn.silu(z_ref[:, W_SSM:c0])
        mix_ref[:, 0:W_SSM] = s.astype(BF16)

    cw = 2 * LANES
    for h in range(CONV_TILES // 2):
        a_lo, b_lo = c0 + h * cw, c0 + W_CONV + h * cw
        in_proj(a_lo, a_lo + cw)
        in_proj(b_lo, b_lo + cw)
        v = z_ref[:, a_lo:a_lo + cw] * jax.nn.sigmoid(z_ref[:, b_lo:b_lo + cw])
        for j in range(2):
            _store_bt(slab_ref, SSM_TILES + 2 * h + j, v[:, LANES * j:LANES * (j + 1)],
                      nb, tl, pitch)
        if h == 0:
            in_proj(0, W_SSM)
            for j in range(SSM_TILES):
                _store_bt(slab_ref, j, z_ref[:, LANES * j:LANES * (j + 1)], nb, tl, pitch)
            for k in range(N_BLOCKS):
                s5_block(k)
            s5_branch()

    for c in range(CONV_TILES):
        lanes = slice(LANES * c, LANES * (c + 1))
        full_ref[hist:hist + rows, lanes] = _gather_tb(slab_ref, SSM_TILES + c, nb, tl, pitch)
    for c in range(CONV_TILES):
        lanes = slice(LANES * c, LANES * (c + 1))
        w = [wdw_ref[tap, :, lanes] for tap in range(CONV_K)]
        for m in range(nsub):
            for t0 in range(0, tl, conv_steps):
                base = t0 * nb + SUBLANES * m
                accs = [None] * conv_steps
                for j in range(conv_steps + CONV_K - 1):
                    xj = full_ref[base + j * nb:base + j * nb + SUBLANES, lanes]
                    for r in range(conv_steps):
                        tap = j - r
                        if 0 <= tap < CONV_K:
                            term = w[tap] * xj
                            accs[r] = term if accs[r] is None else accs[r] + term
                for r in range(conv_steps):
                    slab_ref[SSM_TILES + c, _tb_rows(m, t0 + r, pitch), :] = accs[r]
    cbuf_ref[...] = full_ref[rows:rows + hist, :]
    if n_steps > 1:
        full_ref[0:hist, :] = full_ref[rows:rows + hist, :]
    cv = jnp.concatenate([_load_bt(slab_ref, SSM_TILES + j, nb, tl, pitch)
                          for j in range(CONV_TILES)], axis=1) + bdw_ref[...]
    act = jax.nn.silu(_layernorm(cv, gcl_ref[...], bcl_ref[...]))
    in_proj(c0 + 2 * W_CONV, IN_COLS)
    cpw = jnp.dot(act.astype(BF16), wpw_ref[...], preferred_element_type=F32) + bpw_ref[...]
    cg = cpw * jax.nn.silu(z_ref[:, c0 + 2 * W_CONV:IN_COLS])
    mix_ref[:, W_SSM:W_SSM + W_CONV] = cg.astype(BF16)

    x2 = x_ref[...].reshape(rows, D_MODEL)
    n_chunks = max(1, rows // OUT_CHUNK_ROWS)
    rc = rows // n_chunks
    for r in range(n_chunks):
        rsl = slice(r * rc, (r + 1) * rc)
        mix = jnp.dot(mix_ref[rsl, :], wout_ref[...], preferred_element_type=F32)
        out = _layernorm(alpha * x2[rsl] + mix, gpost_ref[...], bpost_ref[...])
        if len(y_ref.shape) == 3:
            bsl = slice(r * (nb // n_chunks), (r + 1) * (nb // n_chunks))
            y_ref[bsl] = out.reshape(nb // n_chunks, tl, D_MODEL)
        else:
            y_ref[rsl, :] = out


def _layer_call(x, state, consts, *, nb, tl, n_steps, alpha, conv_steps, name):
    rows = nb * tl
    hist = (CONV_K - 1) * nb
    pitch = _slab_pitch(tl)
    assert nb % SUBLANES == 0 and tl % conv_steps == 0 and (nb * pitch) % SUBLANES == 0
    assert tl % 2 == 0 and (n_steps == 1 or (tl >= CONV_K - 1 and tl % SUBLANES == 0))
    out_shape = (
        jax.ShapeDtypeStruct(x.shape, F32),
        jax.ShapeDtypeStruct((nb, N_STATE_ALL), F32),
        jax.ShapeDtypeStruct((nb, N_STATE_ALL), F32),
        jax.ShapeDtypeStruct((hist, W_CONV), F32),
    )
    scratch = [
        pltpu.VMEM((rows, IN_COLS), F32),
        pltpu.VMEM((rows // 2 + nb, 2 * BLOCK_STATE), F32),
        pltpu.VMEM((SSM_TILES + CONV_TILES, nb * pitch, LANES), F32),
        pltpu.VMEM((hist + rows, W_CONV), F32),
        pltpu.VMEM((rows, W_SSM + W_CONV), BF16),
    ]
    operands = (x,) + (() if state is None else tuple(state)) + tuple(consts)
    kernel = functools.partial(_layer_kernel, nb, tl, n_steps, alpha, conv_steps, state is None)
    if n_steps == 1:
        assert x.shape == (nb, tl, D_MODEL)
        return pl.pallas_call(
            kernel, out_shape=out_shape, scratch_shapes=scratch, name=name,
            compiler_params=pltpu.CompilerParams(vmem_limit_bytes=VMEM_LIMIT_BYTES),
        )(*operands)

    assert x.shape == (nb, tl * n_steps, D_MODEL)

    def whole(a):
        zeros = (0,) * a.ndim
        return pl.BlockSpec(a.shape, lambda i: zeros, pipeline_mode=pl.Buffered(1))

    x_spec = pl.BlockSpec((nb, tl, D_MODEL), lambda i: (0, i, 0))
    in_specs = [x_spec] + [whole(a) for a in operands[1:]]
    out_specs = (
        x_spec,
        pl.BlockSpec((nb, N_STATE_ALL), lambda i: (0, 0)),
        pl.BlockSpec((nb, N_STATE_ALL), lambda i: (0, 0)),
        pl.BlockSpec((hist, W_CONV), lambda i: (0, 0)),
    )
    return pl.pallas_call(
        kernel, out_shape=out_shape, grid=(n_steps,), in_specs=in_specs, out_specs=out_specs,
        scratch_shapes=scratch, name=name,
        compiler_params=pltpu.CompilerParams(dimension_semantics=("arbitrary",),
                                             vmem_limit_bytes=VMEM_LIMIT_BYTES),
    )(*operands)


def _prep_call(lam_re, lam_im, log_dt, b_re, b_im, c_re, c_im):
    gp = N_GROUPS * GROUP_P
    ldt_n = jnp.broadcast_to(log_dt[:, None], (N_GROUPS, N_STATE))
    lam = jnp.stack([lam_re, lam_im, ldt_n])
    lamrow = jnp.stack([a.reshape(1, N_STATE_ALL) for a in (lam_re, lam_im, ldt_n)])
    bt = jnp.stack([b.transpose(0, 2, 1).reshape(gp, N_STATE) for b in (b_re, b_im)])
    cg = jnp.stack([c.reshape(gp, N_STATE) for c in (c_re, c_im)])
    out_shape = (
        jax.ShapeDtypeStruct((N_BLOCKS, 2 * LANES, 2 * BLOCK_STATE), BF16),
        jax.ShapeDtypeStruct((N_BLOCKS, 2 * BLOCK_STATE, 2 * LANES), BF16),
        jax.ShapeDtypeStruct((N_BLOCKS, 2 * LANES, 2 * LANES), BF16),
        jax.ShapeDtypeStruct((SUBLANES, N_STATE_ALL), F32),
        jax.ShapeDtypeStruct((SUBLANES, N_STATE_ALL), F32),
    )
    return pl.pallas_call(_prep_kernel, out_shape=out_shape, name="s5_prep")(lam, lamrow, bt, cg)


def kernel(x_prompt, x_sample, state_ssm_re, state_ssm_im, state_conv, w_in, b_in, lam_re, lam_im, log_dt, b_re, b_im, c_re, c_im, d_skip, w_glu, b_glu, w_dw, b_dw, g_conv_ln, b_conv_ln, w_pw2, b_pw2, w_out, g_post, b_post):
    depth = w_in.shape[0]
    alpha = (2.0 * depth) ** 0.25
    bsz, seq, _ = x_prompt.shape
    dbsz, dseq, _ = x_sample.shape
    tl_prompt = 128

    hp = x_prompt
    hs = x_sample
    outs = {k: [] for k in ("re_p", "im_p", "cv_p", "re_s", "im_s", "cv_s")}
    row = lambda a: a.reshape(1, -1)
    for layer in range(depth):
        s_in, s_out, s_dir, l2re, l2im = _prep_call(
            lam_re[layer], lam_im[layer], log_dt[layer], b_re[layer], b_im[layer],
            c_re[layer], c_im[layer])
        wdw8 = jnp.broadcast_to(w_dw[layer][:, None, :], (CONV_K, SUBLANES, W_CONV))
        consts = (w_in[layer].astype(BF16), row(b_in[layer]), l2re, l2im, s_in, s_out, s_dir,
                  row(d_skip[layer]), w_glu[layer].astype(BF16), row(b_glu[layer]), wdw8,
                  row(b_dw[layer]), row(g_conv_ln[layer]), row(b_conv_ln[layer]),
                  w_pw2[layer].astype(BF16), row(b_pw2[layer]), w_out[layer].astype(BF16),
                  row(g_post[layer]), row(b_post[layer]))

        hp, re_p, im_p, cv_p = _layer_call(
            hp, None, consts, nb=bsz, tl=tl_prompt,
            n_steps=seq // tl_prompt, alpha=alpha, conv_steps=8, name="layer_prompt")

        h0re = state_ssm_re[layer].reshape(dbsz, N_STATE_ALL)
        h0im = state_ssm_im[layer].reshape(dbsz, N_STATE_ALL)
        cbuf0 = state_conv[layer].transpose(1, 0, 2).reshape((CONV_K - 1) * dbsz, W_CONV)
        hs, re_s, im_s, cv_s = _layer_call(
            hs, (h0re, h0im, cbuf0), consts, nb=dbsz, tl=dseq, n_steps=1, alpha=alpha,
            conv_steps=dseq, name="layer_sample")

        unrow = lambda a, n: a.reshape(CONV_K - 1, n, W_CONV).transpose(1, 0, 2)
        outs["re_p"].append(re_p.reshape(bsz, N_GROUPS, N_STATE))
        outs["im_p"].append(im_p.reshape(bsz, N_GROUPS, N_STATE))
        outs["cv_p"].append(unrow(cv_p, bsz))
        outs["re_s"].append(re_s.reshape(dbsz, N_GROUPS, N_STATE))
        outs["im_s"].append(im_s.reshape(dbsz, N_GROUPS, N_STATE))
        outs["cv_s"].append(unrow(cv_s, dbsz))

    return (hp, hs,
            jnp.stack(outs["re_p"]), jnp.stack(outs["im_p"]), jnp.stack(outs["cv_p"]),
            jnp.stack(outs["re_s"]), jnp.stack(outs["im_s"]), jnp.stack(outs["cv_s"]))
```

```python
import functools

import jax
import jax.numpy as jnp
from jax import lax
from jax.experimental import pallas as pl
from jax.experimental.pallas import tpu as pltpu

D_MODEL = 1024
W_SSM = 512
W_CONV = 512
GROUP_P = 16
N_GROUPS = 32
N_STATE = 64
CONV_K = 31
IN_COLS = 2 * W_SSM + 3 * W_CONV
LN_EPS = 1e-5

LANES = 128
SUBLANES = 8
GROUPS_PER_BLOCK = LANES // GROUP_P
N_BLOCKS = N_GROUPS // GROUPS_PER_BLOCK
BLOCK_STATE = GROUPS_PER_BLOCK * N_STATE
N_STATE_ALL = N_GROUPS * N_STATE
SSM_TILES = W_SSM // LANES
CONV_TILES = W_CONV // LANES
OUT_CHUNK_ROWS = 256
VMEM_LIMIT_BYTES = 58 * 1024 * 1024

F32 = jnp.float32
BF16 = jnp.bfloat16


def _layernorm(x, g, b):
    mu = jnp.mean(x, axis=-1, keepdims=True)
    xc = x - mu
    var = jnp.mean(xc * xc, axis=-1, keepdims=True)
    return xc * lax.rsqrt(var + LN_EPS) * g + b


def _slab_pitch(tl):
    if tl % SUBLANES:
        return tl
    pitch = tl
    while (pitch // SUBLANES) % 2 == 0:
        pitch += SUBLANES
    return pitch


def _tb_rows(m, t, pitch):
    return pl.ds(SUBLANES * m * pitch + t, SUBLANES, stride=pitch)


def _store_bt(slab_ref, j, val, nb, tl, pitch):
    if pitch == tl:
        slab_ref[j] = val
    else:
        for b in range(nb):
            slab_ref[j, b * pitch:b * pitch + tl, :] = val[b * tl:(b + 1) * tl]


def _load_bt(slab_ref, j, nb, tl, pitch):
    if pitch == tl:
        return slab_ref[j]
    return jnp.concatenate([slab_ref[j, b * pitch:b * pitch + tl, :] for b in range(nb)], axis=0)


def _gather_tb(slab_ref, j, nb, tl, pitch):
    pieces = [slab_ref[j, _tb_rows(m, t, pitch), :]
              for t in range(tl) for m in range(nb // SUBLANES)]
    return jnp.concatenate(pieces, axis=0)


def _prep_kernel(lre_ref, lim_ref, ldt_ref, lre_row_ref, lim_row_ref, ldt_row_ref,
                 btr_ref, bti_ref, cgr_ref, cgi_ref,
                 sin_ref, sout_ref, sdir_ref, l2re_ref, l2im_ref):
    gp = N_GROUPS * GROUP_P
    abt = (((1,), (1,)), ((), ()))

    def zoh(lr, li, ldt):
        dt = jnp.exp(ldt)
        mag = jnp.exp(lr * dt)
        ang = li * dt
        return mag * jnp.cos(ang), mag * jnp.sin(ang)

    def per_channel(a):
        wide = jnp.broadcast_to(a[:, None, :], (N_GROUPS, GROUP_P, a.shape[-1]))
        return wide.reshape(gp, a.shape[-1])

    def per_block(a):
        return jnp.tile(a, (1, GROUPS_PER_BLOCK))

    lr, li = lre_ref[...], lim_ref[...]
    ar, ai = zoh(lr, li, ldt_ref[...])
    nr, ni = ar - 1.0, ai
    den = lr * lr + li * li
    cr = per_channel(per_block((nr * lr + ni * li) / den))
    ci = per_channel(per_block((ni * lr - nr * li) / den))
    ar, ai = per_channel(per_block(ar)), per_channel(per_block(ai))
    shape = ar.shape
    row_g = lax.broadcasted_iota(jnp.int32, shape, 0) // GROUP_P % GROUPS_PER_BLOCK
    col_g = lax.broadcasted_iota(jnp.int32, shape, 1) // N_STATE
    keep = row_g == col_g
    br, bi = per_block(btr_ref[...]), per_block(bti_ref[...])
    bbr = jnp.where(keep, cr * br - ci * bi, 0.0)
    bbi = jnp.where(keep, cr * bi + ci * br, 0.0)
    lbr = ar * bbr - ai * bbi
    lbi = ar * bbi + ai * bbr
    c0r = jnp.where(keep, per_block(cgr_ref[...]), 0.0)
    c0i = jnp.where(keep, per_block(cgi_ref[...]), 0.0)
    c1r = ar * c0r - ai * c0i
    c1i = ar * c0i + ai * c0r
    c2r = ar * c1r - ai * c1i
    c2i = ar * c1i + ai * c1r

    for k in range(N_BLOCKS):
        rs = slice(LANES * k, LANES * (k + 1))
        b_k = jnp.concatenate([bbr[rs], bbi[rs]], axis=1)
        lb_k = jnp.concatenate([lbr[rs], lbi[rs]], axis=1)
        sin_ref[k, 0:LANES, :] = lb_k.astype(BF16)
        sin_ref[k, LANES:2 * LANES, :] = b_k.astype(BF16)
        c_k = jnp.concatenate([c0r[rs], -c0i[rs]], axis=1)
        c12_k = jnp.concatenate([jnp.concatenate([c1r[rs], -c1i[rs]], axis=1),
                                 jnp.concatenate([c2r[rs], -c2i[rs]], axis=1)], axis=0)
        sout_ref[k] = c12_k.T.astype(BF16)
        c_bf = c_k.astype(BF16)
        cb = lax.dot_general(b_k.astype(BF16), c_bf, abt, preferred_element_type=F32)
        clb = lax.dot_general(lb_k.astype(BF16), c_bf, abt, preferred_element_type=F32)
        sdir_ref[k, 0:LANES, 0:LANES] = cb.astype(BF16)
        sdir_ref[k, 0:LANES, LANES:2 * LANES] = clb.astype(BF16)
        sdir_ref[k, LANES:2 * LANES, 0:LANES] = jnp.zeros((LANES, LANES), BF16)
        sdir_ref[k, LANES:2 * LANES, LANES:2 * LANES] = cb.astype(BF16)

    a_r, a_i = zoh(lre_row_ref[...], lim_row_ref[...], ldt_row_ref[...])
    l2re_ref[...] = jnp.broadcast_to(a_r * a_r - a_i * a_i, l2re_ref.shape)
    l2im_ref[...] = jnp.broadcast_to(2.0 * a_r * a_i, l2im_ref.shape)


def _layer_kernel(nb, tl, n_steps, alpha, conv_steps, zero_state, x_ref, *refs):
    if zero_state:
        refs = (None, None, None) + refs
    _layer_body(nb, tl, n_steps, alpha, conv_steps, x_ref, *refs)


def _layer_body(nb, tl, n_steps, alpha, conv_steps,
                x_ref, h0re_ref, h0im_ref, cbuf0_ref, win_ref, bin_ref, l2re_ref, l2im_ref,
                sin_ref, sout_ref, sdir_ref, dskip_ref, wglu_ref, bglu_ref, wdw_ref, bdw_ref,
                gcl_ref, bcl_ref, wpw_ref, bpw_ref, wout_ref, gpost_ref, bpost_ref,
                y_ref, hre_ref, him_ref, cbuf_ref,
                z_ref, bu_ref, slab_ref, full_ref, mix_ref):
    rows = nb * tl
    hist = (CONV_K - 1) * nb
    pitch = _slab_pitch(tl)
    nsub = nb // SUBLANES
    c0 = 2 * W_SSM

    def init_state():
        if h0re_ref is None:
            hre_ref[...] = jnp.zeros_like(hre_ref)
            him_ref[...] = jnp.zeros_like(him_ref)
            full_ref[0:hist, :] = jnp.zeros((hist, W_CONV), F32)
        else:
            hre_ref[...] = h0re_ref[...]
            him_ref[...] = h0im_ref[...]
            full_ref[0:hist, :] = cbuf0_ref[...]

    if n_steps > 1:
        pl.when(pl.program_id(0) == 0)(init_state)
    else:
        init_state()

    half = rows // 2

    def s5_block(k):
        cs = slice(LANES * k, LANES * (k + 1))
        ss = slice(BLOCK_STATE * k, BLOCK_STATE * (k + 1))
        pieces = {(t, m): slab_ref[k, _tb_rows(m, t, pitch), :]
                  for t in range(tl) for m in range(nsub)}
        u_pair = jnp.concatenate(
            [jnp.concatenate([pieces[2 * j, m], pieces[2 * j + 1, m]], axis=1)
             for j in range(tl // 2) for m in range(nsub)], axis=0)
        u_bf = u_pair.astype(BF16)
        bu_ref[0:nb, 0:BLOCK_STATE] = hre_ref[:, ss]
        bu_ref[0:nb, BLOCK_STATE:2 * BLOCK_STATE] = him_ref[:, ss]
        bu_ref[nb:nb + half, :] = jnp.dot(u_bf, sin_ref[k], preferred_element_type=F32)
        lr = l2re_ref[:, ss]
        li = l2im_ref[:, ss]
        for m in range(nsub):
            rs = slice(SUBLANES * m, SUBLANES * (m + 1))
            hr, hi = bu_ref[rs, 0:BLOCK_STATE], bu_ref[rs, BLOCK_STATE:2 * BLOCK_STATE]
            for j in range(tl // 2):
                row = slice((j + 1) * nb + SUBLANES * m, (j + 1) * nb + SUBLANES * (m + 1))
                br = bu_ref[row, 0:BLOCK_STATE]
                bi = bu_ref[row, BLOCK_STATE:2 * BLOCK_STATE]
                hr, hi = lr * hr - li * hi + br, lr * hi + li * hr + bi
                bu_ref[row, 0:BLOCK_STATE] = hr
                bu_ref[row, BLOCK_STATE:2 * BLOCK_STATE] = hi
            hre_ref[rs, ss] = hr
            him_ref[rs, ss] = hi
        dsk = dskip_ref[:, cs]
        y_pair = (jnp.dot(bu_ref[0:half, :].astype(BF16), sout_ref[k], preferred_element_type=F32)
                  + jnp.dot(u_bf, sdir_ref[k], preferred_element_type=F32)
                  + u_pair * jnp.concatenate([dsk, dsk], axis=1))
        for j in range(tl // 2):
            for m in range(nsub):
                r0 = j * nb + SUBLANES * m
                slab_ref[k, _tb_rows(m, 2 * j, pitch), :] = y_pair[r0:r0 + SUBLANES, 0:LANES]
                slab_ref[k, _tb_rows(m, 2 * j + 1, pitch), :] = (
                    y_pair[r0:r0 + SUBLANES, LANES:2 * LANES])

    xb = x_ref[...].reshape(rows, D_MODEL).astype(BF16)

    def in_proj(lo, hi):
        z_ref[:, lo:hi] = (jnp.dot(xb, win_ref[:, lo:hi], preferred_element_type=F32)
                           + bin_ref[:, lo:hi])

    def s5_branch():
        ys = jnp.concatenate([_load_bt(slab_ref, j, nb, tl, pitch) for j in range(SSM_TILES)],
                             axis=1)
        sg = jax.nn.gelu(ys)
        in_proj(W_SSM, c0)
        glu = jnp.dot(sg.astype(BF16), wglu_ref[...], preferred_element_type=F32) + bglu_ref[...]
        s = sg * jax.nn.sigmoid(glu) * jax.nn.silu(z_ref[:, W_SSM:c0])
        mix_ref[:, 0:W_SSM] = s.astype(BF16)

    cw = 2 * LANES
    for h in range(CONV_TILES // 2):
        a_lo, b_lo = c0 + h * cw, c0 + W_CONV + h * cw
        in_proj(a_lo, a_lo + cw)
        in_proj(b_lo, b_lo + cw)
        v = z_ref[:, a_lo:a_lo + cw] * jax.nn.sigmoid(z_ref[:, b_lo:b_lo + cw])
        for j in range(2):
            _store_bt(slab_ref, SSM_TILES + 2 * h + j, v[:, LANES * j:LANES * (j + 1)],
                      nb, tl, pitch)
        if h == 0:
            in_proj(0, W_SSM)
            for j in range(SSM_TILES):
                _store_bt(slab_ref, j, z_ref[:, LANES * j:LANES * (j + 1)], nb, tl, pitch)
            for k in range(N_BLOCKS):
                s5_block(k)
            s5_branch()

    for c in range(CONV_TILES):
        lanes = slice(LANES * c, LANES * (c + 1))
        full_ref[hist:hist + rows, lanes] = _gather_tb(slab_ref, SSM_TILES + c, nb, tl, pitch)
    for c in range(CONV_TILES):
        lanes = slice(LANES * c, LANES * (c + 1))
        w = [wdw_ref[tap, :, lanes] for tap in range(CONV_K)]
        for m in range(nsub):
            for t0 in range(0, tl, conv_steps):
                base = t0 * nb + SUBLANES * m
                accs = [None] * conv_steps
                for j in range(conv_steps + CONV_K - 1):
                    xj = full_ref[base + j * nb:base + j * nb + SUBLANES, lanes]
                    for r in range(conv_steps):
                        tap = j - r
                        if 0 <= tap < CONV_K:
                            term = w[tap] * xj
                            accs[r] = term if accs[r] is None else accs[r] + term
                for r in range(conv_steps):
                    slab_ref[SSM_TILES + c, _tb_rows(m, t0 + r, pitch), :] = accs[r]
    cbuf_ref[...] = full_ref[rows:rows + hist, :]
    if n_steps > 1:
        full_ref[0:hist, :] = full_ref[rows:rows + hist, :]
    cv = jnp.concatenate([_load_bt(slab_ref, SSM_TILES + j, nb, tl, pitch)
                          for j in range(CONV_TILES)], axis=1) + bdw_ref[...]
    act = jax.nn.silu(_layernorm(cv, gcl_ref[...], bcl_ref[...]))
    in_proj(c0 + 2 * W_CONV, IN_COLS)
    cpw = jnp.dot(act.astype(BF16), wpw_ref[...], preferred_element_type=F32) + bpw_ref[...]
    cg = cpw * jax.nn.silu(z_ref[:, c0 + 2 * W_CONV:IN_COLS])
    mix_ref[:, W_SSM:W_SSM + W_CONV] = cg.astype(BF16)

    x2 = x_ref[...].reshape(rows, D_MODEL)
    n_chunks = max(1, rows // OUT_CHUNK_ROWS)
    rc = rows // n_chunks
    for r in range(n_chunks):
        rsl = slice(r * rc, (r + 1) * rc)
        mix = jnp.dot(mix_ref[rsl, :], wout_ref[...], preferred_element_type=F32)
        out = _layernorm(alpha * x2[rsl] + mix, gpost_ref[...], bpost_ref[...])
        if len(y_ref.shape) == 3:
            bsl = slice(r * (nb // n_chunks), (r + 1) * (nb // n_chunks))
            y_ref[bsl] = out.reshape(nb // n_chunks, tl, D_MODEL)
        else:
            y_ref[rsl, :] = out


def _layer_call(x, state, consts, *, nb, tl, n_steps, alpha, conv_steps, name):
    rows = nb * tl
    hist = (CONV_K - 1) * nb
    pitch = _slab_pitch(tl)
    assert nb % SUBLANES == 0 and tl % conv_steps == 0 and (nb * pitch) % SUBLANES == 0
    assert tl % 2 == 0 and (n_steps == 1 or (tl >= CONV_K - 1 and tl % SUBLANES == 0))
    out_shape = (
        jax.ShapeDtypeStruct(x.shape, F32),
        jax.ShapeDtypeStruct((nb, N_STATE_ALL), F32),
        jax.ShapeDtypeStruct((nb, N_STATE_ALL), F32),
        jax.ShapeDtypeStruct((hist, W_CONV), F32),
    )
    scratch = [
        pltpu.VMEM((rows, IN_COLS), F32),
        pltpu.VMEM((rows // 2 + nb, 2 * BLOCK_STATE), F32),
        pltpu.VMEM((SSM_TILES + CONV_TILES, nb * pitch, LANES), F32),
        pltpu.VMEM((hist + rows, W_CONV), F32),
        pltpu.VMEM((rows, W_SSM + W_CONV), BF16),
    ]
    operands = (x,) + (() if state is None else tuple(state)) + tuple(consts)
    kernel = functools.partial(_layer_kernel, nb, tl, n_steps, alpha, conv_steps, state is None)
    if n_steps == 1:
        assert x.shape == (nb, tl, D_MODEL)
        return pl.pallas_call(
            kernel, out_shape=out_shape, scratch_shapes=scratch, name=name,
            compiler_params=pltpu.CompilerParams(vmem_limit_bytes=VMEM_LIMIT_BYTES),
        )(*operands)

    assert x.shape == (nb, tl * n_steps, D_MODEL)

    def whole(a):
        zeros = (0,) * a.ndim
        return pl.BlockSpec(a.shape, lambda i: zeros, pipeline_mode=pl.Buffered(1))

    x_spec = pl.BlockSpec((nb, tl, D_MODEL), lambda i: (0, i, 0))
    in_specs = [x_spec] + [whole(a) for a in operands[1:]]
    out_specs = (
        x_spec,
        pl.BlockSpec((nb, N_STATE_ALL), lambda i: (0, 0)),
        pl.BlockSpec((nb, N_STATE_ALL), lambda i: (0, 0)),
        pl.BlockSpec((hist, W_CONV), lambda i: (0, 0)),
    )
    return pl.pallas_call(
        kernel, out_shape=out_shape, grid=(n_steps,), in_specs=in_specs, out_specs=out_specs,
        scratch_shapes=scratch, name=name,
        compiler_params=pltpu.CompilerParams(dimension_semantics=("arbitrary",),
                                             vmem_limit_bytes=VMEM_LIMIT_BYTES),
    )(*operands)


def _prep_call(lam_re, lam_im, log_dt, b_re, b_im, c_re, c_im):
    gp = N_GROUPS * GROUP_P
    ldt_n = jnp.broadcast_to(log_dt[:, None], (N_GROUPS, N_STATE))
    lam = (lam_re, lam_im, ldt_n)
    lamrow = tuple(a.reshape(1, N_STATE_ALL) for a in lam)
    bt = tuple(b.transpose(0, 2, 1).reshape(gp, N_STATE) for b in (b_re, b_im))
    cg = tuple(c.reshape(gp, N_STATE) for c in (c_re, c_im))
    out_shape = (
        jax.ShapeDtypeStruct((N_BLOCKS, 2 * LANES, 2 * BLOCK_STATE), BF16),
        jax.ShapeDtypeStruct((N_BLOCKS, 2 * BLOCK_STATE, 2 * LANES), BF16),
        jax.ShapeDtypeStruct((N_BLOCKS, 2 * LANES, 2 * LANES), BF16),
        jax.ShapeDtypeStruct((SUBLANES, N_STATE_ALL), F32),
        jax.ShapeDtypeStruct((SUBLANES, N_STATE_ALL), F32),
    )
    return pl.pallas_call(_prep_kernel, out_shape=out_shape, name="s5_prep")(
        *lam, *lamrow, *bt, *cg)


def kernel(x_prompt, x_sample, state_ssm_re, state_ssm_im, state_conv, w_in, b_in, lam_re, lam_im, log_dt, b_re, b_im, c_re, c_im, d_skip, w_glu, b_glu, w_dw, b_dw, g_conv_ln, b_conv_ln, w_pw2, b_pw2, w_out, g_post, b_post):
    depth = w_in.shape[0]
    alpha = (2.0 * depth) ** 0.25
    bsz, seq, _ = x_prompt.shape
    dbsz, dseq, _ = x_sample.shape
    tl_prompt = 128

    hp = x_prompt
    hs = x_sample
    outs = {k: [] for k in ("re_p", "im_p", "cv_p", "re_s", "im_s", "cv_s")}
    row = lambda a: a.reshape(1, -1)
    for layer in range(depth):
        s_in, s_out, s_dir, l2re, l2im = _prep_call(
            lam_re[layer], lam_im[layer], log_dt[layer], b_re[layer], b_im[layer],
            c_re[layer], c_im[layer])
        wdw8 = jnp.broadcast_to(w_dw[layer][:, None, :], (CONV_K, SUBLANES, W_CONV))
        consts = (w_in[layer].astype(BF16), row(b_in[layer]), l2re, l2im, s_in, s_out, s_dir,
                  row(d_skip[layer]), w_glu[layer].astype(BF16), row(b_glu[layer]), wdw8,
                  row(b_dw[layer]), row(g_conv_ln[layer]), row(b_conv_ln[layer]),
                  w_pw2[layer].astype(BF16), row(b_pw2[layer]), w_out[layer].astype(BF16),
                  row(g_post[layer]), row(b_post[layer]))

        hp, re_p, im_p, cv_p = _layer_call(
            hp, None, consts, nb=bsz, tl=tl_prompt,
            n_steps=seq // tl_prompt, alpha=alpha, conv_steps=8, name="layer_prompt")

        h0re = state_ssm_re[layer].reshape(dbsz, N_STATE_ALL)
        h0im = state_ssm_im[layer].reshape(dbsz, N_STATE_ALL)
        cbuf0 = state_conv[layer].transpose(1, 0, 2).reshape((CONV_K - 1) * dbsz, W_CONV)
        hs, re_s, im_s, cv_s = _layer_call(
            hs, (h0re, h0im, cbuf0), consts, nb=dbsz, tl=dseq, n_steps=1, alpha=alpha,
            conv_steps=dseq, name="layer_sample")

        unrow = lambda a, n: a.reshape(CONV_K - 1, n, W_CONV).transpose(1, 0, 2)
        outs["re_p"].append(re_p.reshape(bsz, N_GROUPS, N_STATE))
        outs["im_p"].append(im_p.reshape(bsz, N_GROUPS, N_STATE))
        outs["cv_p"].append(unrow(cv_p, bsz))
        outs["re_s"].append(re_s.reshape(dbsz, N_GROUPS, N_STATE))
        outs["im_s"].append(im_s.reshape(dbsz, N_GROUPS, N_STATE))
        outs["cv_s"].append(unrow(cv_s, dbsz))

    return (hp, hs,
            jnp.stack(outs["re_p"]), jnp.stack(outs["im_p"]), jnp.stack(outs["cv_p"]),
            jnp.stack(outs["re_s"]), jnp.stack(outs["im_s"]), jnp.stack(outs["cv_s"]))
```

```python
import functools

import jax
import jax.numpy as jnp
from jax import lax
from jax.experimental import pallas as pl
from jax.experimental.pallas import tpu as pltpu

D_MODEL = 1024
W_SSM = 512
W_CONV = 512
GROUP_P = 16
N_GROUPS = 32
N_STATE = 64
CONV_K = 31
IN_COLS = 2 * W_SSM + 3 * W_CONV
LN_EPS = 1e-5

LANES = 128
SUBLANES = 8
GROUPS_PER_BLOCK = LANES // GROUP_P
N_BLOCKS = N_GROUPS // GROUPS_PER_BLOCK
BLOCK_STATE = GROUPS_PER_BLOCK * N_STATE
N_STATE_ALL = N_GROUPS * N_STATE
SSM_TILES = W_SSM // LANES
CONV_TILES = W_CONV // LANES
OUT_CHUNK_ROWS = 256
VMEM_LIMIT_BYTES = 58 * 1024 * 1024

F32 = jnp.float32
BF16 = jnp.bfloat16


def _layernorm(x, g, b):
    mu = jnp.mean(x, axis=-1, keepdims=True)
    xc = x - mu
    var = jnp.mean(xc * xc, axis=-1, keepdims=True)
    return xc * lax.rsqrt(var + LN_EPS) * g + b


def _slab_pitch(tl):
    if tl % SUBLANES:
        return tl
    pitch = tl
    while (pitch // SUBLANES) % 2 == 0:
        pitch += SUBLANES
    return pitch


def _tb_rows(m, t, pitch):
    return pl.ds(SUBLANES * m * pitch + t, SUBLANES, stride=pitch)


def _store_bt(slab_ref, j, val, nb, tl, pitch):
    if pitch == tl:
        slab_ref[j] = val
    else:
        for b in range(nb):
            slab_ref[j, b * pitch:b * pitch + tl, :] = val[b * tl:(b + 1) * tl]


def _load_bt(slab_ref, j, nb, tl, pitch):
    if pitch == tl:
        return slab_ref[j]
    return jnp.concatenate([slab_ref[j, b * pitch:b * pitch + tl, :] for b in range(nb)], axis=0)


def _gather_tb(slab_ref, j, nb, tl, pitch):
    pieces = [slab_ref[j, _tb_rows(m, t, pitch), :]
              for t in range(tl) for m in range(nb // SUBLANES)]
    return jnp.concatenate(pieces, axis=0)


def _prep_kernel(lre_ref, lim_ref, ldt_ref, lre_row_ref, lim_row_ref, ldt_row_ref,
                 btr_ref, bti_ref, cgr_ref, cgi_ref,
                 sin_ref, sout_ref, sdir_ref, l2re_ref, l2im_ref):
    gp = N_GROUPS * GROUP_P
    abt = (((1,), (1,)), ((), ()))

    def zoh(lr, li, ldt):
        dt = jnp.exp(ldt)
        mag = jnp.exp(lr * dt)
        ang = li * dt
        return mag * jnp.cos(ang), mag * jnp.sin(ang)

    def per_channel(a):
        wide = jnp.broadcast_to(a[:, None, :], (N_GROUPS, GROUP_P, a.shape[-1]))
        return wide.reshape(gp, a.shape[-1])

    def per_block(a):
        return jnp.tile(a, (1, GROUPS_PER_BLOCK))

    lr, li = lre_ref[...], lim_ref[...]
    ar, ai = zoh(lr, li, ldt_ref[...])
    nr, ni = ar - 1.0, ai
    den = lr * lr + li * li
    cr = per_channel(per_block((nr * lr + ni * li) / den))
    ci = per_channel(per_block((ni * lr - nr * li) / den))
    ar, ai = per_channel(per_block(ar)), per_channel(per_block(ai))
    shape = ar.shape
    row_g = lax.broadcasted_iota(jnp.int32, shape, 0) // GROUP_P % GROUPS_PER_BLOCK
    col_g = lax.broadcasted_iota(jnp.int32, shape, 1) // N_STATE
    keep = row_g == col_g
    br, bi = per_block(btr_ref[...]), per_block(bti_ref[...])
    bbr = jnp.where(keep, cr * br - ci * bi, 0.0)
    bbi = jnp.where(keep, cr * bi + ci * br, 0.0)
    lbr = ar * bbr - ai * bbi
    lbi = ar * bbi + ai * bbr
    c0r = jnp.where(keep, per_block(cgr_ref[...]), 0.0)
    c0i = jnp.where(keep, per_block(cgi_ref[...]), 0.0)
    c1r = ar * c0r - ai * c0i
    c1i = ar * c0i + ai * c0r
    c2r = ar * c1r - ai * c1i
    c2i = ar * c1i + ai * c1r

    for k in range(N_BLOCKS):
        rs = slice(LANES * k, LANES * (k + 1))
        b_k = jnp.concatenate([bbr[rs], bbi[rs]], axis=1)
        lb_k = jnp.concatenate([lbr[rs], lbi[rs]], axis=1)
        sin_ref[k, 0:LANES, :] = lb_k.astype(BF16)
        sin_ref[k, LANES:2 * LANES, :] = b_k.astype(BF16)
        c_k = jnp.concatenate([c0r[rs], -c0i[rs]], axis=1)
        c12_k = jnp.concatenate([jnp.concatenate([c1r[rs], -c1i[rs]], axis=1),
                                 jnp.concatenate([c2r[rs], -c2i[rs]], axis=1)], axis=0)
        sout_ref[k] = c12_k.T.astype(BF16)
        c_bf = c_k.astype(BF16)
        cb = lax.dot_general(b_k.astype(BF16), c_bf, abt, preferred_element_type=F32)
        clb = lax.dot_general(lb_k.astype(BF16), c_bf, abt, preferred_element_type=F32)
        sdir_ref[k, 0:LANES, 0:LANES] = cb.astype(BF16)
        sdir_ref[k, 0:LANES, LANES:2 * LANES] = clb.astype(BF16)
        sdir_ref[k, LANES:2 * LANES, 0:LANES] = jnp.zeros((LANES, LANES), BF16)
        sdir_ref[k, LANES:2 * LANES, LANES:2 * LANES] = cb.astype(BF16)

    a_r, a_i = zoh(lre_row_ref[...], lim_row_ref[...], ldt_row_ref[...])
    l2re_ref[...] = jnp.broadcast_to(a_r * a_r - a_i * a_i, l2re_ref.shape)
    l2im_ref[...] = jnp.broadcast_to(2.0 * a_r * a_i, l2im_ref.shape)


def _layer_kernel(nb, tl, n_steps, alpha, conv_steps, zero_state, x_ref, *refs):
    if zero_state:
        refs = (None, None, None) + refs
    _layer_body(nb, tl, n_steps, alpha, conv_steps, x_ref, *refs)


def _layer_body(nb, tl, n_steps, alpha, conv_steps,
                x_ref, h0re_ref, h0im_ref, cbuf0_ref, win_ref, bin_ref, l2re_ref, l2im_ref,
                sin_ref, sout_ref, sdir_ref, dskip_ref, wglu_ref, bglu_ref, wdw_ref, bdw_ref,
                gcl_ref, bcl_ref, wpw_ref, bpw_ref, wout_ref, gpost_ref, bpost_ref,
                y_ref, hre_ref, him_ref, cbuf_ref,
                z_ref, bu_ref, slab_ref, full_ref, mix_ref, *state_scratch):
    state_t = bool(state_scratch)
    if state_t:
        hre_out_ref, him_out_ref = hre_ref, him_ref
        hre_ref, him_ref = state_scratch
    rows = nb * tl
    hist = (CONV_K - 1) * nb
    pitch = _slab_pitch(tl)
    nsub = nb // SUBLANES
    c0 = 2 * W_SSM

    def init_state():
        if h0re_ref is None:
            hre_ref[...] = jnp.zeros_like(hre_ref)
            him_ref[...] = jnp.zeros_like(him_ref)
            full_ref[0:hist, :] = jnp.zeros((hist, W_CONV), F32)
        else:
            hre_ref[...] = h0re_ref[...].T if state_t else h0re_ref[...]
            him_ref[...] = h0im_ref[...].T if state_t else h0im_ref[...]
            full_ref[0:hist, :] = cbuf0_ref[...]

    if n_steps > 1:
        pl.when(pl.program_id(0) == 0)(init_state)
    else:
        init_state()

    half = rows // 2

    def s5_block(k):
        cs = slice(LANES * k, LANES * (k + 1))
        ss = slice(BLOCK_STATE * k, BLOCK_STATE * (k + 1))
        pieces = {(t, m): slab_ref[k, _tb_rows(m, t, pitch), :]
                  for t in range(tl) for m in range(nsub)}
        u_pair = jnp.concatenate(
            [jnp.concatenate([pieces[2 * j, m], pieces[2 * j + 1, m]], axis=1)
             for j in range(tl // 2) for m in range(nsub)], axis=0)
        u_bf = u_pair.astype(BF16)
        bu_ref[0:nb, 0:BLOCK_STATE] = hre_ref[:, ss]
        bu_ref[0:nb, BLOCK_STATE:2 * BLOCK_STATE] = him_ref[:, ss]
        bu_ref[nb:nb + half, :] = jnp.dot(u_bf, sin_ref[k], preferred_element_type=F32)
        lr = l2re_ref[:, ss]
        li = l2im_ref[:, ss]
        for m in range(nsub):
            rs = slice(SUBLANES * m, SUBLANES * (m + 1))
            hr, hi = bu_ref[rs, 0:BLOCK_STATE], bu_ref[rs, BLOCK_STATE:2 * BLOCK_STATE]
            for j in range(tl // 2):
                row = slice((j + 1) * nb + SUBLANES * m, (j + 1) * nb + SUBLANES * (m + 1))
                br = bu_ref[row, 0:BLOCK_STATE]
                bi = bu_ref[row, BLOCK_STATE:2 * BLOCK_STATE]
                hr, hi = lr * hr - li * hi + br, lr * hi + li * hr + bi
                bu_ref[row, 0:BLOCK_STATE] = hr
                bu_ref[row, BLOCK_STATE:2 * BLOCK_STATE] = hi
            hre_ref[rs, ss] = hr
            him_ref[rs, ss] = hi
        dsk = dskip_ref[:, cs]
        y_pair = (jnp.dot(bu_ref[0:half, :].astype(BF16), sout_ref[k], preferred_element_type=F32)
                  + jnp.dot(u_bf, sdir_ref[k], preferred_element_type=F32)
                  + u_pair * jnp.concatenate([dsk, dsk], axis=1))
        for j in range(tl // 2):
            for m in range(nsub):
                r0 = j * nb + SUBLANES * m
                slab_ref[k, _tb_rows(m, 2 * j, pitch), :] = y_pair[r0:r0 + SUBLANES, 0:LANES]
                slab_ref[k, _tb_rows(m, 2 * j + 1, pitch), :] = (
                    y_pair[r0:r0 + SUBLANES, LANES:2 * LANES])

    xb = x_ref[...].reshape(rows, D_MODEL).astype(BF16)

    def in_proj(lo, hi):
        z_ref[:, lo:hi] = (jnp.dot(xb, win_ref[:, lo:hi], preferred_element_type=F32)
                           + bin_ref[:, lo:hi])

    def s5_branch():
        ys = jnp.concatenate([_load_bt(slab_ref, j, nb, tl, pitch) for j in range(SSM_TILES)],
                             axis=1)
        sg = jax.nn.gelu(ys)
        in_proj(W_SSM, c0)
        glu = jnp.dot(sg.astype(BF16), wglu_ref[...], preferred_element_type=F32) + bglu_ref[...]
        s = sg * jax.nn.sigmoid(glu) * jax.nn.silu(z_ref[:, W_SSM:c0])
        mix_ref[:, 0:W_SSM] = s.astype(BF16)

    cw = 2 * LANES
    for h in range(CONV_TILES // 2):
        a_lo, b_lo = c0 + h * cw, c0 + W_CONV + h * cw
        in_proj(a_lo, a_lo + cw)
        in_proj(b_lo, b_lo + cw)
        v = z_ref[:, a_lo:a_lo + cw] * jax.nn.sigmoid(z_ref[:, b_lo:b_lo + cw])
        for j in range(2):
            _store_bt(slab_ref, SSM_TILES + 2 * h + j, v[:, LANES * j:LANES * (j + 1)],
                      nb, tl, pitch)
        if h == 0:
            in_proj(0, W_SSM)
            for j in range(SSM_TILES):
                _store_bt(slab_ref, j, z_ref[:, LANES * j:LANES * (j + 1)], nb, tl, pitch)
            for k in range(N_BLOCKS):
                s5_block(k)
            s5_branch()

    for c in range(CONV_TILES):
        lanes = slice(LANES * c, LANES * (c + 1))
        full_ref[hist:hist + rows, lanes] = _gather_tb(slab_ref, SSM_TILES + c, nb, tl, pitch)
    for c in range(CONV_TILES):
        lanes = slice(LANES * c, LANES * (c + 1))
        w = [wdw_ref[tap, :, lanes] for tap in range(CONV_K)]
        for m in range(nsub):
            for t0 in range(0, tl, conv_steps):
                base = t0 * nb + SUBLANES * m
                accs = [None] * conv_steps
                for j in range(conv_steps + CONV_K - 1):
                    xj = full_ref[base + j * nb:base + j * nb + SUBLANES, lanes]
                    for r in range(conv_steps):
                        tap = j - r
                        if 0 <= tap < CONV_K:
                            term = w[tap] * xj
                            accs[r] = term if accs[r] is None else accs[r] + term
                for r in range(conv_steps):
                    slab_ref[SSM_TILES + c, _tb_rows(m, t0 + r, pitch), :] = accs[r]
    cbuf_ref[...] = full_ref[rows:rows + hist, :]
    if n_steps > 1:
        full_ref[0:hist, :] = full_ref[rows:rows + hist, :]
    cv = jnp.concatenate([_load_bt(slab_ref, SSM_TILES + j, nb, tl, pitch)
                          for j in range(CONV_TILES)], axis=1) + bdw_ref[...]
    act = jax.nn.silu(_layernorm(cv, gcl_ref[...], bcl_ref[...]))
    in_proj(c0 + 2 * W_CONV, IN_COLS)
    cpw = jnp.dot(act.astype(BF16), wpw_ref[...], preferred_element_type=F32) + bpw_ref[...]
    cg = cpw * jax.nn.silu(z_ref[:, c0 + 2 * W_CONV:IN_COLS])
    mix_ref[:, W_SSM:W_SSM + W_CONV] = cg.astype(BF16)

    x2 = x_ref[...].reshape(rows, D_MODEL)
    n_chunks = max(1, rows // OUT_CHUNK_ROWS)
    rc = rows // n_chunks
    for r in range(n_chunks):
        rsl = slice(r * rc, (r + 1) * rc)
        mix = jnp.dot(mix_ref[rsl, :], wout_ref[...], preferred_element_type=F32)
        out = _layernorm(alpha * x2[rsl] + mix, gpost_ref[...], bpost_ref[...])
        if len(y_ref.shape) == 3:
            bsl = slice(r * (nb // n_chunks), (r + 1) * (nb // n_chunks))
            y_ref[bsl] = out.reshape(nb // n_chunks, tl, D_MODEL)
        else:
            y_ref[rsl, :] = out

    if state_t:
        hre_out_ref[...] = hre_ref[...].T
        him_out_ref[...] = him_ref[...].T


def _layer_call(x, state, consts, *, nb, tl, n_steps, alpha, conv_steps, name):
    rows = nb * tl
    hist = (CONV_K - 1) * nb
    pitch = _slab_pitch(tl)
    assert nb % SUBLANES == 0 and tl % conv_steps == 0 and (nb * pitch) % SUBLANES == 0
    assert tl % 2 == 0 and (n_steps == 1 or (tl >= CONV_K - 1 and tl % SUBLANES == 0))
    state_t = nb % LANES == 0
    state_shape = (N_STATE_ALL, nb) if state_t else (nb, N_STATE_ALL)
    if state_t and state is not None:
        state = (state[0].T, state[1].T, state[2])
    out_shape = (
        jax.ShapeDtypeStruct(x.shape, F32),
        jax.ShapeDtypeStruct(state_shape, F32),
        jax.ShapeDtypeStruct(state_shape, F32),
        jax.ShapeDtypeStruct((hist, W_CONV), F32),
    )
    scratch = [
        pltpu.VMEM((rows, IN_COLS), F32),
        pltpu.VMEM((rows // 2 + nb, 2 * BLOCK_STATE), F32),
        pltpu.VMEM((SSM_TILES + CONV_TILES, nb * pitch, LANES), F32),
        pltpu.VMEM((hist + rows, W_CONV), F32),
        pltpu.VMEM((rows, W_SSM + W_CONV), BF16),
    ]
    if state_t:
        scratch += [pltpu.VMEM((nb, N_STATE_ALL), F32)] * 2
    operands = (x,) + (() if state is None else tuple(state)) + tuple(consts)
    kernel = functools.partial(_layer_kernel, nb, tl, n_steps, alpha, conv_steps, state is None)

    def untransposed(outs):
        y, hre, him, cbuf = outs
        return (y, hre.T, him.T, cbuf) if state_t else (y, hre, him, cbuf)

    if n_steps == 1:
        assert x.shape == (nb, tl, D_MODEL)
        return untransposed(pl.pallas_call(
            kernel, out_shape=out_shape, scratch_shapes=scratch, name=name,
            compiler_params=pltpu.CompilerParams(vmem_limit_bytes=VMEM_LIMIT_BYTES),
        )(*operands))

    assert x.shape == (nb, tl * n_steps, D_MODEL)

    def whole(a):
        zeros = (0,) * a.ndim
        return pl.BlockSpec(a.shape, lambda i: zeros, pipeline_mode=pl.Buffered(1))

    x_spec = pl.BlockSpec((nb, tl, D_MODEL), lambda i: (0, i, 0))
    in_specs = [x_spec] + [whole(a) for a in operands[1:]]
    out_specs = (
        x_spec,
        pl.BlockSpec(state_shape, lambda i: (0, 0)),
        pl.BlockSpec(state_shape, lambda i: (0, 0)),
        pl.BlockSpec((hist, W_CONV), lambda i: (0, 0)),
    )
    return untransposed(pl.pallas_call(
        kernel, out_shape=out_shape, grid=(n_steps,), in_specs=in_specs, out_specs=out_specs,
        scratch_shapes=scratch, name=name,
        compiler_params=pltpu.CompilerParams(dimension_semantics=("arbitrary",),
                                             vmem_limit_bytes=VMEM_LIMIT_BYTES),
    )(*operands))


def _prep_call(lam_re, lam_im, log_dt, b_re, b_im, c_re, c_im):
    gp = N_GROUPS * GROUP_P
    ldt_n = jnp.broadcast_to(log_dt[:, None], (N_GROUPS, N_STATE))
    lam = (lam_re, lam_im, ldt_n)
    lamrow = tuple(a.reshape(1, N_STATE_ALL) for a in lam)
    bt = tuple(b.transpose(0, 2, 1).reshape(gp, N_STATE) for b in (b_re, b_im))
    cg = tuple(c.reshape(gp, N_STATE) for c in (c_re, c_im))
    out_shape = (
        jax.ShapeDtypeStruct((N_BLOCKS, 2 * LANES, 2 * BLOCK_STATE), BF16),
        jax.ShapeDtypeStruct((N_BLOCKS, 2 * BLOCK_STATE, 2 * LANES), BF16),
        jax.ShapeDtypeStruct((N_BLOCKS, 2 * LANES, 2 * LANES), BF16),
        jax.ShapeDtypeStruct((SUBLANES, N_STATE_ALL), F32),
        jax.ShapeDtypeStruct((SUBLANES, N_STATE_ALL), F32),
    )
    return pl.pallas_call(_prep_kernel, out_shape=out_shape, name="s5_prep")(
        *lam, *lamrow, *bt, *cg)


def kernel(x_prompt, x_sample, state_ssm_re, state_ssm_im, state_conv, w_in, b_in, lam_re, lam_im, log_dt, b_re, b_im, c_re, c_im, d_skip, w_glu, b_glu, w_dw, b_dw, g_conv_ln, b_conv_ln, w_pw2, b_pw2, w_out, g_post, b_post):
    depth = w_in.shape[0]
    alpha = (2.0 * depth) ** 0.25
    bsz, seq, _ = x_prompt.shape
    dbsz, dseq, _ = x_sample.shape
    tl_prompt = 128

    hp = x_prompt
    hs = x_sample
    outs = {k: [] for k in ("re_p", "im_p", "cv_p", "re_s", "im_s", "cv_s")}
    row = lambda a: a.reshape(1, -1)
    for layer in range(depth):
        s_in, s_out, s_dir, l2re, l2im = _prep_call(
            lam_re[layer], lam_im[layer], log_dt[layer], b_re[layer], b_im[layer],
            c_re[layer], c_im[layer])
        wdw8 = jnp.broadcast_to(w_dw[layer][:, None, :], (CONV_K, SUBLANES, W_CONV))
        consts = (w_in[layer].astype(BF16), row(b_in[layer]), l2re, l2im, s_in, s_out, s_dir,
                  row(d_skip[layer]), w_glu[layer].astype(BF16), row(b_glu[layer]), wdw8,
                  row(b_dw[layer]), row(g_conv_ln[layer]), row(b_conv_ln[layer]),
                  w_pw2[layer].astype(BF16), row(b_pw2[layer]), w_out[layer].astype(BF16),
                  row(g_post[layer]), row(b_post[layer]))

        hp, re_p, im_p, cv_p = _layer_call(
            hp, None, consts, nb=bsz, tl=tl_prompt,
            n_steps=seq // tl_prompt, alpha=alpha, conv_steps=8, name="layer_prompt")

        h0re = state_ssm_re[layer].reshape(dbsz, N_STATE_ALL)
        h0im = state_ssm_im[layer].reshape(dbsz, N_STATE_ALL)
        cbuf0 = state_conv[layer].transpose(1, 0, 2).reshape((CONV_K - 1) * dbsz, W_CONV)
        hs, re_s, im_s, cv_s = _layer_call(
            hs, (h0re, h0im, cbuf0), consts, nb=dbsz, tl=dseq, n_steps=1, alpha=alpha,
            conv_steps=dseq, name="layer_sample")

        unrow = lambda a, n: a.reshape(CONV_K - 1, n, W_CONV).transpose(1, 0, 2)
        outs["re_p"].append(re_p.reshape(bsz, N_GROUPS, N_STATE))
        outs["im_p"].append(im_p.reshape(bsz, N_GROUPS, N_STATE))
        outs["cv_p"].append(unrow(cv_p, bsz))
        outs["re_s"].append(re_s.reshape(dbsz, N_GROUPS, N_STATE))
        outs["im_s"].append(im_s.reshape(dbsz, N_GROUPS, N_STATE))
        outs["cv_s"].append(unrow(cv_s, dbsz))

    return (hp, hs,
            jnp.stack(outs["re_p"]), jnp.stack(outs["im_p"]), jnp.stack(outs["cv_p"]),
            jnp.stack(outs["re_s"]), jnp.stack(outs["im_s"]), jnp.stack(outs["cv_s"]))
```

```python
import functools

import jax
import jax.numpy as jnp
from jax import lax
from jax.experimental import pallas as pl
from jax.experimental.pallas import tpu as pltpu

D_MODEL = 1024
W_SSM = 512
W_CONV = 512
GROUP_P = 16
N_GROUPS = 32
N_STATE = 64
CONV_K = 31
IN_COLS = 2 * W_SSM + 3 * W_CONV
LN_EPS = 1e-5

LANES = 128
SUBLANES = 8
GROUPS_PER_BLOCK = LANES // GROUP_P
N_BLOCKS = N_GROUPS // GROUPS_PER_BLOCK
BLOCK_STATE = GROUPS_PER_BLOCK * N_STATE
N_STATE_ALL = N_GROUPS * N_STATE
SSM_TILES = W_SSM // LANES
CONV_TILES = W_CONV // LANES
OUT_CHUNK_ROWS = 256
VMEM_LIMIT_BYTES = 58 * 1024 * 1024

F32 = jnp.float32
BF16 = jnp.bfloat16


def _layernorm(x, g, b):
    mu = jnp.mean(x, axis=-1, keepdims=True)
    xc = x - mu
    var = jnp.mean(xc * xc, axis=-1, keepdims=True)
    return xc * lax.rsqrt(var + LN_EPS) * g + b


def _slab_pitch(tl):
    if tl % SUBLANES:
        return tl
    pitch = tl
    while (pitch // SUBLANES) % 2 == 0:
        pitch += SUBLANES
    return pitch


def _tb_rows(m, t, pitch):
    return pl.ds(SUBLANES * m * pitch + t, SUBLANES, stride=pitch)


def _store_bt(slab_ref, j, val, nb, tl, pitch):
    if pitch == tl:
        slab_ref[j] = val
    else:
        for b in range(nb):
            slab_ref[j, b * pitch:b * pitch + tl, :] = val[b * tl:(b + 1) * tl]


def _load_bt(slab_ref, j, nb, tl, pitch):
    if pitch == tl:
        return slab_ref[j]
    return jnp.concatenate([slab_ref[j, b * pitch:b * pitch + tl, :] for b in range(nb)], axis=0)


def _gather_tb(slab_ref, j, nb, tl, pitch):
    pieces = [slab_ref[j, _tb_rows(m, t, pitch), :]
              for t in range(tl) for m in range(nb // SUBLANES)]
    return jnp.concatenate(pieces, axis=0)


def _prep_kernel(lre_ref, lim_ref, ldt_ref, btr_ref, bti_ref, cgr_ref, cgi_ref,
                 sin_ref, sout_ref, sdir_ref, l2re_ref, l2im_ref):
    gp = N_GROUPS * GROUP_P
    abt = (((1,), (1,)), ((), ()))

    def zoh(lr, li, ldt):
        dt = jnp.exp(ldt)
        mag = jnp.exp(lr * dt)
        ang = li * dt
        return mag * jnp.cos(ang), mag * jnp.sin(ang)

    def per_channel(a):
        wide = jnp.broadcast_to(a[:, None, :], (N_GROUPS, GROUP_P, a.shape[-1]))
        return wide.reshape(gp, a.shape[-1])

    def per_block(a):
        return jnp.tile(a, (1, GROUPS_PER_BLOCK))

    lr, li = lre_ref[...], lim_ref[...]
    ar, ai = zoh(lr, li, ldt_ref[...])
    a_r, a_i = ar, ai
    nr, ni = ar - 1.0, ai
    den = lr * lr + li * li
    cr = per_channel(per_block((nr * lr + ni * li) / den))
    ci = per_channel(per_block((ni * lr - nr * li) / den))
    ar, ai = per_channel(per_block(ar)), per_channel(per_block(ai))
    shape = ar.shape
    row_g = lax.broadcasted_iota(jnp.int32, shape, 0) // GROUP_P % GROUPS_PER_BLOCK
    col_g = lax.broadcasted_iota(jnp.int32, shape, 1) // N_STATE
    keep = row_g == col_g
    br, bi = per_block(btr_ref[...]), per_block(bti_ref[...])
    bbr = jnp.where(keep, cr * br - ci * bi, 0.0)
    bbi = jnp.where(keep, cr * bi + ci * br, 0.0)
    lbr = ar * bbr - ai * bbi
    lbi = ar * bbi + ai * bbr
    c0r = jnp.where(keep, per_block(cgr_ref[...]), 0.0)
    c0i = jnp.where(keep, per_block(cgi_ref[...]), 0.0)
    c1r = ar * c0r - ai * c0i
    c1i = ar * c0i + ai * c0r
    c2r = ar * c1r - ai * c1i
    c2i = ar * c1i + ai * c1r

    for k in range(N_BLOCKS):
        rs = slice(LANES * k, LANES * (k + 1))
        b_k = jnp.concatenate([bbr[rs], bbi[rs]], axis=1)
        lb_k = jnp.concatenate([lbr[rs], lbi[rs]], axis=1)
        sin_ref[k, 0:LANES, :] = lb_k.astype(BF16)
        sin_ref[k, LANES:2 * LANES, :] = b_k.astype(BF16)
        c_k = jnp.concatenate([c0r[rs], -c0i[rs]], axis=1)
        c12_k = jnp.concatenate([jnp.concatenate([c1r[rs], -c1i[rs]], axis=1),
                                 jnp.concatenate([c2r[rs], -c2i[rs]], axis=1)], axis=0)
        sout_ref[k] = c12_k.T.astype(BF16)
        c_bf = c_k.astype(BF16)
        cb = lax.dot_general(b_k.astype(BF16), c_bf, abt, preferred_element_type=F32)
        clb = lax.dot_general(lb_k.astype(BF16), c_bf, abt, preferred_element_type=F32)
        sdir_ref[k, 0:LANES, 0:LANES] = cb.astype(BF16)
        sdir_ref[k, 0:LANES, LANES:2 * LANES] = clb.astype(BF16)
        sdir_ref[k, LANES:2 * LANES, 0:LANES] = jnp.zeros((LANES, LANES), BF16)
        sdir_ref[k, LANES:2 * LANES, LANES:2 * LANES] = cb.astype(BF16)

    def as_row(a):
        wide = jnp.tile(a, (1, N_GROUPS))
        own = (lax.broadcasted_iota(jnp.int32, wide.shape, 1) // N_STATE
               == lax.broadcasted_iota(jnp.int32, wide.shape, 0))
        return jnp.sum(jnp.where(own, wide, 0.0), axis=0, keepdims=True)

    l2re_ref[...] = jnp.broadcast_to(as_row(a_r * a_r - a_i * a_i), l2re_ref.shape)
    l2im_ref[...] = jnp.broadcast_to(as_row(2.0 * a_r * a_i), l2im_ref.shape)


def _layer_kernel(nb, tl, n_steps, alpha, conv_steps, zero_state, x_ref, *refs):
    if zero_state:
        refs = (None, None, None) + refs
    _layer_body(nb, tl, n_steps, alpha, conv_steps, x_ref, *refs)


def _layer_body(nb, tl, n_steps, alpha, conv_steps,
                x_ref, h0re_ref, h0im_ref, cbuf0_ref, win_ref, bin_ref, l2re_ref, l2im_ref,
                sin_ref, sout_ref, sdir_ref, dskip_ref, wglu_ref, bglu_ref, wdw_ref, bdw_ref,
                gcl_ref, bcl_ref, wpw_ref, bpw_ref, wout_ref, gpost_ref, bpost_ref,
                y_ref, hre_ref, him_ref, cbuf_ref,
                z_ref, bu_ref, slab_ref, full_ref, mix_ref, *state_scratch):
    state_t = bool(state_scratch)
    if state_t:
        hre_out_ref, him_out_ref = hre_ref, him_ref
        hre_ref, him_ref = state_scratch
    rows = nb * tl
    hist = (CONV_K - 1) * nb
    pitch = _slab_pitch(tl)
    nsub = nb // SUBLANES
    c0 = 2 * W_SSM

    def init_state():
        if h0re_ref is None:
            hre_ref[...] = jnp.zeros_like(hre_ref)
            him_ref[...] = jnp.zeros_like(him_ref)
            full_ref[0:hist, :] = jnp.zeros((hist, W_CONV), F32)
        else:
            hre_ref[...] = h0re_ref[...].T if state_t else h0re_ref[...]
            him_ref[...] = h0im_ref[...].T if state_t else h0im_ref[...]
            full_ref[0:hist, :] = cbuf0_ref[...]

    if n_steps > 1:
        pl.when(pl.program_id(0) == 0)(init_state)
    else:
        init_state()

    half = rows // 2

    def s5_block(k):
        cs = slice(LANES * k, LANES * (k + 1))
        ss = slice(BLOCK_STATE * k, BLOCK_STATE * (k + 1))
        pieces = {(t, m): slab_ref[k, _tb_rows(m, t, pitch), :]
                  for t in range(tl) for m in range(nsub)}
        u_pair = jnp.concatenate(
            [jnp.concatenate([pieces[2 * j, m], pieces[2 * j + 1, m]], axis=1)
             for j in range(tl // 2) for m in range(nsub)], axis=0)
        u_bf = u_pair.astype(BF16)
        bu_ref[0:nb, 0:BLOCK_STATE] = hre_ref[:, ss]
        bu_ref[0:nb, BLOCK_STATE:2 * BLOCK_STATE] = him_ref[:, ss]
        bu_ref[nb:nb + half, :] = jnp.dot(u_bf, sin_ref[k], preferred_element_type=F32)
        lr = l2re_ref[:, ss]
        li = l2im_ref[:, ss]
        for m in range(nsub):
            rs = slice(SUBLANES * m, SUBLANES * (m + 1))
            hr, hi = bu_ref[rs, 0:BLOCK_STATE], bu_ref[rs, BLOCK_STATE:2 * BLOCK_STATE]
            for j in range(tl // 2):
                row = slice((j + 1) * nb + SUBLANES * m, (j + 1) * nb + SUBLANES * (m + 1))
                br = bu_ref[row, 0:BLOCK_STATE]
                bi = bu_ref[row, BLOCK_STATE:2 * BLOCK_STATE]
                hr, hi = lr * hr - li * hi + br, lr * hi + li * hr + bi
                bu_ref[row, 0:BLOCK_STATE] = hr
                bu_ref[row, BLOCK_STATE:2 * BLOCK_STATE] = hi
            hre_ref[rs, ss] = hr
            him_ref[rs, ss] = hi
        dsk = dskip_ref[:, cs]
        y_pair = (jnp.dot(bu_ref[0:half, :].astype(BF16), sout_ref[k], preferred_element_type=F32)
                  + jnp.dot(u_bf, sdir_ref[k], preferred_element_type=F32)
                  + u_pair * jnp.concatenate([dsk, dsk], axis=1))
        for j in range(tl // 2):
            for m in range(nsub):
                r0 = j * nb + SUBLANES * m
                slab_ref[k, _tb_rows(m, 2 * j, pitch), :] = y_pair[r0:r0 + SUBLANES, 0:LANES]
                slab_ref[k, _tb_rows(m, 2 * j + 1, pitch), :] = (
                    y_pair[r0:r0 + SUBLANES, LANES:2 * LANES])

    xb = x_ref[...].reshape(rows, D_MODEL).astype(BF16)

    def in_proj(lo, hi):
        z_ref[:, lo:hi] = (jnp.dot(xb, win_ref[:, lo:hi], preferred_element_type=F32)
                           + bin_ref[:, lo:hi])

    def s5_branch():
        ys = jnp.concatenate([_load_bt(slab_ref, j, nb, tl, pitch) for j in range(SSM_TILES)],
                             axis=1)
        sg = jax.nn.gelu(ys)
        in_proj(W_SSM, c0)
        glu = jnp.dot(sg.astype(BF16), wglu_ref[...], preferred_element_type=F32) + bglu_ref[...]
        s = sg * jax.nn.sigmoid(glu) * jax.nn.silu(z_ref[:, W_SSM:c0])
        mix_ref[:, 0:W_SSM] = s.astype(BF16)

    cw = 2 * LANES
    for h in range(CONV_TILES // 2):
        a_lo, b_lo = c0 + h * cw, c0 + W_CONV + h * cw
        in_proj(a_lo, a_lo + cw)
        in_proj(b_lo, b_lo + cw)
        v = z_ref[:, a_lo:a_lo + cw] * jax.nn.sigmoid(z_ref[:, b_lo:b_lo + cw])
        for j in range(2):
            _store_bt(slab_ref, SSM_TILES + 2 * h + j, v[:, LANES * j:LANES * (j + 1)],
                      nb, tl, pitch)
        if h == 0:
            in_proj(0, W_SSM)
            for j in range(SSM_TILES):
                _store_bt(slab_ref, j, z_ref[:, LANES * j:LANES * (j + 1)], nb, tl, pitch)
            for k in range(N_BLOCKS):
                s5_block(k)
            s5_branch()

    for c in range(CONV_TILES):
        lanes = slice(LANES * c, LANES * (c + 1))
        full_ref[hist:hist + rows, lanes] = _gather_tb(slab_ref, SSM_TILES + c, nb, tl, pitch)
    for c in range(CONV_TILES):
        lanes = slice(LANES * c, LANES * (c + 1))
        w = [wdw_ref[tap, :, lanes] for tap in range(CONV_K)]
        for m in range(nsub):
            for t0 in range(0, tl, conv_steps):
                base = t0 * nb + SUBLANES * m
                accs = [None] * conv_steps
                for j in range(conv_steps + CONV_K - 1):
                    xj = full_ref[base + j * nb:base + j * nb + SUBLANES, lanes]
                    for r in range(conv_steps):
                        tap = j - r
                        if 0 <= tap < CONV_K:
                            term = w[tap] * xj
                            accs[r] = term if accs[r] is None else accs[r] + term
                for r in range(conv_steps):
                    slab_ref[SSM_TILES + c, _tb_rows(m, t0 + r, pitch), :] = accs[r]
    cbuf_ref[...] = full_ref[rows:rows + hist, :]
    if n_steps > 1:
        full_ref[0:hist, :] = full_ref[rows:rows + hist, :]
    cv = jnp.concatenate([_load_bt(slab_ref, SSM_TILES + j, nb, tl, pitch)
                          for j in range(CONV_TILES)], axis=1) + bdw_ref[...]
    act = jax.nn.silu(_layernorm(cv, gcl_ref[...], bcl_ref[...]))
    in_proj(c0 + 2 * W_CONV, IN_COLS)
    cpw = jnp.dot(act.astype(BF16), wpw_ref[...], preferred_element_type=F32) + bpw_ref[...]
    cg = cpw * jax.nn.silu(z_ref[:, c0 + 2 * W_CONV:IN_COLS])
    mix_ref[:, W_SSM:W_SSM + W_CONV] = cg.astype(BF16)

    x2 = x_ref[...].reshape(rows, D_MODEL)
    n_chunks = max(1, rows // OUT_CHUNK_ROWS)
    rc = rows // n_chunks
    for r in range(n_chunks):
        rsl = slice(r * rc, (r + 1) * rc)
        mix = jnp.dot(mix_ref[rsl, :], wout_ref[...], preferred_element_type=F32)
        out = _layernorm(alpha * x2[rsl] + mix, gpost_ref[...], bpost_ref[...])
        if len(y_ref.shape) == 3:
            bsl = slice(r * (nb // n_chunks), (r + 1) * (nb // n_chunks))
            y_ref[bsl] = out.reshape(nb // n_chunks, tl, D_MODEL)
        else:
            y_ref[rsl, :] = out

    if state_t:
        hre_out_ref[...] = hre_ref[...].T
        him_out_ref[...] = him_ref[...].T


def _layer_call(x, state, consts, *, nb, tl, n_steps, alpha, conv_steps, name):
    rows = nb * tl
    hist = (CONV_K - 1) * nb
    pitch = _slab_pitch(tl)
    assert nb % SUBLANES == 0 and tl % conv_steps == 0 and (nb * pitch) % SUBLANES == 0
    assert tl % 2 == 0 and (n_steps == 1 or (tl >= CONV_K - 1 and tl % SUBLANES == 0))
    state_t = nb % LANES == 0
    state_shape = (N_STATE_ALL, nb) if state_t else (nb, N_STATE_ALL)
    if state_t and state is not None:
        state = (state[0].T, state[1].T, state[2])
    out_shape = (
        jax.ShapeDtypeStruct(x.shape, F32),
        jax.ShapeDtypeStruct(state_shape, F32),
        jax.ShapeDtypeStruct(state_shape, F32),
        jax.ShapeDtypeStruct((hist, W_CONV), F32),
    )
    scratch = [
        pltpu.VMEM((rows, IN_COLS), F32),
        pltpu.VMEM((rows // 2 + nb, 2 * BLOCK_STATE), F32),
        pltpu.VMEM((SSM_TILES + CONV_TILES, nb * pitch, LANES), F32),
        pltpu.VMEM((hist + rows, W_CONV), F32),
        pltpu.VMEM((rows, W_SSM + W_CONV), BF16),
    ]
    if state_t:
        scratch += [pltpu.VMEM((nb, N_STATE_ALL), F32)] * 2
    operands = (x,) + (() if state is None else tuple(state)) + tuple(consts)
    kernel = functools.partial(_layer_kernel, nb, tl, n_steps, alpha, conv_steps, state is None)

    def untransposed(outs):
        y, hre, him, cbuf = outs
        return (y, hre.T, him.T, cbuf) if state_t else (y, hre, him, cbuf)

    if n_steps == 1:
        assert x.shape == (nb, tl, D_MODEL)
        return untransposed(pl.pallas_call(
            kernel, out_shape=out_shape, scratch_shapes=scratch, name=name,
            compiler_params=pltpu.CompilerParams(vmem_limit_bytes=VMEM_LIMIT_BYTES),
        )(*operands))

    assert x.shape == (nb, tl * n_steps, D_MODEL)

    def whole(a):
        zeros = (0,) * a.ndim
        return pl.BlockSpec(a.shape, lambda i: zeros, pipeline_mode=pl.Buffered(1))

    x_spec = pl.BlockSpec((nb, tl, D_MODEL), lambda i: (0, i, 0))
    in_specs = [x_spec] + [whole(a) for a in operands[1:]]
    out_specs = (
        x_spec,
        pl.BlockSpec(state_shape, lambda i: (0, 0)),
        pl.BlockSpec(state_shape, lambda i: (0, 0)),
        pl.BlockSpec((hist, W_CONV), lambda i: (0, 0)),
    )
    return untransposed(pl.pallas_call(
        kernel, out_shape=out_shape, grid=(n_steps,), in_specs=in_specs, out_specs=out_specs,
        scratch_shapes=scratch, name=name,
        compiler_params=pltpu.CompilerParams(dimension_semantics=("arbitrary",),
                                             vmem_limit_bytes=VMEM_LIMIT_BYTES),
    )(*operands))


def _prep_call(lam_re, lam_im, log_dt, b_re, b_im, c_re, c_im):
    gp = N_GROUPS * GROUP_P
    ldt_n = jnp.broadcast_to(log_dt[:, None], (N_GROUPS, N_STATE))
    lam = (lam_re, lam_im, ldt_n)
    bt = tuple(b.transpose(0, 2, 1).reshape(gp, N_STATE) for b in (b_re, b_im))
    cg = tuple(c.reshape(gp, N_STATE) for c in (c_re, c_im))
    out_shape = (
        jax.ShapeDtypeStruct((N_BLOCKS, 2 * LANES, 2 * BLOCK_STATE), BF16),
        jax.ShapeDtypeStruct((N_BLOCKS, 2 * BLOCK_STATE, 2 * LANES), BF16),
        jax.ShapeDtypeStruct((N_BLOCKS, 2 * LANES, 2 * LANES), BF16),
        jax.ShapeDtypeStruct((SUBLANES, N_STATE_ALL), F32),
        jax.ShapeDtypeStruct((SUBLANES, N_STATE_ALL), F32),
    )
    return pl.pallas_call(_prep_kernel, out_shape=out_shape, name="s5_prep")(*lam, *bt, *cg)


def kernel(x_prompt, x_sample, state_ssm_re, state_ssm_im, state_conv, w_in, b_in, lam_re, lam_im, log_dt, b_re, b_im, c_re, c_im, d_skip, w_glu, b_glu, w_dw, b_dw, g_conv_ln, b_conv_ln, w_pw2, b_pw2, w_out, g_post, b_post):
    depth = w_in.shape[0]
    alpha = (2.0 * depth) ** 0.25
    bsz, seq, _ = x_prompt.shape
    dbsz, dseq, _ = x_sample.shape
    tl_prompt = 128

    hp = x_prompt
    hs = x_sample
    outs = {k: [] for k in ("re_p", "im_p", "cv_p", "re_s", "im_s", "cv_s")}
    row = lambda a: a.reshape(1, -1)
    for layer in range(depth):
        s_in, s_out, s_dir, l2re, l2im = _prep_call(
            lam_re[layer], lam_im[layer], log_dt[layer], b_re[layer], b_im[layer],
            c_re[layer], c_im[layer])
        wdw8 = jnp.broadcast_to(w_dw[layer][:, None, :], (CONV_K, SUBLANES, W_CONV))
        consts = (w_in[layer].astype(BF16), row(b_in[layer]), l2re, l2im, s_in, s_out, s_dir,
                  row(d_skip[layer]), w_glu[layer].astype(BF16), row(b_glu[layer]), wdw8,
                  row(b_dw[layer]), row(g_conv_ln[layer]), row(b_conv_ln[layer]),
                  w_pw2[layer].astype(BF16), row(b_pw2[layer]), w_out[layer].astype(BF16),
                  row(g_post[layer]), row(b_post[layer]))

        hp, re_p, im_p, cv_p = _layer_call(
            hp, None, consts, nb=bsz, tl=tl_prompt,
            n_steps=seq // tl_prompt, alpha=alpha, conv_steps=8, name="layer_prompt")

        h0re = state_ssm_re[layer].reshape(dbsz, N_STATE_ALL)
        h0im = state_ssm_im[layer].reshape(dbsz, N_STATE_ALL)
        cbuf0 = state_conv[layer].transpose(1, 0, 2).reshape((CONV_K - 1) * dbsz, W_CONV)
        hs, re_s, im_s, cv_s = _layer_call(
            hs, (h0re, h0im, cbuf0), consts, nb=dbsz, tl=dseq, n_steps=1, alpha=alpha,
            conv_steps=dseq, name="layer_sample")

        unrow = lambda a, n: a.reshape(CONV_K - 1, n, W_CONV).transpose(1, 0, 2)
        outs["re_p"].append(re_p.reshape(bsz, N_GROUPS, N_STATE))
        outs["im_p"].append(im_p.reshape(bsz, N_GROUPS, N_STATE))
        outs["cv_p"].append(unrow(cv_p, bsz))
        outs["re_s"].append(re_s.reshape(dbsz, N_GROUPS, N_STATE))
        outs["im_s"].append(im_s.reshape(dbsz, N_GROUPS, N_STATE))
        outs["cv_s"].append(unrow(cv_s, dbsz))

    return (hp, hs,
            jnp.stack(outs["re_p"]), jnp.stack(outs["im_p"]), jnp.stack(outs["cv_p"]),
            jnp.stack(outs["re_s"]), jnp.stack(outs["im_s"]), jnp.stack(outs["cv_s"]))
```

```python
import functools

import jax
import jax.numpy as jnp
from jax import lax
from jax.experimental import pallas as pl
from jax.experimental.pallas import tpu as pltpu

D_MODEL = 1024
W_SSM = 512
W_CONV = 512
GROUP_P = 16
N_GROUPS = 32
N_STATE = 64
CONV_K = 31
IN_COLS = 2 * W_SSM + 3 * W_CONV
LN_EPS = 1e-5

LANES = 128
SUBLANES = 8
GROUPS_PER_BLOCK = LANES // GROUP_P
N_BLOCKS = N_GROUPS // GROUPS_PER_BLOCK
BLOCK_STATE = GROUPS_PER_BLOCK * N_STATE
N_STATE_ALL = N_GROUPS * N_STATE
SSM_TILES = W_SSM // LANES
CONV_TILES = W_CONV // LANES
OUT_CHUNK_ROWS = 256
VMEM_LIMIT_BYTES = 58 * 1024 * 1024

F32 = jnp.float32
BF16 = jnp.bfloat16


def _layernorm(x, g, b):
    mu = jnp.mean(x, axis=-1, keepdims=True)
    xc = x - mu
    var = jnp.mean(xc * xc, axis=-1, keepdims=True)
    return xc * lax.rsqrt(var + LN_EPS) * g + b


def _slab_pitch(tl):
    if tl % SUBLANES:
        return tl
    pitch = tl
    while (pitch // SUBLANES) % 2 == 0:
        pitch += SUBLANES
    return pitch


def _tb_rows(m, t, pitch):
    return pl.ds(SUBLANES * m * pitch + t, SUBLANES, stride=pitch)


def _store_bt(slab_ref, j, val, nb, tl, pitch):
    if pitch == tl:
        slab_ref[j] = val
    else:
        for b in range(nb):
            slab_ref[j, b * pitch:b * pitch + tl, :] = val[b * tl:(b + 1) * tl]


def _load_bt(slab_ref, j, nb, tl, pitch):
    if pitch == tl:
        return slab_ref[j]
    return jnp.concatenate([slab_ref[j, b * pitch:b * pitch + tl, :] for b in range(nb)], axis=0)


def _gather_tb(slab_ref, j, nb, tl, pitch):
    pieces = [slab_ref[j, _tb_rows(m, t, pitch), :]
              for t in range(tl) for m in range(nb // SUBLANES)]
    return jnp.concatenate(pieces, axis=0)


def _prep_kernel(lre_ref, lim_ref, ldt_ref, btr_ref, bti_ref, cgr_ref, cgi_ref,
                 sin_ref, sout_ref, sdir_ref, l2re_ref, l2im_ref):
    gp = N_GROUPS * GROUP_P
    abt = (((1,), (1,)), ((), ()))

    def zoh(lr, li, ldt):
        dt = jnp.exp(ldt)
        mag = jnp.exp(lr * dt)
        ang = li * dt
        return mag * jnp.cos(ang), mag * jnp.sin(ang)

    def per_channel(a):
        wide = jnp.broadcast_to(a[:, None, :], (N_GROUPS, GROUP_P, a.shape[-1]))
        return wide.reshape(gp, a.shape[-1])

    def per_block(a):
        return jnp.tile(a, (1, GROUPS_PER_BLOCK))

    lr, li = lre_ref[...], lim_ref[...]
    ar, ai = zoh(lr, li, ldt_ref[...])
    a_r, a_i = ar, ai
    nr, ni = ar - 1.0, ai
    den = lr * lr + li * li
    cr = per_channel(per_block((nr * lr + ni * li) / den))
    ci = per_channel(per_block((ni * lr - nr * li) / den))
    ar, ai = per_channel(per_block(ar)), per_channel(per_block(ai))
    shape = ar.shape
    row_g = lax.broadcasted_iota(jnp.int32, shape, 0) // GROUP_P % GROUPS_PER_BLOCK
    col_g = lax.broadcasted_iota(jnp.int32, shape, 1) // N_STATE
    keep = row_g == col_g
    br, bi = per_block(btr_ref[...]), per_block(bti_ref[...])
    bbr = jnp.where(keep, cr * br - ci * bi, 0.0)
    bbi = jnp.where(keep, cr * bi + ci * br, 0.0)
    lbr = ar * bbr - ai * bbi
    lbi = ar * bbi + ai * bbr
    c0r = jnp.where(keep, per_block(cgr_ref[...]), 0.0)
    c0i = jnp.where(keep, per_block(cgi_ref[...]), 0.0)
    c1r = ar * c0r - ai * c0i
    c1i = ar * c0i + ai * c0r
    c2r = ar * c1r - ai * c1i
    c2i = ar * c1i + ai * c1r

    for k in range(N_BLOCKS):
        rs = slice(LANES * k, LANES * (k + 1))
        b_k = jnp.concatenate([bbr[rs], bbi[rs]], axis=1)
        lb_k = jnp.concatenate([lbr[rs], lbi[rs]], axis=1)
        sin_ref[k, 0:LANES, :] = lb_k.astype(BF16)
        sin_ref[k, LANES:2 * LANES, :] = b_k.astype(BF16)
        c_k = jnp.concatenate([c0r[rs], -c0i[rs]], axis=1)
        c12_k = jnp.concatenate([jnp.concatenate([c1r[rs], -c1i[rs]], axis=1),
                                 jnp.concatenate([c2r[rs], -c2i[rs]], axis=1)], axis=0)
        sout_ref[k] = c12_k.T.astype(BF16)
        c_bf = c_k.astype(BF16)
        cb = lax.dot_general(b_k.astype(BF16), c_bf, abt, preferred_element_type=F32)
        clb = lax.dot_general(lb_k.astype(BF16), c_bf, abt, preferred_element_type=F32)
        sdir_ref[k, 0:LANES, 0:LANES] = cb.astype(BF16)
        sdir_ref[k, 0:LANES, LANES:2 * LANES] = clb.astype(BF16)
        sdir_ref[k, LANES:2 * LANES, 0:LANES] = jnp.zeros((LANES, LANES), BF16)
        sdir_ref[k, LANES:2 * LANES, LANES:2 * LANES] = cb.astype(BF16)

    def as_row(a):
        wide = jnp.tile(a, (1, N_GROUPS))
        own = (lax.broadcasted_iota(jnp.int32, wide.shape, 1) // N_STATE
               == lax.broadcasted_iota(jnp.int32, wide.shape, 0))
        return jnp.sum(jnp.where(own, wide, 0.0), axis=0, keepdims=True)

    l2re_ref[...] = jnp.broadcast_to(as_row(a_r * a_r - a_i * a_i), l2re_ref.shape)
    l2im_ref[...] = jnp.broadcast_to(as_row(2.0 * a_r * a_i), l2im_ref.shape)


def _layer_kernel(nb, tl, n_steps, alpha, conv_steps, zero_state, x_ref, *refs):
    if zero_state:
        refs = (None, None, None) + refs
    _layer_body(nb, tl, n_steps, alpha, conv_steps, x_ref, *refs)


def _layer_body(nb, tl, n_steps, alpha, conv_steps,
                x_ref, h0re_ref, h0im_ref, cbuf0_ref, win_ref, bin_ref, l2re_ref, l2im_ref,
                sin_ref, sout_ref, sdir_ref, dskip_ref, wglu_ref, bglu_ref, wdw_ref, bdw_ref,
                gcl_ref, bcl_ref, wpw_ref, bpw_ref, wout_ref, gpost_ref, bpost_ref,
                y_ref, hre_out_ref, him_out_ref, cbuf_ref,
                z_ref, bu_ref, slab_ref, full_ref, mix_ref, hre_ref, him_ref):
    state_t = nb % LANES == 0
    rows = nb * tl
    hist = (CONV_K - 1) * nb
    pitch = _slab_pitch(tl)
    nsub = nb // SUBLANES
    c0 = 2 * W_SSM

    def init_state():
        if h0re_ref is None:
            hre_ref[...] = jnp.zeros_like(hre_ref)
            him_ref[...] = jnp.zeros_like(him_ref)
            full_ref[0:hist, :] = jnp.zeros((hist, W_CONV), F32)
        else:
            hre_ref[...] = h0re_ref[...].T if state_t else h0re_ref[...]
            him_ref[...] = h0im_ref[...].T if state_t else h0im_ref[...]
            full_ref[0:hist, :] = cbuf0_ref[...]

    if n_steps > 1:
        pl.when(pl.program_id(0) == 0)(init_state)
    else:
        init_state()

    half = rows // 2

    def s5_block(k):
        cs = slice(LANES * k, LANES * (k + 1))
        ss = slice(BLOCK_STATE * k, BLOCK_STATE * (k + 1))
        pieces = {(t, m): slab_ref[k, _tb_rows(m, t, pitch), :]
                  for t in range(tl) for m in range(nsub)}
        u_pair = jnp.concatenate(
            [jnp.concatenate([pieces[2 * j, m], pieces[2 * j + 1, m]], axis=1)
             for j in range(tl // 2) for m in range(nsub)], axis=0)
        u_bf = u_pair.astype(BF16)
        bu_ref[0:nb, 0:BLOCK_STATE] = hre_ref[:, ss]
        bu_ref[0:nb, BLOCK_STATE:2 * BLOCK_STATE] = him_ref[:, ss]
        bu_ref[nb:nb + half, :] = jnp.dot(u_bf, sin_ref[k], preferred_element_type=F32)
        lr = l2re_ref[:, ss]
        li = l2im_ref[:, ss]
        for m in range(nsub):
            rs = slice(SUBLANES * m, SUBLANES * (m + 1))
            hr, hi = bu_ref[rs, 0:BLOCK_STATE], bu_ref[rs, BLOCK_STATE:2 * BLOCK_STATE]
            for j in range(tl // 2):
                row = slice((j + 1) * nb + SUBLANES * m, (j + 1) * nb + SUBLANES * (m + 1))
                br = bu_ref[row, 0:BLOCK_STATE]
                bi = bu_ref[row, BLOCK_STATE:2 * BLOCK_STATE]
                hr, hi = lr * hr - li * hi + br, lr * hi + li * hr + bi
                bu_ref[row, 0:BLOCK_STATE] = hr
                bu_ref[row, BLOCK_STATE:2 * BLOCK_STATE] = hi
            hre_ref[rs, ss] = hr
            him_ref[rs, ss] = hi
        dsk = dskip_ref[:, cs]
        y_pair = (jnp.dot(bu_ref[0:half, :].astype(BF16), sout_ref[k], preferred_element_type=F32)
                  + jnp.dot(u_bf, sdir_ref[k], preferred_element_type=F32)
                  + u_pair * jnp.concatenate([dsk, dsk], axis=1))
        for j in range(tl // 2):
            for m in range(nsub):
                r0 = j * nb + SUBLANES * m
                slab_ref[k, _tb_rows(m, 2 * j, pitch), :] = y_pair[r0:r0 + SUBLANES, 0:LANES]
                slab_ref[k, _tb_rows(m, 2 * j + 1, pitch), :] = (
                    y_pair[r0:r0 + SUBLANES, LANES:2 * LANES])

    xb = x_ref[...].reshape(rows, D_MODEL).astype(BF16)

    def in_proj(lo, hi):
        z_ref[:, lo:hi] = (jnp.dot(xb, win_ref[:, lo:hi], preferred_element_type=F32)
                           + bin_ref[:, lo:hi])

    def s5_branch():
        ys = jnp.concatenate([_load_bt(slab_ref, j, nb, tl, pitch) for j in range(SSM_TILES)],
                             axis=1)
        sg = jax.nn.gelu(ys)
        in_proj(W_SSM, c0)
        glu = jnp.dot(sg.astype(BF16), wglu_ref[...], preferred_element_type=F32) + bglu_ref[...]
        s = sg * jax.nn.sigmoid(glu) * jax.nn.silu(z_ref[:, W_SSM:c0])
        mix_ref[:, 0:W_SSM] = s.astype(BF16)

    cw = 2 * LANES
    for h in range(CONV_TILES // 2):
        a_lo, b_lo = c0 + h * cw, c0 + W_CONV + h * cw
        in_proj(a_lo, a_lo + cw)
        in_proj(b_lo, b_lo + cw)
        v = z_ref[:, a_lo:a_lo + cw] * jax.nn.sigmoid(z_ref[:, b_lo:b_lo + cw])
        for j in range(2):
            _store_bt(slab_ref, SSM_TILES + 2 * h + j, v[:, LANES * j:LANES * (j + 1)],
                      nb, tl, pitch)
        if h == 0:
            in_proj(0, W_SSM)
            for j in range(SSM_TILES):
                _store_bt(slab_ref, j, z_ref[:, LANES * j:LANES * (j + 1)], nb, tl, pitch)
            for k in range(N_BLOCKS):
                s5_block(k)
            s5_branch()

    for c in range(CONV_TILES):
        lanes = slice(LANES * c, LANES * (c + 1))
        full_ref[hist:hist + rows, lanes] = _gather_tb(slab_ref, SSM_TILES + c, nb, tl, pitch)
    for c in range(CONV_TILES):
        lanes = slice(LANES * c, LANES * (c + 1))
        w = [wdw_ref[tap, :, lanes] for tap in range(CONV_K)]
        for m in range(nsub):
            for t0 in range(0, tl, conv_steps):
                base = t0 * nb + SUBLANES * m
                accs = [None] * conv_steps
                for j in range(conv_steps + CONV_K - 1):
                    xj = full_ref[base + j * nb:base + j * nb + SUBLANES, lanes]
                    for r in range(conv_steps):
                        tap = j - r
                        if 0 <= tap < CONV_K:
                            term = w[tap] * xj
                            accs[r] = term if accs[r] is None else accs[r] + term
                for r in range(conv_steps):
                    slab_ref[SSM_TILES + c, _tb_rows(m, t0 + r, pitch), :] = accs[r]
    cbuf_ref[...] = full_ref[rows:rows + hist, :]
    if n_steps > 1:
        full_ref[0:hist, :] = full_ref[rows:rows + hist, :]
    cv = jnp.concatenate([_load_bt(slab_ref, SSM_TILES + j, nb, tl, pitch)
                          for j in range(CONV_TILES)], axis=1) + bdw_ref[...]
    act = jax.nn.silu(_layernorm(cv, gcl_ref[...], bcl_ref[...]))
    in_proj(c0 + 2 * W_CONV, IN_COLS)
    cpw = jnp.dot(act.astype(BF16), wpw_ref[...], preferred_element_type=F32) + bpw_ref[...]
    cg = cpw * jax.nn.silu(z_ref[:, c0 + 2 * W_CONV:IN_COLS])
    mix_ref[:, W_SSM:W_SSM + W_CONV] = cg.astype(BF16)

    x2 = x_ref[...].reshape(rows, D_MODEL)
    n_chunks = max(1, rows // OUT_CHUNK_ROWS)
    rc = rows // n_chunks
    for r in range(n_chunks):
        rsl = slice(r * rc, (r + 1) * rc)
        mix = jnp.dot(mix_ref[rsl, :], wout_ref[...], preferred_element_type=F32)
        out = _layernorm(alpha * x2[rsl] + mix, gpost_ref[...], bpost_ref[...])
        if len(y_ref.shape) == 3:
            bsl = slice(r * (nb // n_chunks), (r + 1) * (nb // n_chunks))
            y_ref[bsl] = out.reshape(nb // n_chunks, tl, D_MODEL)
        else:
            y_ref[rsl, :] = out

    if state_t:
        hre_out_ref[...] = hre_ref[...].T
        him_out_ref[...] = him_ref[...].T
    else:
        for g in range(N_GROUPS):
            gs = slice(g * N_STATE, (g + 1) * N_STATE)
            hre_out_ref[pl.ds(g, nb, stride=N_GROUPS), :] = hre_ref[:, gs]
            him_out_ref[pl.ds(g, nb, stride=N_GROUPS), :] = him_ref[:, gs]


def _layer_call(x, state, consts, *, nb, tl, n_steps, alpha, conv_steps, name):
    rows = nb * tl
    hist = (CONV_K - 1) * nb
    pitch = _slab_pitch(tl)
    assert nb % SUBLANES == 0 and tl % conv_steps == 0 and (nb * pitch) % SUBLANES == 0
    assert tl % 2 == 0 and (n_steps == 1 or (tl >= CONV_K - 1 and tl % SUBLANES == 0))
    state_t = nb % LANES == 0
    state_shape = (N_STATE_ALL, nb) if state_t else (nb * N_GROUPS, N_STATE)
    if state_t and state is not None:
        state = (state[0].T, state[1].T, state[2])
    out_shape = (
        jax.ShapeDtypeStruct(x.shape, F32),
        jax.ShapeDtypeStruct(state_shape, F32),
        jax.ShapeDtypeStruct(state_shape, F32),
        jax.ShapeDtypeStruct((hist, W_CONV), F32),
    )
    scratch = [
        pltpu.VMEM((rows, IN_COLS), F32),
        pltpu.VMEM((rows // 2 + nb, 2 * BLOCK_STATE), F32),
        pltpu.VMEM((SSM_TILES + CONV_TILES, nb * pitch, LANES), F32),
        pltpu.VMEM((hist + rows, W_CONV), F32),
        pltpu.VMEM((rows, W_SSM + W_CONV), BF16),
    ]
    scratch += [pltpu.VMEM((nb, N_STATE_ALL), F32)] * 2
    operands = (x,) + (() if state is None else tuple(state)) + tuple(consts)
    kernel = functools.partial(_layer_kernel, nb, tl, n_steps, alpha, conv_steps, state is None)

    def untransposed(outs):
        y, hre, him, cbuf = outs
        if state_t:
            return y, hre.T, him.T, cbuf
        return y, hre.reshape(nb, N_STATE_ALL), him.reshape(nb, N_STATE_ALL), cbuf

    if n_steps == 1:
        assert x.shape == (nb, tl, D_MODEL)
        return untransposed(pl.pallas_call(
            kernel, out_shape=out_shape, scratch_shapes=scratch, name=name,
            compiler_params=pltpu.CompilerParams(vmem_limit_bytes=VMEM_LIMIT_BYTES),
        )(*operands))

    assert x.shape == (nb, tl * n_steps, D_MODEL)

    def whole(a):
        zeros = (0,) * a.ndim
        return pl.BlockSpec(a.shape, lambda i: zeros, pipeline_mode=pl.Buffered(1))

    x_spec = pl.BlockSpec((nb, tl, D_MODEL), lambda i: (0, i, 0))
    in_specs = [x_spec] + [whole(a) for a in operands[1:]]
    out_specs = (
        x_spec,
        pl.BlockSpec(state_shape, lambda i: (0, 0)),
        pl.BlockSpec(state_shape, lambda i: (0, 0)),
        pl.BlockSpec((hist, W_CONV), lambda i: (0, 0)),
    )
    return untransposed(pl.pallas_call(
        kernel, out_shape=out_shape, grid=(n_steps,), in_specs=in_specs, out_specs=out_specs,
        scratch_shapes=scratch, name=name,
        compiler_params=pltpu.CompilerParams(dimension_semantics=("arbitrary",),
                                             vmem_limit_bytes=VMEM_LIMIT_BYTES),
    )(*operands))


def _prep_call(lam_re, lam_im, log_dt, b_re, b_im, c_re, c_im):
    gp = N_GROUPS * GROUP_P
    ldt_n = jnp.broadcast_to(log_dt[:, None], (N_GROUPS, N_STATE))
    lam = (lam_re, lam_im, ldt_n)
    bt = tuple(b.transpose(0, 2, 1).reshape(gp, N_STATE) for b in (b_re, b_im))
    cg = tuple(c.reshape(gp, N_STATE) for c in (c_re, c_im))
    out_shape = (
        jax.ShapeDtypeStruct((N_BLOCKS, 2 * LANES, 2 * BLOCK_STATE), BF16),
        jax.ShapeDtypeStruct((N_BLOCKS, 2 * BLOCK_STATE, 2 * LANES), BF16),
        jax.ShapeDtypeStruct((N_BLOCKS, 2 * LANES, 2 * LANES), BF16),
        jax.ShapeDtypeStruct((SUBLANES, N_STATE_ALL), F32),
        jax.ShapeDtypeStruct((SUBLANES, N_STATE_ALL), F32),
    )
    return pl.pallas_call(_prep_kernel, out_shape=out_shape, name="s5_prep")(*lam, *bt, *cg)


def kernel(x_prompt, x_sample, state_ssm_re, state_ssm_im, state_conv, w_in, b_in, lam_re, lam_im, log_dt, b_re, b_im, c_re, c_im, d_skip, w_glu, b_glu, w_dw, b_dw, g_conv_ln, b_conv_ln, w_pw2, b_pw2, w_out, g_post, b_post):
    depth = w_in.shape[0]
    alpha = (2.0 * depth) ** 0.25
    bsz, seq, _ = x_prompt.shape
    dbsz, dseq, _ = x_sample.shape
    tl_prompt = 128

    hp = x_prompt
    hs = x_sample
    outs = {k: [] for k in ("re_p", "im_p", "cv_p", "re_s", "im_s", "cv_s")}
    row = lambda a: a.reshape(1, -1)
    for layer in range(depth):
        s_in, s_out, s_dir, l2re, l2im = _prep_call(
            lam_re[layer], lam_im[layer], log_dt[layer], b_re[layer], b_im[layer],
            c_re[layer], c_im[layer])
        wdw8 = jnp.broadcast_to(w_dw[layer][:, None, :], (CONV_K, SUBLANES, W_CONV))
        consts = (w_in[layer].astype(BF16), row(b_in[layer]), l2re, l2im, s_in, s_out, s_dir,
                  row(d_skip[layer]), w_glu[layer].astype(BF16), row(b_glu[layer]), wdw8,
                  row(b_dw[layer]), row(g_conv_ln[layer]), row(b_conv_ln[layer]),
                  w_pw2[layer].astype(BF16), row(b_pw2[layer]), w_out[layer].astype(BF16),
                  row(g_post[layer]), row(b_post[layer]))

        hp, re_p, im_p, cv_p = _layer_call(
            hp, None, consts, nb=bsz, tl=tl_prompt,
            n_steps=seq // tl_prompt, alpha=alpha, conv_steps=8, name="layer_prompt")

        h0re = state_ssm_re[layer].reshape(dbsz, N_STATE_ALL)
        h0im = state_ssm_im[layer].reshape(dbsz, N_STATE_ALL)
        cbuf0 = state_conv[layer].transpose(1, 0, 2).reshape((CONV_K - 1) * dbsz, W_CONV)
        hs, re_s, im_s, cv_s = _layer_call(
            hs, (h0re, h0im, cbuf0), consts, nb=dbsz, tl=dseq, n_steps=1, alpha=alpha,
            conv_steps=dseq, name="layer_sample")

        unrow = lambda a, n: a.reshape(CONV_K - 1, n, W_CONV).transpose(1, 0, 2)
        outs["re_p"].append(re_p.reshape(bsz, N_GROUPS, N_STATE))
        outs["im_p"].append(im_p.reshape(bsz, N_GROUPS, N_STATE))
        outs["cv_p"].append(unrow(cv_p, bsz))
        outs["re_s"].append(re_s.reshape(dbsz, N_GROUPS, N_STATE))
        outs["im_s"].append(im_s.reshape(dbsz, N_GROUPS, N_STATE))
        outs["cv_s"].append(unrow(cv_s, dbsz))

    return (hp, hs,
            jnp.stack(outs["re_p"]), jnp.stack(outs["im_p"]), jnp.stack(outs["cv_p"]),
            jnp.stack(outs["re_s"]), jnp.stack(outs["im_s"]), jnp.stack(outs["cv_s"]))
```

```python
import functools

import jax
import jax.numpy as jnp
from jax import lax
from jax.experimental import pallas as pl
from jax.experimental.pallas import tpu as pltpu

D_MODEL = 1024
W_SSM = 512
W_CONV = 512
GROUP_P = 16
N_GROUPS = 32
N_STATE = 64
CONV_K = 31
IN_COLS = 2 * W_SSM + 3 * W_CONV
LN_EPS = 1e-5

LANES = 128
SUBLANES = 8
GROUPS_PER_BLOCK = LANES // GROUP_P
N_BLOCKS = N_GROUPS // GROUPS_PER_BLOCK
BLOCK_STATE = GROUPS_PER_BLOCK * N_STATE
N_STATE_ALL = N_GROUPS * N_STATE
SSM_TILES = W_SSM // LANES
CONV_TILES = W_CONV // LANES
OUT_CHUNK_ROWS = 256
VMEM_LIMIT_BYTES = 58 * 1024 * 1024

F32 = jnp.float32
BF16 = jnp.bfloat16


def _layernorm(x, g, b):
    mu = jnp.mean(x, axis=-1, keepdims=True)
    xc = x - mu
    var = jnp.mean(xc * xc, axis=-1, keepdims=True)
    return xc * lax.rsqrt(var + LN_EPS) * g + b


def _slab_pitch(tl):
    if tl % SUBLANES:
        return tl
    pitch = tl
    while (pitch // SUBLANES) % 2 == 0:
        pitch += SUBLANES
    return pitch


def _tb_rows(m, t, pitch):
    return pl.ds(SUBLANES * m * pitch + t, SUBLANES, stride=pitch)


def _store_bt(slab_ref, j, val, nb, tl, pitch):
    if pitch == tl:
        slab_ref[j] = val
    else:
        for b in range(nb):
            slab_ref[j, b * pitch:b * pitch + tl, :] = val[b * tl:(b + 1) * tl]


def _load_bt(slab_ref, j, nb, tl, pitch):
    if pitch == tl:
        return slab_ref[j]
    return jnp.concatenate([slab_ref[j, b * pitch:b * pitch + tl, :] for b in range(nb)], axis=0)


def _gather_tb(slab_ref, j, nb, tl, pitch):
    pieces = [slab_ref[j, _tb_rows(m, t, pitch), :]
              for t in range(tl) for m in range(nb // SUBLANES)]
    return jnp.concatenate(pieces, axis=0)


def _prep_kernel(lre_ref, lim_ref, ldt_ref, btr_ref, bti_ref, cgr_ref, cgi_ref,
                 sin_ref, sout_ref, sdir_ref, l2re_ref, l2im_ref):
    gp = N_GROUPS * GROUP_P
    abt = (((1,), (1,)), ((), ()))

    def zoh(lr, li, ldt):
        dt = jnp.exp(ldt)
        mag = jnp.exp(lr * dt)
        ang = li * dt
        return mag * jnp.cos(ang), mag * jnp.sin(ang)

    def per_channel(a):
        wide = jnp.broadcast_to(a[:, None, :], (N_GROUPS, GROUP_P, a.shape[-1]))
        return wide.reshape(gp, a.shape[-1])

    def per_block(a):
        return jnp.tile(a, (1, GROUPS_PER_BLOCK))

    lr, li = lre_ref[...], lim_ref[...]
    ar, ai = zoh(lr, li, ldt_ref[...])
    a_r, a_i = ar, ai
    nr, ni = ar - 1.0, ai
    den = lr * lr + li * li
    cr = per_channel(per_block((nr * lr + ni * li) / den))
    ci = per_channel(per_block((ni * lr - nr * li) / den))
    ar, ai = per_channel(per_block(ar)), per_channel(per_block(ai))
    shape = ar.shape
    row_g = lax.broadcasted_iota(jnp.int32, shape, 0) // GROUP_P % GROUPS_PER_BLOCK
    col_g = lax.broadcasted_iota(jnp.int32, shape, 1) // N_STATE
    keep = row_g == col_g
    br, bi = per_block(btr_ref[...]), per_block(bti_ref[...])
    bbr = jnp.where(keep, cr * br - ci * bi, 0.0)
    bbi = jnp.where(keep, cr * bi + ci * br, 0.0)
    lbr = ar * bbr - ai * bbi
    lbi = ar * bbi + ai * bbr
    c0r = jnp.where(keep, per_block(cgr_ref[...]), 0.0)
    c0i = jnp.where(keep, per_block(cgi_ref[...]), 0.0)
    c1r = ar * c0r - ai * c0i
    c1i = ar * c0i + ai * c0r
    c2r = ar * c1r - ai * c1i
    c2i = ar * c1i + ai * c1r

    for k in range(N_BLOCKS):
        rs = slice(LANES * k, LANES * (k + 1))
        b_k = jnp.concatenate([bbr[rs], bbi[rs]], axis=1)
        lb_k = jnp.concatenate([lbr[rs], lbi[rs]], axis=1)
        sin_ref[k, 0:LANES, :] = lb_k.astype(BF16)
        sin_ref[k, LANES:2 * LANES, :] = b_k.astype(BF16)
        c_k = jnp.concatenate([c0r[rs], -c0i[rs]], axis=1)
        c12_k = jnp.concatenate([jnp.concatenate([c1r[rs], -c1i[rs]], axis=1),
                                 jnp.concatenate([c2r[rs], -c2i[rs]], axis=1)], axis=0)
        sout_ref[k] = c12_k.T.astype(BF16)
        c_bf = c_k.astype(BF16)
        cb = lax.dot_general(b_k.astype(BF16), c_bf, abt, preferred_element_type=F32)
        clb = lax.dot_general(lb_k.astype(BF16), c_bf, abt, preferred_element_type=F32)
        sdir_ref[k, 0:LANES, 0:LANES] = cb.astype(BF16)
        sdir_ref[k, 0:LANES, LANES:2 * LANES] = clb.astype(BF16)
        sdir_ref[k, LANES:2 * LANES, 0:LANES] = jnp.zeros((LANES, LANES), BF16)
        sdir_ref[k, LANES:2 * LANES, LANES:2 * LANES] = cb.astype(BF16)

    def as_row(a):
        wide = jnp.tile(a, (1, N_GROUPS))
        own = (lax.broadcasted_iota(jnp.int32, wide.shape, 1) // N_STATE
               == lax.broadcasted_iota(jnp.int32, wide.shape, 0))
        return jnp.sum(jnp.where(own, wide, 0.0), axis=0, keepdims=True)

    l2re_ref[...] = jnp.broadcast_to(as_row(a_r * a_r - a_i * a_i), l2re_ref.shape)
    l2im_ref[...] = jnp.broadcast_to(as_row(2.0 * a_r * a_i), l2im_ref.shape)


def _layer_kernel(nb, tl, n_steps, alpha, conv_steps, zero_state, x_ref, *refs):
    if zero_state:
        refs = (None, None, None) + refs
    _layer_body(nb, tl, n_steps, alpha, conv_steps, x_ref, *refs)


def _layer_body(nb, tl, n_steps, alpha, conv_steps,
                x_ref, h0re_ref, h0im_ref, cbuf0_ref, win_ref, bin_ref, l2re_ref, l2im_ref,
                sin_ref, sout_ref, sdir_ref, dskip_ref, wglu_ref, bglu_ref, wdw_ref, bdw_ref,
                gcl_ref, bcl_ref, wpw_ref, bpw_ref, wout_ref, gpost_ref, bpost_ref,
                y_ref, hre_out_ref, him_out_ref, cbuf_ref,
                z_ref, bu_ref, slab_ref, full_ref, mix_ref, hre_ref, him_ref):
    state_t = nb % LANES == 0
    rows = nb * tl
    hist = (CONV_K - 1) * nb
    pitch = _slab_pitch(tl)
    nsub = nb // SUBLANES
    c0 = 2 * W_SSM

    def init_state():
        if h0re_ref is None:
            hre_ref[...] = jnp.zeros_like(hre_ref)
            him_ref[...] = jnp.zeros_like(him_ref)
            full_ref[0:hist, :] = jnp.zeros((hist, W_CONV), F32)
        else:
            hre_ref[...] = h0re_ref[...].T if state_t else h0re_ref[...]
            him_ref[...] = h0im_ref[...].T if state_t else h0im_ref[...]
            full_ref[0:hist, :] = cbuf0_ref[...]

    if n_steps > 1:
        pl.when(pl.program_id(0) == 0)(init_state)
    else:
        init_state()

    half = rows // 2

    def s5_block(k):
        cs = slice(LANES * k, LANES * (k + 1))
        ss = slice(BLOCK_STATE * k, BLOCK_STATE * (k + 1))
        pieces = {(t, m): slab_ref[k, _tb_rows(m, t, pitch), :]
                  for t in range(tl) for m in range(nsub)}
        u_pair = jnp.concatenate(
            [jnp.concatenate([pieces[2 * j, m], pieces[2 * j + 1, m]], axis=1)
             for j in range(tl // 2) for m in range(nsub)], axis=0)
        u_bf = u_pair.astype(BF16)
        bu_ref[0:nb, 0:BLOCK_STATE] = hre_ref[:, ss]
        bu_ref[0:nb, BLOCK_STATE:2 * BLOCK_STATE] = him_ref[:, ss]
        bu_ref[nb:nb + half, :] = jnp.dot(u_bf, sin_ref[k], preferred_element_type=F32)
        lr = l2re_ref[:, ss]
        li = l2im_ref[:, ss]
        for m in range(nsub):
            rs = slice(SUBLANES * m, SUBLANES * (m + 1))
            hr, hi = bu_ref[rs, 0:BLOCK_STATE], bu_ref[rs, BLOCK_STATE:2 * BLOCK_STATE]
            for j in range(tl // 2):
                row = slice((j + 1) * nb + SUBLANES * m, (j + 1) * nb + SUBLANES * (m + 1))
                br = bu_ref[row, 0:BLOCK_STATE]
                bi = bu_ref[row, BLOCK_STATE:2 * BLOCK_STATE]
                hr, hi = lr * hr - li * hi + br, lr * hi + li * hr + bi
                bu_ref[row, 0:BLOCK_STATE] = hr
                bu_ref[row, BLOCK_STATE:2 * BLOCK_STATE] = hi
            hre_ref[rs, ss] = hr
            him_ref[rs, ss] = hi
        dsk = dskip_ref[:, cs]
        y_pair = (jnp.dot(bu_ref[0:half, :].astype(BF16), sout_ref[k], preferred_element_type=F32)
                  + jnp.dot(u_bf, sdir_ref[k], preferred_element_type=F32)
                  + u_pair * jnp.concatenate([dsk, dsk], axis=1))
        for j in range(tl // 2):
            for m in range(nsub):
                r0 = j * nb + SUBLANES * m
                slab_ref[k, _tb_rows(m, 2 * j, pitch), :] = y_pair[r0:r0 + SUBLANES, 0:LANES]
                slab_ref[k, _tb_rows(m, 2 * j + 1, pitch), :] = (
                    y_pair[r0:r0 + SUBLANES, LANES:2 * LANES])

    xb = x_ref[...].reshape(rows, D_MODEL).astype(BF16)

    def in_proj(lo, hi):
        z_ref[:, lo:hi] = (jnp.dot(xb, win_ref[:, lo:hi], preferred_element_type=F32)
                           + bin_ref[:, lo:hi])

    def s5_branch():
        ys = jnp.concatenate([_load_bt(slab_ref, j, nb, tl, pitch) for j in range(SSM_TILES)],
                             axis=1)
        sg = jax.nn.gelu(ys)
        in_proj(W_SSM, c0)
        glu = jnp.dot(sg.astype(BF16), wglu_ref[...], preferred_element_type=F32) + bglu_ref[...]
        s = sg * jax.nn.sigmoid(glu) * jax.nn.silu(z_ref[:, W_SSM:c0])
        mix_ref[:, 0:W_SSM] = s.astype(BF16)

    cw = 2 * LANES
    for h in range(CONV_TILES // 2):
        a_lo, b_lo = c0 + h * cw, c0 + W_CONV + h * cw
        in_proj(a_lo, a_lo + cw)
        in_proj(b_lo, b_lo + cw)
        v = z_ref[:, a_lo:a_lo + cw] * jax.nn.sigmoid(z_ref[:, b_lo:b_lo + cw])
        for j in range(2):
            _store_bt(slab_ref, SSM_TILES + 2 * h + j, v[:, LANES * j:LANES * (j + 1)],
                      nb, tl, pitch)
        if h == 0:
            in_proj(0, W_SSM)
            for j in range(SSM_TILES):
                _store_bt(slab_ref, j, z_ref[:, LANES * j:LANES * (j + 1)], nb, tl, pitch)
            for k in range(N_BLOCKS):
                s5_block(k)
            s5_branch()

    for c in range(CONV_TILES):
        lanes = slice(LANES * c, LANES * (c + 1))
        full_ref[hist:hist + rows, lanes] = _gather_tb(slab_ref, SSM_TILES + c, nb, tl, pitch)
    for c in range(CONV_TILES):
        lanes = slice(LANES * c, LANES * (c + 1))
        w = [wdw_ref[tap, :, lanes] for tap in range(CONV_K)]
        for m in range(nsub):
            for t0 in range(0, tl, conv_steps):
                base = t0 * nb + SUBLANES * m
                accs = [None] * conv_steps
                for j in range(conv_steps + CONV_K - 1):
                    xj = full_ref[base + j * nb:base + j * nb + SUBLANES, lanes]
                    for r in range(conv_steps):
                        tap = j - r
                        if 0 <= tap < CONV_K:
                            term = w[tap] * xj
                            accs[r] = term if accs[r] is None else accs[r] + term
                for r in range(conv_steps):
                    slab_ref[SSM_TILES + c, _tb_rows(m, t0 + r, pitch), :] = accs[r]
    cbuf_ref[...] = full_ref[rows:rows + hist, :]
    if n_steps > 1:
        full_ref[0:hist, :] = full_ref[rows:rows + hist, :]
    cv = jnp.concatenate([_load_bt(slab_ref, SSM_TILES + j, nb, tl, pitch)
                          for j in range(CONV_TILES)], axis=1) + bdw_ref[...]
    act = jax.nn.silu(_layernorm(cv, gcl_ref[...], bcl_ref[...]))
    in_proj(c0 + 2 * W_CONV, IN_COLS)
    cpw = jnp.dot(act.astype(BF16), wpw_ref[...], preferred_element_type=F32) + bpw_ref[...]
    cg = cpw * jax.nn.silu(z_ref[:, c0 + 2 * W_CONV:IN_COLS])
    mix_ref[:, W_SSM:W_SSM + W_CONV] = cg.astype(BF16)

    x2 = x_ref[...].reshape(rows, D_MODEL)
    n_chunks = max(1, rows // OUT_CHUNK_ROWS)
    rc = rows // n_chunks
    for r in range(n_chunks):
        rsl = slice(r * rc, (r + 1) * rc)
        mix = jnp.dot(mix_ref[rsl, :], wout_ref[...], preferred_element_type=F32)
        out = _layernorm(alpha * x2[rsl] + mix, gpost_ref[...], bpost_ref[...])
        if len(y_ref.shape) == 3:
            bsl = slice(r * (nb // n_chunks), (r + 1) * (nb // n_chunks))
            y_ref[bsl] = out.reshape(nb // n_chunks, tl, D_MODEL)
        else:
            y_ref[rsl, :] = out

    def write_state():
        if state_t:
            hre_out_ref[...] = hre_ref[...].T
            him_out_ref[...] = him_ref[...].T
        else:
            for g in range(N_GROUPS):
                gs = slice(g * N_STATE, (g + 1) * N_STATE)
                hre_out_ref[pl.ds(g, nb, stride=N_GROUPS), :] = hre_ref[:, gs]
                him_out_ref[pl.ds(g, nb, stride=N_GROUPS), :] = him_ref[:, gs]

    if n_steps > 1:
        pl.when(pl.program_id(0) == n_steps - 1)(write_state)
    else:
        write_state()


def _layer_call(x, state, consts, *, nb, tl, n_steps, alpha, conv_steps, name):
    rows = nb * tl
    hist = (CONV_K - 1) * nb
    pitch = _slab_pitch(tl)
    assert nb % SUBLANES == 0 and tl % conv_steps == 0 and (nb * pitch) % SUBLANES == 0
    assert tl % 2 == 0 and (n_steps == 1 or (tl >= CONV_K - 1 and tl % SUBLANES == 0))
    state_t = nb % LANES == 0
    state_shape = (N_STATE_ALL, nb) if state_t else (nb * N_GROUPS, N_STATE)
    if state_t and state is not None:
        state = (state[0].T, state[1].T, state[2])
    out_shape = (
        jax.ShapeDtypeStruct(x.shape, F32),
        jax.ShapeDtypeStruct(state_shape, F32),
        jax.ShapeDtypeStruct(state_shape, F32),
        jax.ShapeDtypeStruct((hist, W_CONV), F32),
    )
    scratch = [
        pltpu.VMEM((rows, IN_COLS), F32),
        pltpu.VMEM((rows // 2 + nb, 2 * BLOCK_STATE), F32),
        pltpu.VMEM((SSM_TILES + CONV_TILES, nb * pitch, LANES), F32),
        pltpu.VMEM((hist + rows, W_CONV), F32),
        pltpu.VMEM((rows, W_SSM + W_CONV), BF16),
    ]
    scratch += [pltpu.VMEM((nb, N_STATE_ALL), F32)] * 2
    operands = (x,) + (() if state is None else tuple(state)) + tuple(consts)
    kernel = functools.partial(_layer_kernel, nb, tl, n_steps, alpha, conv_steps, state is None)

    def untransposed(outs):
        y, hre, him, cbuf = outs
        if state_t:
            return y, hre.T, him.T, cbuf
        return y, hre.reshape(nb, N_STATE_ALL), him.reshape(nb, N_STATE_ALL), cbuf

    if n_steps == 1:
        assert x.shape == (nb, tl, D_MODEL)
        return untransposed(pl.pallas_call(
            kernel, out_shape=out_shape, scratch_shapes=scratch, name=name,
            compiler_params=pltpu.CompilerParams(vmem_limit_bytes=VMEM_LIMIT_BYTES),
        )(*operands))

    assert x.shape == (nb, tl * n_steps, D_MODEL)

    def whole(a):
        zeros = (0,) * a.ndim
        return pl.BlockSpec(a.shape, lambda i: zeros, pipeline_mode=pl.Buffered(1))

    x_spec = pl.BlockSpec((nb, tl, D_MODEL), lambda i: (0, i, 0))
    in_specs = [x_spec] + [whole(a) for a in operands[1:]]
    out_specs = (
        x_spec,
        pl.BlockSpec(state_shape, lambda i: (0, 0)),
        pl.BlockSpec(state_shape, lambda i: (0, 0)),
        pl.BlockSpec((hist, W_CONV), lambda i: (0, 0)),
    )
    return untransposed(pl.pallas_call(
        kernel, out_shape=out_shape, grid=(n_steps,), in_specs=in_specs, out_specs=out_specs,
        scratch_shapes=scratch, name=name,
        compiler_params=pltpu.CompilerParams(dimension_semantics=("arbitrary",),
                                             vmem_limit_bytes=VMEM_LIMIT_BYTES),
    )(*operands))


def _prep_call(lam_re, lam_im, log_dt, b_re, b_im, c_re, c_im):
    gp = N_GROUPS * GROUP_P
    ldt_n = jnp.broadcast_to(log_dt[:, None], (N_GROUPS, N_STATE))
    lam = (lam_re, lam_im, ldt_n)
    bt = tuple(b.transpose(0, 2, 1).reshape(gp, N_STATE) for b in (b_re, b_im))
    cg = tuple(c.reshape(gp, N_STATE) for c in (c_re, c_im))
    out_shape = (
        jax.ShapeDtypeStruct((N_BLOCKS, 2 * LANES, 2 * BLOCK_STATE), BF16),
        jax.ShapeDtypeStruct((N_BLOCKS, 2 * BLOCK_STATE, 2 * LANES), BF16),
        jax.ShapeDtypeStruct((N_BLOCKS, 2 * LANES, 2 * LANES), BF16),
        jax.ShapeDtypeStruct((SUBLANES, N_STATE_ALL), F32),
        jax.ShapeDtypeStruct((SUBLANES, N_STATE_ALL), F32),
    )
    return pl.pallas_call(_prep_kernel, out_shape=out_shape, name="s5_prep")(*lam, *bt, *cg)


def kernel(x_prompt, x_sample, state_ssm_re, state_ssm_im, state_conv, w_in, b_in, lam_re, lam_im, log_dt, b_re, b_im, c_re, c_im, d_skip, w_glu, b_glu, w_dw, b_dw, g_conv_ln, b_conv_ln, w_pw2, b_pw2, w_out, g_post, b_post):
    depth = w_in.shape[0]
    alpha = (2.0 * depth) ** 0.25
    bsz, seq, _ = x_prompt.shape
    dbsz, dseq, _ = x_sample.shape
    tl_prompt = 128

    hp = x_prompt
    hs = x_sample
    outs = {k: [] for k in ("re_p", "im_p", "cv_p", "re_s", "im_s", "cv_s")}
    row = lambda a: a.reshape(1, -1)
    for layer in range(depth):
        s_in, s_out, s_dir, l2re, l2im = _prep_call(
            lam_re[layer], lam_im[layer], log_dt[layer], b_re[layer], b_im[layer],
            c_re[layer], c_im[layer])
        wdw8 = jnp.broadcast_to(w_dw[layer][:, None, :], (CONV_K, SUBLANES, W_CONV))
        consts = (w_in[layer].astype(BF16), row(b_in[layer]), l2re, l2im, s_in, s_out, s_dir,
                  row(d_skip[layer]), w_glu[layer].astype(BF16), row(b_glu[layer]), wdw8,
                  row(b_dw[layer]), row(g_conv_ln[layer]), row(b_conv_ln[layer]),
                  w_pw2[layer].astype(BF16), row(b_pw2[layer]), w_out[layer].astype(BF16),
                  row(g_post[layer]), row(b_post[layer]))

        hp, re_p, im_p, cv_p = _layer_call(
            hp, None, consts, nb=bsz, tl=tl_prompt,
            n_steps=seq // tl_prompt, alpha=alpha, conv_steps=8, name="layer_prompt")

        h0re = state_ssm_re[layer].reshape(dbsz, N_STATE_ALL)
        h0im = state_ssm_im[layer].reshape(dbsz, N_STATE_ALL)
        cbuf0 = state_conv[layer].transpose(1, 0, 2).reshape((CONV_K - 1) * dbsz, W_CONV)
        hs, re_s, im_s, cv_s = _layer_call(
            hs, (h0re, h0im, cbuf0), consts, nb=dbsz, tl=dseq, n_steps=1, alpha=alpha,
            conv_steps=dseq, name="layer_sample")

        unrow = lambda a, n: a.reshape(CONV_K - 1, n, W_CONV).transpose(1, 0, 2)
        outs["re_p"].append(re_p.reshape(bsz, N_GROUPS, N_STATE))
        outs["im_p"].append(im_p.reshape(bsz, N_GROUPS, N_STATE))
        outs["cv_p"].append(unrow(cv_p, bsz))
        outs["re_s"].append(re_s.reshape(dbsz, N_GROUPS, N_STATE))
        outs["im_s"].append(im_s.reshape(dbsz, N_GROUPS, N_STATE))
        outs["cv_s"].append(unrow(cv_s, dbsz))

    return (hp, hs,
            jnp.stack(outs["re_p"]), jnp.stack(outs["im_p"]), jnp.stack(outs["cv_p"]),
            jnp.stack(outs["re_s"]), jnp.stack(outs["im_s"]), jnp.stack(outs["cv_s"]))
```

```python
import functools

import jax
import jax.numpy as jnp
from jax import lax
from jax.experimental import pallas as pl
from jax.experimental.pallas import tpu as pltpu

D_MODEL = 1024
W_SSM = 512
W_CONV = 512
GROUP_P = 16
N_GROUPS = 32
N_STATE = 64
CONV_K = 31
IN_COLS = 2 * W_SSM + 3 * W_CONV
LN_EPS = 1e-5

LANES = 128
SUBLANES = 8
GROUPS_PER_BLOCK = LANES // GROUP_P
N_BLOCKS = N_GROUPS // GROUPS_PER_BLOCK
BLOCK_STATE = GROUPS_PER_BLOCK * N_STATE
N_STATE_ALL = N_GROUPS * N_STATE
SSM_TILES = W_SSM // LANES
CONV_TILES = W_CONV // LANES
OUT_CHUNK_ROWS = 256
VMEM_LIMIT_BYTES = 58 * 1024 * 1024

F32 = jnp.float32
BF16 = jnp.bfloat16


def _layernorm(x, g, b):
    mu = jnp.mean(x, axis=-1, keepdims=True)
    xc = x - mu
    var = jnp.mean(xc * xc, axis=-1, keepdims=True)
    return xc * lax.rsqrt(var + LN_EPS) * g + b


def _slab_pitch(tl):
    if tl % SUBLANES:
        return tl
    pitch = tl
    while (pitch // SUBLANES) % 2 == 0:
        pitch += SUBLANES
    return pitch


def _tb_rows(m, t, pitch):
    return pl.ds(SUBLANES * m * pitch + t, SUBLANES, stride=pitch)


def _store_bt(slab_ref, j, val, nb, tl, pitch):
    if pitch == tl:
        slab_ref[j] = val
    else:
        for b in range(nb):
            slab_ref[j, b * pitch:b * pitch + tl, :] = val[b * tl:(b + 1) * tl]


def _load_bt(slab_ref, j, nb, tl, pitch):
    if pitch == tl:
        return slab_ref[j]
    return jnp.concatenate([slab_ref[j, b * pitch:b * pitch + tl, :] for b in range(nb)], axis=0)


def _gather_tb(slab_ref, j, nb, tl, pitch):
    pieces = [slab_ref[j, _tb_rows(m, t, pitch), :]
              for t in range(tl) for m in range(nb // SUBLANES)]
    return jnp.concatenate(pieces, axis=0)


def _prep_kernel(lre_ref, lim_ref, ldt_ref, btr_ref, bti_ref, cgr_ref, cgi_ref,
                 sin_ref, sout_ref, sdir_ref, l2re_ref, l2im_ref):
    gp = N_GROUPS * GROUP_P
    abt = (((1,), (1,)), ((), ()))

    def zoh(lr, li, ldt):
        dt = jnp.exp(ldt)
        mag = jnp.exp(lr * dt)
        ang = li * dt
        return mag * jnp.cos(ang), mag * jnp.sin(ang)

    def per_channel(a):
        wide = jnp.broadcast_to(a[:, None, :], (N_GROUPS, GROUP_P, a.shape[-1]))
        return wide.reshape(gp, a.shape[-1])

    def per_block(a):
        return jnp.tile(a, (1, GROUPS_PER_BLOCK))

    lr, li = lre_ref[...], lim_ref[...]
    ar, ai = zoh(lr, li, ldt_ref[...])
    a_r, a_i = ar, ai
    nr, ni = ar - 1.0, ai
    den = lr * lr + li * li
    cr = per_channel(per_block((nr * lr + ni * li) / den))
    ci = per_channel(per_block((ni * lr - nr * li) / den))
    ar, ai = per_channel(per_block(ar)), per_channel(per_block(ai))
    shape = ar.shape
    row_g = lax.broadcasted_iota(jnp.int32, shape, 0) // GROUP_P % GROUPS_PER_BLOCK
    col_g = lax.broadcasted_iota(jnp.int32, shape, 1) // N_STATE
    keep = row_g == col_g
    br, bi = per_block(btr_ref[...]), per_block(bti_ref[...])
    bbr = jnp.where(keep, cr * br - ci * bi, 0.0)
    bbi = jnp.where(keep, cr * bi + ci * br, 0.0)
    lbr = ar * bbr - ai * bbi
    lbi = ar * bbi + ai * bbr
    c0r = jnp.where(keep, per_block(cgr_ref[...]), 0.0)
    c0i = jnp.where(keep, per_block(cgi_ref[...]), 0.0)
    c1r = ar * c0r - ai * c0i
    c1i = ar * c0i + ai * c0r
    c2r = ar * c1r - ai * c1i
    c2i = ar * c1i + ai * c1r

    for k in range(N_BLOCKS):
        rs = slice(LANES * k, LANES * (k + 1))
        b_k = jnp.concatenate([bbr[rs], bbi[rs]], axis=1)
        lb_k = jnp.concatenate([lbr[rs], lbi[rs]], axis=1)
        sin_ref[k, 0:LANES, :] = lb_k.astype(BF16)
        sin_ref[k, LANES:2 * LANES, :] = b_k.astype(BF16)
        c_k = jnp.concatenate([c0r[rs], -c0i[rs]], axis=1)
        c12_k = jnp.concatenate([jnp.concatenate([c1r[rs], -c1i[rs]], axis=1),
                                 jnp.concatenate([c2r[rs], -c2i[rs]], axis=1)], axis=0)
        sout_ref[k] = c12_k.T.astype(BF16)
        c_bf = c_k.astype(BF16)
        cb = lax.dot_general(b_k.astype(BF16), c_bf, abt, preferred_element_type=F32)
        clb = lax.dot_general(lb_k.astype(BF16), c_bf, abt, preferred_element_type=F32)
        sdir_ref[k, 0:LANES, 0:LANES] = cb.astype(BF16)
        sdir_ref[k, 0:LANES, LANES:2 * LANES] = clb.astype(BF16)
        sdir_ref[k, LANES:2 * LANES, 0:LANES] = jnp.zeros((LANES, LANES), BF16)
        sdir_ref[k, LANES:2 * LANES, LANES:2 * LANES] = cb.astype(BF16)

    def as_row(a):
        wide = jnp.tile(a, (1, N_GROUPS))
        own = (lax.broadcasted_iota(jnp.int32, wide.shape, 1) // N_STATE
               == lax.broadcasted_iota(jnp.int32, wide.shape, 0))
        return jnp.sum(jnp.where(own, wide, 0.0), axis=0, keepdims=True)

    l2re_ref[...] = jnp.broadcast_to(as_row(a_r * a_r - a_i * a_i), l2re_ref.shape)
    l2im_ref[...] = jnp.broadcast_to(as_row(2.0 * a_r * a_i), l2im_ref.shape)


def _state_modes(nb, tl, n_steps, has_state):
    hist_dma = n_steps == 1 and has_state and nb * tl <= (CONV_K - 1) * nb
    return nb % LANES == 0, hist_dma


def _layer_kernel(nb, tl, n_steps, alpha, conv_steps, zero_state, x_ref, *refs):
    if zero_state:
        refs = (None, None, None) + refs
    _layer_body(nb, tl, n_steps, alpha, conv_steps, x_ref, *refs)


def _layer_body(nb, tl, n_steps, alpha, conv_steps,
                x_ref, h0re_ref, h0im_ref, cbuf0_ref, win_ref, bin_ref, l2re_ref, l2im_ref,
                sin_ref, sout_ref, sdir_ref, dskip_ref, wglu_ref, bglu_ref, wdw_ref, bdw_ref,
                gcl_ref, bcl_ref, wpw_ref, bpw_ref, wout_ref, gpost_ref, bpost_ref,
                y_ref, hre_ref, him_ref, cbuf_ref,
                z_ref, bu_ref, slab_ref, full_ref, mix_ref, *state_scratch):
    rows = nb * tl
    hist = (CONV_K - 1) * nb
    state_t, hist_dma = _state_modes(nb, tl, n_steps, cbuf0_ref is not None)
    if state_t:
        hre_out_ref, him_out_ref = hre_ref, him_ref
        hre_ref, him_ref = state_scratch[:2]
    if hist_dma:
        sem = state_scratch[-1]
        hist_in = pltpu.make_async_copy(cbuf0_ref, full_ref.at[0:hist], sem.at[0])
        hist_keep = pltpu.make_async_copy(cbuf0_ref.at[rows:hist], cbuf_ref.at[0:hist - rows], sem.at[1])
        hist_new = pltpu.make_async_copy(full_ref.at[hist:hist + rows], cbuf_ref.at[hist - rows:hist],
                                         sem.at[2])
    pitch = _slab_pitch(tl)
    nsub = nb // SUBLANES
    c0 = 2 * W_SSM

    def init_state():
        if h0re_ref is None:
            hre_ref[...] = jnp.zeros_like(hre_ref)
            him_ref[...] = jnp.zeros_like(him_ref)
            full_ref[0:hist, :] = jnp.zeros((hist, W_CONV), F32)
        else:
            hre_ref[...] = h0re_ref[...].T if state_t else h0re_ref[...]
            him_ref[...] = h0im_ref[...].T if state_t else h0im_ref[...]
            if hist_dma:
                hist_in.start()
                hist_keep.start()
            else:
                full_ref[0:hist, :] = cbuf0_ref[...]

    if n_steps > 1:
        pl.when(pl.program_id(0) == 0)(init_state)
    else:
        init_state()

    half = rows // 2

    def s5_block(k):
        cs = slice(LANES * k, LANES * (k + 1))
        ss = slice(BLOCK_STATE * k, BLOCK_STATE * (k + 1))
        pieces = {(t, m): slab_ref[k, _tb_rows(m, t, pitch), :]
                  for t in range(tl) for m in range(nsub)}
        u_pair = jnp.concatenate(
            [jnp.concatenate([pieces[2 * j, m], pieces[2 * j + 1, m]], axis=1)
             for j in range(tl // 2) for m in range(nsub)], axis=0)
        u_bf = u_pair.astype(BF16)
        bu_ref[0:nb, 0:BLOCK_STATE] = hre_ref[:, ss]
        bu_ref[0:nb, BLOCK_STATE:2 * BLOCK_STATE] = him_ref[:, ss]
        bu_ref[nb:nb + half, :] = jnp.dot(u_bf, sin_ref[k], preferred_element_type=F32)
        lr = l2re_ref[:, ss]
        li = l2im_ref[:, ss]
        for m in range(nsub):
            rs = slice(SUBLANES * m, SUBLANES * (m + 1))
            hr, hi = bu_ref[rs, 0:BLOCK_STATE], bu_ref[rs, BLOCK_STATE:2 * BLOCK_STATE]
            for j in range(tl // 2):
                row = slice((j + 1) * nb + SUBLANES * m, (j + 1) * nb + SUBLANES * (m + 1))
                br = bu_ref[row, 0:BLOCK_STATE]
                bi = bu_ref[row, BLOCK_STATE:2 * BLOCK_STATE]
                hr, hi = lr * hr - li * hi + br, lr * hi + li * hr + bi
                bu_ref[row, 0:BLOCK_STATE] = hr
                bu_ref[row, BLOCK_STATE:2 * BLOCK_STATE] = hi
            hre_ref[rs, ss] = hr
            him_ref[rs, ss] = hi
        dsk = dskip_ref[:, cs]
        y_pair = (jnp.dot(bu_ref[0:half, :].astype(BF16), sout_ref[k], preferred_element_type=F32)
                  + jnp.dot(u_bf, sdir_ref[k], preferred_element_type=F32)
                  + u_pair * jnp.concatenate([dsk, dsk], axis=1))
        for j in range(tl // 2):
            for m in range(nsub):
                r0 = j * nb + SUBLANES * m
                slab_ref[k, _tb_rows(m, 2 * j, pitch), :] = y_pair[r0:r0 + SUBLANES, 0:LANES]
                slab_ref[k, _tb_rows(m, 2 * j + 1, pitch), :] = (
                    y_pair[r0:r0 + SUBLANES, LANES:2 * LANES])

    xb = x_ref[...].reshape(rows, D_MODEL).astype(BF16)

    def in_proj(lo, hi):
        z_ref[:, lo:hi] = (jnp.dot(xb, win_ref[:, lo:hi], preferred_element_type=F32)
                           + bin_ref[:, lo:hi])

    def s5_branch():
        ys = jnp.concatenate([_load_bt(slab_ref, j, nb, tl, pitch) for j in range(SSM_TILES)],
                             axis=1)
        sg = jax.nn.gelu(ys)
        in_proj(W_SSM, c0)
        glu = jnp.dot(sg.astype(BF16), wglu_ref[...], preferred_element_type=F32) + bglu_ref[...]
        s = sg * jax.nn.sigmoid(glu) * jax.nn.silu(z_ref[:, W_SSM:c0])
        mix_ref[:, 0:W_SSM] = s.astype(BF16)

    cw = 2 * LANES
    for h in range(CONV_TILES // 2):
        a_lo, b_lo = c0 + h * cw, c0 + W_CONV + h * cw
        in_proj(a_lo, a_lo + cw)
        in_proj(b_lo, b_lo + cw)
        v = z_ref[:, a_lo:a_lo + cw] * jax.nn.sigmoid(z_ref[:, b_lo:b_lo + cw])
        for j in range(2):
            _store_bt(slab_ref, SSM_TILES + 2 * h + j, v[:, LANES * j:LANES * (j + 1)],
                      nb, tl, pitch)
        if h == 0:
            in_proj(0, W_SSM)
            for j in range(SSM_TILES):
                _store_bt(slab_ref, j, z_ref[:, LANES * j:LANES * (j + 1)], nb, tl, pitch)
            for k in range(N_BLOCKS):
                s5_block(k)
            s5_branch()

    for c in range(CONV_TILES):
        lanes = slice(LANES * c, LANES * (c + 1))
        full_ref[hist:hist + rows, lanes] = _gather_tb(slab_ref, SSM_TILES + c, nb, tl, pitch)
    if hist_dma:
        hist_in.wait()
        hist_new.start()
    for c in range(CONV_TILES):
        lanes = slice(LANES * c, LANES * (c + 1))
        w = [wdw_ref[tap, :, lanes] for tap in range(CONV_K)]
        for m in range(nsub):
            for t0 in range(0, tl, conv_steps):
                base = t0 * nb + SUBLANES * m
                accs = [None] * conv_steps
                for j in range(conv_steps + CONV_K - 1):
                    xj = full_ref[base + j * nb:base + j * nb + SUBLANES, lanes]
                    for r in range(conv_steps):
                        tap = j - r
                        if 0 <= tap < CONV_K:
                            term = w[tap] * xj
                            accs[r] = term if accs[r] is None else accs[r] + term
                for r in range(conv_steps):
                    slab_ref[SSM_TILES + c, _tb_rows(m, t0 + r, pitch), :] = accs[r]
    if not hist_dma:
        cbuf_ref[...] = full_ref[rows:rows + hist, :]
    if n_steps > 1:
        full_ref[0:hist, :] = full_ref[rows:rows + hist, :]
    cv = jnp.concatenate([_load_bt(slab_ref, SSM_TILES + j, nb, tl, pitch)
                          for j in range(CONV_TILES)], axis=1) + bdw_ref[...]
    act = jax.nn.silu(_layernorm(cv, gcl_ref[...], bcl_ref[...]))
    in_proj(c0 + 2 * W_CONV, IN_COLS)
    cpw = jnp.dot(act.astype(BF16), wpw_ref[...], preferred_element_type=F32) + bpw_ref[...]
    cg = cpw * jax.nn.silu(z_ref[:, c0 + 2 * W_CONV:IN_COLS])
    mix_ref[:, W_SSM:W_SSM + W_CONV] = cg.astype(BF16)

    x2 = x_ref[...].reshape(rows, D_MODEL)
    n_chunks = max(1, rows // OUT_CHUNK_ROWS)
    rc = rows // n_chunks
    for r in range(n_chunks):
        rsl = slice(r * rc, (r + 1) * rc)
        mix = jnp.dot(mix_ref[rsl, :], wout_ref[...], preferred_element_type=F32)
        out = _layernorm(alpha * x2[rsl] + mix, gpost_ref[...], bpost_ref[...])
        if len(y_ref.shape) == 3:
            bsl = slice(r * (nb // n_chunks), (r + 1) * (nb // n_chunks))
            y_ref[bsl] = out.reshape(nb // n_chunks, tl, D_MODEL)
        else:
            y_ref[rsl, :] = out

    if state_t:
        hre_out_ref[...] = hre_ref[...].T
        him_out_ref[...] = him_ref[...].T
    if hist_dma:
        hist_keep.wait()
        hist_new.wait()


def _layer_call(x, state, consts, *, nb, tl, n_steps, alpha, conv_steps, name):
    rows = nb * tl
    hist = (CONV_K - 1) * nb
    pitch = _slab_pitch(tl)
    assert nb % SUBLANES == 0 and tl % conv_steps == 0 and (nb * pitch) % SUBLANES == 0
    assert tl % 2 == 0 and (n_steps == 1 or (tl >= CONV_K - 1 and tl % SUBLANES == 0))
    state_t, hist_dma = _state_modes(nb, tl, n_steps, state is not None)
    state_shape = (N_STATE_ALL, nb) if state_t else (nb, N_STATE_ALL)
    if state_t and state is not None:
        state = (state[0].T, state[1].T, state[2])
    out_shape = (
        jax.ShapeDtypeStruct(x.shape, F32),
        jax.ShapeDtypeStruct(state_shape, F32),
        jax.ShapeDtypeStruct(state_shape, F32),
        jax.ShapeDtypeStruct((hist, W_CONV), F32),
    )
    scratch = [
        pltpu.VMEM((rows, IN_COLS), F32),
        pltpu.VMEM((rows // 2 + nb, 2 * BLOCK_STATE), F32),
        pltpu.VMEM((SSM_TILES + CONV_TILES, nb * pitch, LANES), F32),
        pltpu.VMEM((hist + rows, W_CONV), F32),
        pltpu.VMEM((rows, W_SSM + W_CONV), BF16),
    ]
    if state_t:
        scratch += [pltpu.VMEM((nb, N_STATE_ALL), F32)] * 2
    operands = (x,) + (() if state is None else tuple(state)) + tuple(consts)
    kernel = functools.partial(_layer_kernel, nb, tl, n_steps, alpha, conv_steps, state is None)

    def untransposed(outs):
        y, hre, him, cbuf = outs
        return (y, hre.T, him.T, cbuf) if state_t else (y, hre, him, cbuf)

    if n_steps == 1:
        assert x.shape == (nb, tl, D_MODEL)
        vmem, hbm = pl.BlockSpec(memory_space=pltpu.VMEM), pl.BlockSpec(memory_space=pl.ANY)
        in_specs = [vmem] * len(operands)
        out_specs = [vmem] * len(out_shape)
        if hist_dma:
            in_specs[3] = out_specs[3] = hbm
            scratch += [pltpu.SemaphoreType.DMA((3,))]
        return untransposed(pl.pallas_call(
            kernel, out_shape=out_shape, in_specs=in_specs, out_specs=tuple(out_specs),
            scratch_shapes=scratch, name=name,
            compiler_params=pltpu.CompilerParams(vmem_limit_bytes=VMEM_LIMIT_BYTES),
        )(*operands))

    assert x.shape == (nb, tl * n_steps, D_MODEL)

    def whole(a):
        zeros = (0,) * a.ndim
        return pl.BlockSpec(a.shape, lambda i: zeros, pipeline_mode=pl.Buffered(1))

    x_spec = pl.BlockSpec((nb, tl, D_MODEL), lambda i: (0, i, 0))
    in_specs = [x_spec] + [whole(a) for a in operands[1:]]
    out_specs = (
        x_spec,
        pl.BlockSpec(state_shape, lambda i: (0, 0)),
        pl.BlockSpec(state_shape, lambda i: (0, 0)),
        pl.BlockSpec((hist, W_CONV), lambda i: (0, 0)),
    )
    return untransposed(pl.pallas_call(
        kernel, out_shape=out_shape, grid=(n_steps,), in_specs=in_specs, out_specs=out_specs,
        scratch_shapes=scratch, name=name,
        compiler_params=pltpu.CompilerParams(dimension_semantics=("arbitrary",),
                                             vmem_limit_bytes=VMEM_LIMIT_BYTES),
    )(*operands))


def _prep_call(lam_re, lam_im, log_dt, b_re, b_im, c_re, c_im):
    gp = N_GROUPS * GROUP_P
    ldt_n = jnp.broadcast_to(log_dt[:, None], (N_GROUPS, N_STATE))
    lam = (lam_re, lam_im, ldt_n)
    bt = tuple(b.transpose(0, 2, 1).reshape(gp, N_STATE) for b in (b_re, b_im))
    cg = tuple(c.reshape(gp, N_STATE) for c in (c_re, c_im))
    out_shape = (
        jax.ShapeDtypeStruct((N_BLOCKS, 2 * LANES, 2 * BLOCK_STATE), BF16),
        jax.ShapeDtypeStruct((N_BLOCKS, 2 * BLOCK_STATE, 2 * LANES), BF16),
        jax.ShapeDtypeStruct((N_BLOCKS, 2 * LANES, 2 * LANES), BF16),
        jax.ShapeDtypeStruct((SUBLANES, N_STATE_ALL), F32),
        jax.ShapeDtypeStruct((SUBLANES, N_STATE_ALL), F32),
    )
    return pl.pallas_call(_prep_kernel, out_shape=out_shape, name="s5_prep")(*lam, *bt, *cg)


def kernel(x_prompt, x_sample, state_ssm_re, state_ssm_im, state_conv, w_in, b_in, lam_re, lam_im, log_dt, b_re, b_im, c_re, c_im, d_skip, w_glu, b_glu, w_dw, b_dw, g_conv_ln, b_conv_ln, w_pw2, b_pw2, w_out, g_post, b_post):
    depth = w_in.shape[0]
    alpha = (2.0 * depth) ** 0.25
    bsz, seq, _ = x_prompt.shape
    dbsz, dseq, _ = x_sample.shape
    tl_prompt = 128

    hp = x_prompt
    hs = x_sample
    outs = {k: [] for k in ("re_p", "im_p", "cv_p", "re_s", "im_s", "cv_s")}
    row = lambda a: a.reshape(1, -1)
    for layer in range(depth):
        s_in, s_out, s_dir, l2re, l2im = _prep_call(
            lam_re[layer], lam_im[layer], log_dt[layer], b_re[layer], b_im[layer],
            c_re[layer], c_im[layer])
        wdw8 = jnp.broadcast_to(w_dw[layer][:, None, :], (CONV_K, SUBLANES, W_CONV))
        consts = (w_in[layer].astype(BF16), row(b_in[layer]), l2re, l2im, s_in, s_out, s_dir,
                  row(d_skip[layer]), w_glu[layer].astype(BF16), row(b_glu[layer]), wdw8,
                  row(b_dw[layer]), row(g_conv_ln[layer]), row(b_conv_ln[layer]),
                  w_pw2[layer].astype(BF16), row(b_pw2[layer]), w_out[layer].astype(BF16),
                  row(g_post[layer]), row(b_post[layer]))

        hp, re_p, im_p, cv_p = _layer_call(
            hp, None, consts, nb=bsz, tl=tl_prompt,
            n_steps=seq // tl_prompt, alpha=alpha, conv_steps=8, name="layer_prompt")

        h0re = state_ssm_re[layer].reshape(dbsz, N_STATE_ALL)
        h0im = state_ssm_im[layer].reshape(dbsz, N_STATE_ALL)
        cbuf0 = state_conv[layer].transpose(1, 0, 2).reshape((CONV_K - 1) * dbsz, W_CONV)
        hs, re_s, im_s, cv_s = _layer_call(
            hs, (h0re, h0im, cbuf0), consts, nb=dbsz, tl=dseq, n_steps=1, alpha=alpha,
            conv_steps=dseq, name="layer_sample")

        unrow = lambda a, n: a.reshape(CONV_K - 1, n, W_CONV).transpose(1, 0, 2)
        outs["re_p"].append(re_p.reshape(bsz, N_GROUPS, N_STATE))
        outs["im_p"].append(im_p.reshape(bsz, N_GROUPS, N_STATE))
        outs["cv_p"].append(unrow(cv_p, bsz))
        outs["re_s"].append(re_s.reshape(dbsz, N_GROUPS, N_STATE))
        outs["im_s"].append(im_s.reshape(dbsz, N_GROUPS, N_STATE))
        outs["cv_s"].append(unrow(cv_s, dbsz))

    return (hp, hs,
            jnp.stack(outs["re_p"]), jnp.stack(outs["im_p"]), jnp.stack(outs["cv_p"]),
            jnp.stack(outs["re_s"]), jnp.stack(outs["im_s"]), jnp.stack(outs["cv_s"]))
```

```python
import functools

import jax
import jax.numpy as jnp
from jax import lax
from jax.experimental import pallas as pl
from jax.experimental.pallas import tpu as pltpu

D_MODEL = 1024
W_SSM = 512
W_CONV = 512
GROUP_P = 16
N_GROUPS = 32
N_STATE = 64
CONV_K = 31
IN_COLS = 2 * W_SSM + 3 * W_CONV
LN_EPS = 1e-5

LANES = 128
SUBLANES = 8
GROUPS_PER_BLOCK = LANES // GROUP_P
N_BLOCKS = N_GROUPS // GROUPS_PER_BLOCK
BLOCK_STATE = GROUPS_PER_BLOCK * N_STATE
N_STATE_ALL = N_GROUPS * N_STATE
SSM_TILES = W_SSM // LANES
CONV_TILES = W_CONV // LANES
OUT_CHUNK_ROWS = 256
VMEM_LIMIT_BYTES = 58 * 1024 * 1024

F32 = jnp.float32
BF16 = jnp.bfloat16


def _layernorm(x, g, b):
    mu = jnp.mean(x, axis=-1, keepdims=True)
    xc = x - mu
    var = jnp.mean(xc * xc, axis=-1, keepdims=True)
    return xc * lax.rsqrt(var + LN_EPS) * g + b


def _slab_pitch(tl):
    if tl % SUBLANES:
        return tl
    pitch = tl
    while (pitch // SUBLANES) % 2 == 0:
        pitch += SUBLANES
    return pitch


def _tb_rows(m, t, pitch):
    return pl.ds(SUBLANES * m * pitch + t, SUBLANES, stride=pitch)


def _store_bt(slab_ref, j, val, nb, tl, pitch):
    if pitch == tl:
        slab_ref[j] = val
    else:
        for b in range(nb):
            slab_ref[j, b * pitch:b * pitch + tl, :] = val[b * tl:(b + 1) * tl]


def _load_bt(slab_ref, j, nb, tl, pitch):
    if pitch == tl:
        return slab_ref[j]
    return jnp.concatenate([slab_ref[j, b * pitch:b * pitch + tl, :] for b in range(nb)], axis=0)


def _gather_tb(slab_ref, j, nb, tl, pitch):
    pieces = [slab_ref[j, _tb_rows(m, t, pitch), :]
              for t in range(tl) for m in range(nb // SUBLANES)]
    return jnp.concatenate(pieces, axis=0)


def _prep_kernel(lre_ref, lim_ref, ldt_ref, btr_ref, bti_ref, cgr_ref, cgi_ref,
                 sin_ref, sout_ref, sdir_ref, l2re_ref, l2im_ref):
    gp = N_GROUPS * GROUP_P
    abt = (((1,), (1,)), ((), ()))

    def zoh(lr, li, ldt):
        dt = jnp.exp(ldt)
        mag = jnp.exp(lr * dt)
        ang = li * dt
        return mag * jnp.cos(ang), mag * jnp.sin(ang)

    def per_channel(a):
        wide = jnp.broadcast_to(a[:, None, :], (N_GROUPS, GROUP_P, a.shape[-1]))
        return wide.reshape(gp, a.shape[-1])

    def per_block(a):
        return jnp.tile(a, (1, GROUPS_PER_BLOCK))

    lr, li = lre_ref[...], lim_ref[...]
    ar, ai = zoh(lr, li, ldt_ref[...])
    a_r, a_i = ar, ai
    nr, ni = ar - 1.0, ai
    den = lr * lr + li * li
    cr = per_channel(per_block((nr * lr + ni * li) / den))
    ci = per_channel(per_block((ni * lr - nr * li) / den))
    ar, ai = per_channel(per_block(ar)), per_channel(per_block(ai))
    shape = ar.shape
    row_g = lax.broadcasted_iota(jnp.int32, shape, 0) // GROUP_P % GROUPS_PER_BLOCK
    col_g = lax.broadcasted_iota(jnp.int32, shape, 1) // N_STATE
    keep = row_g == col_g
    br, bi = per_block(btr_ref[...]), per_block(bti_ref[...])
    bbr = jnp.where(keep, cr * br - ci * bi, 0.0)
    bbi = jnp.where(keep, cr * bi + ci * br, 0.0)
    lbr = ar * bbr - ai * bbi
    lbi = ar * bbi + ai * bbr
    c0r = jnp.where(keep, per_block(cgr_ref[...]), 0.0)
    c0i = jnp.where(keep, per_block(cgi_ref[...]), 0.0)
    c1r = ar * c0r - ai * c0i
    c1i = ar * c0i + ai * c0r
    c2r = ar * c1r - ai * c1i
    c2i = ar * c1i + ai * c1r

    for k in range(N_BLOCKS):
        rs = slice(LANES * k, LANES * (k + 1))
        b_k = jnp.concatenate([bbr[rs], bbi[rs]], axis=1)
        lb_k = jnp.concatenate([lbr[rs], lbi[rs]], axis=1)
        sin_ref[k, 0:LANES, :] = lb_k.astype(BF16)
        sin_ref[k, LANES:2 * LANES, :] = b_k.astype(BF16)
        c_k = jnp.concatenate([c0r[rs], -c0i[rs]], axis=1)
        c12_k = jnp.concatenate([jnp.concatenate([c1r[rs], -c1i[rs]], axis=1),
                                 jnp.concatenate([c2r[rs], -c2i[rs]], axis=1)], axis=0)
        sout_ref[k] = c12_k.T.astype(BF16)
        c_bf = c_k.astype(BF16)
        cb = lax.dot_general(b_k.astype(BF16), c_bf, abt, preferred_element_type=F32)
        clb = lax.dot_general(lb_k.astype(BF16), c_bf, abt, preferred_element_type=F32)
        sdir_ref[k, 0:LANES, 0:LANES] = cb.astype(BF16)
        sdir_ref[k, 0:LANES, LANES:2 * LANES] = clb.astype(BF16)
        sdir_ref[k, LANES:2 * LANES, 0:LANES] = jnp.zeros((LANES, LANES), BF16)
        sdir_ref[k, LANES:2 * LANES, LANES:2 * LANES] = cb.astype(BF16)

    def as_row(a):
        wide = jnp.tile(a, (1, N_GROUPS))
        own = (lax.broadcasted_iota(jnp.int32, wide.shape, 1) // N_STATE
               == lax.broadcasted_iota(jnp.int32, wide.shape, 0))
        return jnp.sum(jnp.where(own, wide, 0.0), axis=0, keepdims=True)

    l2re_ref[...] = jnp.broadcast_to(as_row(a_r * a_r - a_i * a_i), l2re_ref.shape)
    l2im_ref[...] = jnp.broadcast_to(as_row(2.0 * a_r * a_i), l2im_ref.shape)


def _state_modes(nb, tl, n_steps, has_state):
    hist_dma = n_steps == 1 and has_state and nb * tl <= (CONV_K - 1) * nb
    return nb % LANES == 0, hist_dma


def _layer_kernel(nb, tl, n_steps, alpha, conv_steps, zero_state, x_ref, *refs):
    if zero_state:
        refs = (None, None, None) + refs
    _layer_body(nb, tl, n_steps, alpha, conv_steps, x_ref, *refs)


def _layer_body(nb, tl, n_steps, alpha, conv_steps,
                x_ref, h0re_ref, h0im_ref, cbuf0_ref, win_ref, bin_ref, l2re_ref, l2im_ref,
                sin_ref, sout_ref, sdir_ref, dskip_ref, wglu_ref, bglu_ref, wdw_ref, bdw_ref,
                gcl_ref, bcl_ref, wpw_ref, bpw_ref, wout_ref, gpost_ref, bpost_ref,
                y_ref, hre_ref, him_ref, cbuf_ref,
                z_ref, bu_ref, slab_ref, full_ref, mix_ref, *state_scratch):
    rows = nb * tl
    hist = (CONV_K - 1) * nb
    state_t, hist_dma = _state_modes(nb, tl, n_steps, cbuf0_ref is not None)
    if state_t:
        hre_out_ref, him_out_ref = hre_ref, him_ref
        hre_ref, him_ref = state_scratch[:2]
    if hist_dma:
        sem = state_scratch[-1]
        hist_in = pltpu.make_async_copy(cbuf0_ref, full_ref.at[0:hist], sem.at[0])
    pitch = _slab_pitch(tl)
    nsub = nb // SUBLANES
    c0 = 2 * W_SSM

    def init_state():
        if h0re_ref is None:
            hre_ref[...] = jnp.zeros_like(hre_ref)
            him_ref[...] = jnp.zeros_like(him_ref)
            full_ref[0:hist, :] = jnp.zeros((hist, W_CONV), F32)
        else:
            hre_ref[...] = h0re_ref[...].T if state_t else h0re_ref[...]
            him_ref[...] = h0im_ref[...].T if state_t else h0im_ref[...]
            if hist_dma:
                hist_in.start()
            else:
                full_ref[0:hist, :] = cbuf0_ref[...]

    if n_steps > 1:
        pl.when(pl.program_id(0) == 0)(init_state)
    else:
        init_state()

    half = rows // 2

    def s5_block(k):
        cs = slice(LANES * k, LANES * (k + 1))
        ss = slice(BLOCK_STATE * k, BLOCK_STATE * (k + 1))
        pieces = {(t, m): slab_ref[k, _tb_rows(m, t, pitch), :]
                  for t in range(tl) for m in range(nsub)}
        u_pair = jnp.concatenate(
            [jnp.concatenate([pieces[2 * j, m], pieces[2 * j + 1, m]], axis=1)
             for j in range(tl // 2) for m in range(nsub)], axis=0)
        u_bf = u_pair.astype(BF16)
        bu_ref[0:nb, 0:BLOCK_STATE] = hre_ref[:, ss]
        bu_ref[0:nb, BLOCK_STATE:2 * BLOCK_STATE] = him_ref[:, ss]
        bu_ref[nb:nb + half, :] = jnp.dot(u_bf, sin_ref[k], preferred_element_type=F32)
        lr = l2re_ref[:, ss]
        li = l2im_ref[:, ss]
        for m in range(nsub):
            rs = slice(SUBLANES * m, SUBLANES * (m + 1))
            hr, hi = bu_ref[rs, 0:BLOCK_STATE], bu_ref[rs, BLOCK_STATE:2 * BLOCK_STATE]
            for j in range(tl // 2):
                row = slice((j + 1) * nb + SUBLANES * m, (j + 1) * nb + SUBLANES * (m + 1))
                br = bu_ref[row, 0:BLOCK_STATE]
                bi = bu_ref[row, BLOCK_STATE:2 * BLOCK_STATE]
                hr, hi = lr * hr - li * hi + br, lr * hi + li * hr + bi
                bu_ref[row, 0:BLOCK_STATE] = hr
                bu_ref[row, BLOCK_STATE:2 * BLOCK_STATE] = hi
            hre_ref[rs, ss] = hr
            him_ref[rs, ss] = hi
        dsk = dskip_ref[:, cs]
        y_pair = (jnp.dot(bu_ref[0:half, :].astype(BF16), sout_ref[k], preferred_element_type=F32)
                  + jnp.dot(u_bf, sdir_ref[k], preferred_element_type=F32)
                  + u_pair * jnp.concatenate([dsk, dsk], axis=1))
        for j in range(tl // 2):
            for m in range(nsub):
                r0 = j * nb + SUBLANES * m
                slab_ref[k, _tb_rows(m, 2 * j, pitch), :] = y_pair[r0:r0 + SUBLANES, 0:LANES]
                slab_ref[k, _tb_rows(m, 2 * j + 1, pitch), :] = (
                    y_pair[r0:r0 + SUBLANES, LANES:2 * LANES])

    xb = x_ref[...].reshape(rows, D_MODEL).astype(BF16)

    def in_proj(lo, hi):
        z_ref[:, lo:hi] = (jnp.dot(xb, win_ref[:, lo:hi], preferred_element_type=F32)
                           + bin_ref[:, lo:hi])

    def s5_branch():
        ys = jnp.concatenate([_load_bt(slab_ref, j, nb, tl, pitch) for j in range(SSM_TILES)],
                             axis=1)
        sg = jax.nn.gelu(ys)
        in_proj(W_SSM, c0)
        glu = jnp.dot(sg.astype(BF16), wglu_ref[...], preferred_element_type=F32) + bglu_ref[...]
        s = sg * jax.nn.sigmoid(glu) * jax.nn.silu(z_ref[:, W_SSM:c0])
        mix_ref[:, 0:W_SSM] = s.astype(BF16)

    cw = 2 * LANES
    for h in range(CONV_TILES // 2):
        a_lo, b_lo = c0 + h * cw, c0 + W_CONV + h * cw
        in_proj(a_lo, a_lo + cw)
        in_proj(b_lo, b_lo + cw)
        v = z_ref[:, a_lo:a_lo + cw] * jax.nn.sigmoid(z_ref[:, b_lo:b_lo + cw])
        for j in range(2):
            _store_bt(slab_ref, SSM_TILES + 2 * h + j, v[:, LANES * j:LANES * (j + 1)],
                      nb, tl, pitch)
        if h == 0:
            in_proj(0, W_SSM)
            for j in range(SSM_TILES):
                _store_bt(slab_ref, j, z_ref[:, LANES * j:LANES * (j + 1)], nb, tl, pitch)
            for k in range(N_BLOCKS):
                s5_block(k)
            s5_branch()

    for c in range(CONV_TILES):
        lanes = slice(LANES * c, LANES * (c + 1))
        full_ref[hist:hist + rows, lanes] = _gather_tb(slab_ref, SSM_TILES + c, nb, tl, pitch)
    if hist_dma:
        hist_in.wait()
    for c in range(CONV_TILES):
        lanes = slice(LANES * c, LANES * (c + 1))
        w = [wdw_ref[tap, :, lanes] for tap in range(CONV_K)]
        for m in range(nsub):
            for t0 in range(0, tl, conv_steps):
                base = t0 * nb + SUBLANES * m
                accs = [None] * conv_steps
                for j in range(conv_steps + CONV_K - 1):
                    xj = full_ref[base + j * nb:base + j * nb + SUBLANES, lanes]
                    for r in range(conv_steps):
                        tap = j - r
                        if 0 <= tap < CONV_K:
                            term = w[tap] * xj
                            accs[r] = term if accs[r] is None else accs[r] + term
                for r in range(conv_steps):
                    slab_ref[SSM_TILES + c, _tb_rows(m, t0 + r, pitch), :] = accs[r]
    cbuf_ref[...] = full_ref[rows:rows + hist, :]
    if n_steps > 1:
        full_ref[0:hist, :] = full_ref[rows:rows + hist, :]
    cv = jnp.concatenate([_load_bt(slab_ref, SSM_TILES + j, nb, tl, pitch)
                          for j in range(CONV_TILES)], axis=1) + bdw_ref[...]
    act = jax.nn.silu(_layernorm(cv, gcl_ref[...], bcl_ref[...]))
    in_proj(c0 + 2 * W_CONV, IN_COLS)
    cpw = jnp.dot(act.astype(BF16), wpw_ref[...], preferred_element_type=F32) + bpw_ref[...]
    cg = cpw * jax.nn.silu(z_ref[:, c0 + 2 * W_CONV:IN_COLS])
    mix_ref[:, W_SSM:W_SSM + W_CONV] = cg.astype(BF16)

    x2 = x_ref[...].reshape(rows, D_MODEL)
    n_chunks = max(1, rows // OUT_CHUNK_ROWS)
    rc = rows // n_chunks
    for r in range(n_chunks):
        rsl = slice(r * rc, (r + 1) * rc)
        mix = jnp.dot(mix_ref[rsl, :], wout_ref[...], preferred_element_type=F32)
        out = _layernorm(alpha * x2[rsl] + mix, gpost_ref[...], bpost_ref[...])
        if len(y_ref.shape) == 3:
            bsl = slice(r * (nb // n_chunks), (r + 1) * (nb // n_chunks))
            y_ref[bsl] = out.reshape(nb // n_chunks, tl, D_MODEL)
        else:
            y_ref[rsl, :] = out

    if state_t:
        hre_out_ref[...] = hre_ref[...].T
        him_out_ref[...] = him_ref[...].T


def _layer_call(x, state, consts, *, nb, tl, n_steps, alpha, conv_steps, name):
    rows = nb * tl
    hist = (CONV_K - 1) * nb
    pitch = _slab_pitch(tl)
    assert nb % SUBLANES == 0 and tl % conv_steps == 0 and (nb * pitch) % SUBLANES == 0
    assert tl % 2 == 0 and (n_steps == 1 or (tl >= CONV_K - 1 and tl % SUBLANES == 0))
    state_t, hist_dma = _state_modes(nb, tl, n_steps, state is not None)
    state_shape = (N_STATE_ALL, nb) if state_t else (nb, N_STATE_ALL)
    if state_t and state is not None:
        state = (state[0].T, state[1].T, state[2])
    out_shape = (
        jax.ShapeDtypeStruct(x.shape, F32),
        jax.ShapeDtypeStruct(state_shape, F32),
        jax.ShapeDtypeStruct(state_shape, F32),
        jax.ShapeDtypeStruct((hist, W_CONV), F32),
    )
    scratch = [
        pltpu.VMEM((rows, IN_COLS), F32),
        pltpu.VMEM((rows // 2 + nb, 2 * BLOCK_STATE), F32),
        pltpu.VMEM((SSM_TILES + CONV_TILES, nb * pitch, LANES), F32),
        pltpu.VMEM((hist + rows, W_CONV), F32),
        pltpu.VMEM((rows, W_SSM + W_CONV), BF16),
    ]
    if state_t:
        scratch += [pltpu.VMEM((nb, N_STATE_ALL), F32)] * 2
    operands = (x,) + (() if state is None else tuple(state)) + tuple(consts)
    kernel = functools.partial(_layer_kernel, nb, tl, n_steps, alpha, conv_steps, state is None)

    def untransposed(outs):
        y, hre, him, cbuf = outs
        return (y, hre.T, him.T, cbuf) if state_t else (y, hre, him, cbuf)

    if n_steps == 1:
        assert x.shape == (nb, tl, D_MODEL)
        vmem, hbm = pl.BlockSpec(memory_space=pltpu.VMEM), pl.BlockSpec(memory_space=pl.ANY)
        in_specs = [vmem] * len(operands)
        out_specs = [vmem] * len(out_shape)
        if hist_dma:
            in_specs[3] = hbm
            scratch += [pltpu.SemaphoreType.DMA((1,))]
        return untransposed(pl.pallas_call(
            kernel, out_shape=out_shape, in_specs=in_specs, out_specs=tuple(out_specs),
            scratch_shapes=scratch, name=name,
            compiler_params=pltpu.CompilerParams(vmem_limit_bytes=VMEM_LIMIT_BYTES),
        )(*operands))

    assert x.shape == (nb, tl * n_steps, D_MODEL)

    def whole(a):
        zeros = (0,) * a.ndim
        return pl.BlockSpec(a.shape, lambda i: zeros, pipeline_mode=pl.Buffered(1))

    x_spec = pl.BlockSpec((nb, tl, D_MODEL), lambda i: (0, i, 0))
    in_specs = [x_spec] + [whole(a) for a in operands[1:]]
    out_specs = (
        x_spec,
        pl.BlockSpec(state_shape, lambda i: (0, 0)),
        pl.BlockSpec(state_shape, lambda i: (0, 0)),
        pl.BlockSpec((hist, W_CONV), lambda i: (0, 0)),
    )
    return untransposed(pl.pallas_call(
        kernel, out_shape=out_shape, grid=(n_steps,), in_specs=in_specs, out_specs=out_specs,
        scratch_shapes=scratch, name=name,
        compiler_params=pltpu.CompilerParams(dimension_semantics=("arbitrary",),
                                             vmem_limit_bytes=VMEM_LIMIT_BYTES),
    )(*operands))


def _prep_call(lam_re, lam_im, log_dt, b_re, b_im, c_re, c_im):
    gp = N_GROUPS * GROUP_P
    ldt_n = jnp.broadcast_to(log_dt[:, None], (N_GROUPS, N_STATE))
    lam = (lam_re, lam_im, ldt_n)
    bt = tuple(b.transpose(0, 2, 1).reshape(gp, N_STATE) for b in (b_re, b_im))
    cg = tuple(c.reshape(gp, N_STATE) for c in (c_re, c_im))
    out_shape = (
        jax.ShapeDtypeStruct((N_BLOCKS, 2 * LANES, 2 * BLOCK_STATE), BF16),
        jax.ShapeDtypeStruct((N_BLOCKS, 2 * BLOCK_STATE, 2 * LANES), BF16),
        jax.ShapeDtypeStruct((N_BLOCKS, 2 * LANES, 2 * LANES), BF16),
        jax.ShapeDtypeStruct((SUBLANES, N_STATE_ALL), F32),
        jax.ShapeDtypeStruct((SUBLANES, N_STATE_ALL), F32),
    )
    return pl.pallas_call(_prep_kernel, out_shape=out_shape, name="s5_prep")(*lam, *bt, *cg)


def kernel(x_prompt, x_sample, state_ssm_re, state_ssm_im, state_conv, w_in, b_in, lam_re, lam_im, log_dt, b_re, b_im, c_re, c_im, d_skip, w_glu, b_glu, w_dw, b_dw, g_conv_ln, b_conv_ln, w_pw2, b_pw2, w_out, g_post, b_post):
    depth = w_in.shape[0]
    alpha = (2.0 * depth) ** 0.25
    bsz, seq, _ = x_prompt.shape
    dbsz, dseq, _ = x_sample.shape
    tl_prompt = 128

    hp = x_prompt
    hs = x_sample
    outs = {k: [] for k in ("re_p", "im_p", "cv_p", "re_s", "im_s", "cv_s")}
    row = lambda a: a.reshape(1, -1)
    for layer in range(depth):
        s_in, s_out, s_dir, l2re, l2im = _prep_call(
            lam_re[layer], lam_im[layer], log_dt[layer], b_re[layer], b_im[layer],
            c_re[layer], c_im[layer])
        wdw8 = jnp.broadcast_to(w_dw[layer][:, None, :], (CONV_K, SUBLANES, W_CONV))
        consts = (w_in[layer].astype(BF16), row(b_in[layer]), l2re, l2im, s_in, s_out, s_dir,
                  row(d_skip[layer]), w_glu[layer].astype(BF16), row(b_glu[layer]), wdw8,
                  row(b_dw[layer]), row(g_conv_ln[layer]), row(b_conv_ln[layer]),
                  w_pw2[layer].astype(BF16), row(b_pw2[layer]), w_out[layer].astype(BF16),
                  row(g_post[layer]), row(b_post[layer]))

        hp, re_p, im_p, cv_p = _layer_call(
            hp, None, consts, nb=bsz, tl=tl_prompt,
            n_steps=seq // tl_prompt, alpha=alpha, conv_steps=8, name="layer_prompt")

        h0re = state_ssm_re[layer].reshape(dbsz, N_STATE_ALL)
        h0im = state_ssm_im[layer].reshape(dbsz, N_STATE_ALL)
        cbuf0 = state_conv[layer].transpose(1, 0, 2).reshape((CONV_K - 1) * dbsz, W_CONV)
        hs, re_s, im_s, cv_s = _layer_call(
            hs, (h0re, h0im, cbuf0), consts, nb=dbsz, tl=dseq, n_steps=1, alpha=alpha,
            conv_steps=dseq, name="layer_sample")

        unrow = lambda a, n: a.reshape(CONV_K - 1, n, W_CONV).transpose(1, 0, 2)
        outs["re_p"].append(re_p.reshape(bsz, N_GROUPS, N_STATE))
        outs["im_p"].append(im_p.reshape(bsz, N_GROUPS, N_STATE))
        outs["cv_p"].append(unrow(cv_p, bsz))
        outs["re_s"].append(re_s.reshape(dbsz, N_GROUPS, N_STATE))
        outs["im_s"].append(im_s.reshape(dbsz, N_GROUPS, N_STATE))
        outs["cv_s"].append(unrow(cv_s, dbsz))

    return (hp, hs,
            jnp.stack(outs["re_p"]), jnp.stack(outs["im_p"]), jnp.stack(outs["cv_p"]),
            jnp.stack(outs["re_s"]), jnp.stack(outs["im_s"]), jnp.stack(outs["cv_s"]))
```
